```python
import math
import jax, jax.numpy as jnp
from jax import lax
import numpy as np

D_MODEL = 1024
BATCH = 8
SEQ = 2048
DEPTH = 4
DEC_BATCH = 128
DEC_SEQ = 4
PAST_LEN = 2048
PAGE_SIZE = 128

HEAD_DIM = 64
MOBA_HEADS = 8
MOBA_KV_HEADS = 4
MOBA_BLOCK = 256
MOBA_TOPK = 3
SSM_HEADS = 8
SSM_HEAD_DIM = 64
SSM_INNER = 512
SSM_GROUPS = 2
SSM_STATE = 128
SSM_CONV = 4
SSM_CHUNK = 128
SSM_CONV_DIM = 1024
NSA_HEADS = 8
NSA_KV_HEADS = 2
NSA_CMP_LEN = 32
NSA_CMP_STRIDE = 16
NSA_CMP_HIDDEN = 256
NSA_SLC_BLOCK = 64
NSA_TOPN = 16
NSA_WINDOW = 512
N_BRANCHES = 3
BRANCH_WIDTH = 512
D_FF = 2816
N_EXPERTS = 8
TOP_K_EXPERTS = 2
D_FF_EXPERT = 2816
QUERY_CHUNK = 128
RMS_EPS = 1e-6
NEG_BIG = -1e30
SPLIT_SIZES = (MOBA_HEADS * HEAD_DIM, MOBA_KV_HEADS * HEAD_DIM, MOBA_KV_HEADS * HEAD_DIM,
               SSM_INNER, SSM_CONV_DIM, SSM_HEADS,
               NSA_HEADS * HEAD_DIM, 6 * NSA_KV_HEADS * HEAD_DIM, 3 * NSA_HEADS,
               N_BRANCHES * D_MODEL)

kernel_name = "hybrid_moba_ssd_nsa_decoder_step"

F32 = jnp.float32


def rmsnorm(x, g):
    xf = x.astype(F32)
    y = xf * lax.rsqrt(jnp.mean(xf * xf, axis=-1, keepdims=True) + RMS_EPS)
    return (y * g.astype(F32)).astype(x.dtype)


def pad_time(a, n):
    return jnp.pad(a, [(0, 0), (0, n)] + [(0, 0)] * (a.ndim - 2))


def masked_softmax(s, valid):
    p = jax.nn.softmax(jnp.where(valid, s.astype(F32), NEG_BIG), axis=-1)
    return p * valid


def gqa_attend(q, k, v, valid):
    s = jnp.einsum('...qhgd,...khd->...hgqk', q, k, preferred_element_type=F32) * HEAD_DIM ** -0.5
    p = masked_softmax(s, valid)
    o = jnp.einsum('...hgqk,...khd->...qhgd', p.astype(v.dtype), v)
    return o, p


def map_query_chunks(fn, per_token, per_seq):
    L = per_token[0].shape[1]
    chunk = math.gcd(L, QUERY_CHUNK)
    n = L // chunk

    def one_seq(args):
        tok, seq = args
        tok = tuple(a.reshape((n, chunk) + a.shape[1:]) for a in tok)
        out = lax.map(lambda t: fn(*t, *seq), tok)
        return out.reshape((L,) + out.shape[2:])

    return lax.map(one_seq, (tuple(per_token), tuple(per_seq)))


def moba_attend_block(q, q_pos, k_blocks, v_blocks, k_mean):
    Q = q.shape[0]
    G = MOBA_HEADS // MOBA_KV_HEADS
    nb = k_blocks.shape[1]
    qg = q.reshape(Q, MOBA_KV_HEADS, G, HEAD_DIM)
    gate = jnp.einsum('qhgd,hnd->qhgn', qg, k_mean, preferred_element_type=F32)
    own = q_pos // MOBA_BLOCK
    eligible = jnp.arange(nb)[None, :] < own[:, None]
    gate = jnp.where(eligible[:, None, None, :], gate, -jnp.inf)
    k_sel = min(MOBA_TOPK, nb)
    _, top = lax.top_k(gate, k_sel)
    top_valid = jnp.arange(k_sel)[None, :] < own[:, None]
    sel = jnp.concatenate([top, jnp.broadcast_to(own[:, None, None, None], (Q, MOBA_KV_HEADS, G, 1))], axis=-1)
    sel_valid = jnp.concatenate([top_valid, jnp.ones((Q, 1), bool)], axis=-1)
    h_idx = jnp.arange(MOBA_KV_HEADS)[None, :, None, None]
    kg = k_blocks[h_idx, sel]
    vg = v_blocks[h_idx, sel]
    k_pos = sel[..., None] * MOBA_BLOCK + jnp.arange(MOBA_BLOCK)
    valid = sel_valid[:, None, None, :, None] & (k_pos <= q_pos[:, None, None, None, None])
    s = jnp.einsum('qhgd,qhgjsd->qhgjs', qg, kg, preferred_element_type=F32) * HEAD_DIM ** -0.5
    p = masked_softmax(s.reshape(Q, MOBA_KV_HEADS, G, -1), valid.reshape(Q, MOBA_KV_HEADS, G, -1))
    o = jnp.einsum('qhgjs,qhgjsd->qhgd', p.reshape(s.shape).astype(vg.dtype), vg)
    return o.reshape(Q, MOBA_HEADS * HEAD_DIM)


def moba_mixer(q, k, v, past_kv, q_pos):
    B = q.shape[0]
    new_kv = jnp.stack([k, v], axis=2)
    kv = jnp.concatenate([past_kv.astype(new_kv.dtype), new_kv], axis=1)
    T = kv.shape[1]
    nb = -(-T // MOBA_BLOCK)
    kv = pad_time(kv, nb * MOBA_BLOCK - T)
    kvb = kv.reshape(B, nb, MOBA_BLOCK, 2, MOBA_KV_HEADS, HEAD_DIM).transpose(3, 0, 4, 1, 2, 5)
    k_blocks, v_blocks = kvb[0], kvb[1]
    k_mean = jnp.mean(k_blocks.astype(F32), axis=3)
    out = map_query_chunks(moba_attend_block, (q, q_pos), (k_blocks, v_blocks, k_mean))
    return out, new_kv


def ssd_scan(x, dt, A, Bg, Cg, h0):
    x, dt, Bg, Cg, h0 = (a.astype(F32) for a in (x, dt, Bg, Cg, h0))
    Bsz, L = x.shape[:2]
    Q = min(SSM_CHUNK, L)
    nc = -(-L // Q)
    pad = nc * Q - L
    x, dt, Bg, Cg = (pad_time(a, pad) for a in (x, dt, Bg, Cg))
    rep = SSM_HEADS // SSM_GROUPS
    Bh = jnp.repeat(Bg, rep, axis=2).reshape(Bsz, nc, Q, SSM_HEADS, SSM_STATE)
    Ch = jnp.repeat(Cg, rep, axis=2).reshape(Bsz, nc, Q, SSM_HEADS, SSM_STATE)
    x = x.reshape(Bsz, nc, Q, SSM_HEADS, SSM_HEAD_DIM)
    dt = dt.reshape(Bsz, nc, Q, SSM_HEADS)
    a_cs = jnp.cumsum(dt * A, axis=2)
    xdt = x * dt[..., None]
    causal = jnp.tril(jnp.ones((Q, Q), bool))
    seg = a_cs[:, :, :, None, :] - a_cs[:, :, None, :, :]
    decay = jnp.exp(jnp.where(causal[None, None, :, :, None], seg, -jnp.inf))
    cb = jnp.einsum('bcthn,bcshn->bctsh', Ch, Bh) * decay
    y_diag = jnp.einsum('bctsh,bcshp->bcthp', cb, xdt)
    decay_end = jnp.exp(a_cs[:, :, -1:, :] - a_cs)
    chunk_states = jnp.einsum('bcshn,bcsh,bcshp->bchpn', Bh, decay_end, xdt)
    chunk_decay = jnp.exp(a_cs[:, :, -1, :])

    def step(h, inp):
        dec, st = inp
        return dec[:, :, None, None] * h + st, h

    h_final, h_in = lax.scan(step, h0, (chunk_decay.transpose(1, 0, 2), chunk_states.transpose(1, 0, 2, 3, 4)))
    h_in = h_in.transpose(1, 0, 2, 3, 4)
    y_off = jnp.einsum('bcthn,bchpn,bcth->bcthp', Ch, h_in, jnp.exp(a_cs))
    y = (y_diag + y_off).reshape(Bsz, nc * Q, SSM_HEADS, SSM_HEAD_DIM)[:, :L]
    return y, h_final


def mamba_mixer(z, xbc, dt_raw, conv_state, ssm_state, conv_w, conv_b, dt_bias, a_log, d_skip, g_ssm):
    B, L = z.shape[:2]
    xin = jnp.concatenate([conv_state.astype(xbc.dtype), xbc], axis=1)
    conv = conv_b + sum(xin[:, w:w + L] * conv_w[w] for w in range(SSM_CONV))
    new_conv = xin[:, -(SSM_CONV - 1):]
    act = jax.nn.silu(conv)
    xs = act[..., :SSM_INNER].reshape(B, L, SSM_HEADS, SSM_HEAD_DIM)
    Bg = act[..., SSM_INNER:SSM_INNER + SSM_GROUPS * SSM_STATE].reshape(B, L, SSM_GROUPS, SSM_STATE)
    Cg = act[..., SSM_INNER + SSM_GROUPS * SSM_STATE:].reshape(B, L, SSM_GROUPS, SSM_STATE)
    dt = jax.nn.softplus(dt_raw.astype(F32) + dt_bias.astype(F32))
    A = -jnp.exp(a_log.astype(F32))
    y, new_ssm = ssd_scan(xs, dt, A, Bg, Cg, ssm_state)
    y = y + xs.astype(F32) * d_skip.astype(F32)[:, None]
    y = y.reshape(B, L, SSM_INNER).astype(z.dtype)
    y = rmsnorm(y * jax.nn.silu(z), g_ssm)
    return y, new_conv, new_ssm.astype(z.dtype)


def compress_tokens(rows, pe, w1, b1, w2, b2):
    B, T = rows.shape[:2]
    nc = (T - NSA_CMP_LEN) // NSA_CMP_STRIDE + 1
    start = jnp.arange(nc) * NSA_CMP_STRIDE
    idx = start[:, None] + jnp.arange(NSA_CMP_LEN)[None, :]
    blk = rows[:, idx] + pe[:, None, :]
    flat = blk.transpose(0, 1, 3, 2, 4).reshape(B, nc, NSA_KV_HEADS, NSA_CMP_LEN * HEAD_DIM)
    hid = jax.nn.silu(flat @ w1 + b1)
    return hid @ w2 + b2, start + NSA_CMP_LEN - 1


def nsa_select_block(q, q_pos, p_slc, k_blocks, v_blocks):
    Q = q.shape[0]
    G = NSA_HEADS // NSA_KV_HEADS
    ns = k_blocks.shape[1]
    cur = q_pos // NSA_SLC_BLOCK
    j = jnp.arange(ns)[None, :]
    forced = (j == 0) | (j == cur[:, None]) | (j == cur[:, None] - 1)
    eligible = j <= cur[:, None]
    score = jnp.where(forced[:, None, :], jnp.inf, p_slc.astype(F32))
    score = jnp.where(eligible[:, None, :], score, -jnp.inf)
    n_sel = min(NSA_TOPN, ns)
    _, top = lax.top_k(score, n_sel)
    sel_valid = jnp.arange(n_sel)[None, :] < (cur + 1)[:, None]
    h_idx = jnp.arange(NSA_KV_HEADS)[None, :, None]
    kg = k_blocks[h_idx, top]
    vg = v_blocks[h_idx, top]
    k_pos = top[..., None] * NSA_SLC_BLOCK + jnp.arange(NSA_SLC_BLOCK)
    valid = sel_valid[:, None, :, None] & (k_pos <= q_pos[:, None, None, None])
    s = jnp.einsum('qhgd,qhnsd->qhgns', q, kg, preferred_element_type=F32) * HEAD_DIM ** -0.5
    p = masked_softmax(s.reshape(Q, NSA_KV_HEADS, G, -1), valid.reshape(Q, NSA_KV_HEADS, 1, -1))
    return jnp.einsum('qhgns,qhnsd->qhgd', p.reshape(s.shape).astype(vg.dtype), vg)


def nsa_mixer(q, kv6, gates, past_kv, past_win, q_pos, win_len, pe, w1, b1, w2, b2):
    B, L = q.shape[:2]
    G = NSA_HEADS // NSA_KV_HEADS
    new_kv = kv6[:, :, :4]
    kv = jnp.concatenate([past_kv.astype(new_kv.dtype), new_kv], axis=1)
    T = kv.shape[1]
    kc, end_pos = compress_tokens(kv[:, :, 0], pe[0], w1[0], b1[0], w2[0], b2[0])
    vc, _ = compress_tokens(kv[:, :, 1], pe[1], w1[1], b1[1], w2[1], b2[1])
    valid_c = end_pos[None, None, :] <= q_pos[:, :, None]
    o_cmp, p_cmp = gqa_attend(q, kc, vc, valid_c[:, None, None])
    ns = -(-T // NSA_SLC_BLOCK)
    slc = pad_time(kv[:, :, 2:4], ns * NSA_SLC_BLOCK - T)
    slc = slc.reshape(B, ns, NSA_SLC_BLOCK, 2, NSA_KV_HEADS, HEAD_DIM).transpose(3, 0, 4, 1, 2, 5)
    c_start = jnp.arange(kc.shape[1]) * NSA_CMP_STRIDE
    s_start = jnp.arange(ns) * NSA_SLC_BLOCK
    overlap = ((c_start[:, None] < s_start[None, :] + NSA_SLC_BLOCK)
               & (c_start[:, None] + NSA_CMP_LEN > s_start[None, :])).astype(F32)
    p_slc = jnp.einsum('bhgqc,cn->bqhn', p_cmp, overlap)
    o_slc = map_query_chunks(nsa_select_block, (q, q_pos, p_slc), (slc[0], slc[1]))
    win = jnp.concatenate([past_win.astype(kv6.dtype), kv6[:, :, 4:]], axis=1)
    Wp = past_win.shape[1]
    winp = jnp.pad(win, [(0, 0), (NSA_WINDOW, 0)] + [(0, 0)] * (win.ndim - 2))
    cq = math.gcd(L, QUERY_CHUNK)
    n_q = L // cq
    idx = Wp + jnp.arange(n_q)[:, None] * cq + jnp.arange(NSA_WINDOW + cq)[None, :]
    band = winp[:, idx]
    qc = q.reshape(B, n_q, cq, NSA_KV_HEADS, G, HEAD_DIM)
    qpc = q_pos.reshape(B, n_q, cq)
    k_pos = qpc[:, :, :1] - NSA_WINDOW + jnp.arange(NSA_WINDOW + cq)
    dist = qpc[..., :, None] - k_pos[..., None, :]
    valid_w = (dist >= 0) & (dist <= NSA_WINDOW) & (k_pos[..., None, :] >= 0)
    o_win, _ = gqa_attend(qc, band[:, :, :, 0], band[:, :, :, 1], valid_w[:, :, None, None])
    o_win = o_win.reshape(B, L, NSA_KV_HEADS, G, HEAD_DIM)
    g = jax.nn.sigmoid(gates.astype(F32)).astype(q.dtype).reshape(B, L, 3, NSA_KV_HEADS, G, 1)
    o = g[:, :, 0] * o_cmp + g[:, :, 1] * o_slc + g[:, :, 2] * o_win
    return o.reshape(B, L, NSA_HEADS * HEAD_DIM), new_kv, winp[:, -win_len:]


def swiglu(h, wg, wu, wd):
    return (jax.nn.silu(h @ wg) * (h @ wu)) @ wd


def moe_swiglu(h, w_router, wg, wu, wd):
    logits = (h @ w_router).astype(F32)
    top_val, top_idx = lax.top_k(logits, TOP_K_EXPERTS)
    probs = jax.nn.softmax(top_val, axis=-1)
    combine = jnp.sum(jax.nn.one_hot(top_idx, N_EXPERTS, dtype=F32) * probs[..., None], axis=-2)
    out = jnp.zeros_like(h)
    for e in range(N_EXPERTS):
        out = out + combine[..., e, None].astype(h.dtype) * swiglu(h, wg[e], wu[e], wd[e])
    return out


def trunk_layer(x, c, q_pos, past_moba, past_nsa, past_win, conv_state, ssm_state, win_len,
                w_ada, b_ada, g_mix, w_in, conv_w, conv_b, dt_bias, a_log, d_skip, g_ssm,
                cmp_pe, cmp_w1, cmp_b1, cmp_w2, cmp_b2, w_branch, w_out, g_ffn, ffn_weights):
    B, L, D = x.shape
    mod = (jax.nn.silu(c) @ w_ada + b_ada).reshape(B, 6, 1, D)
    shift1, scale1, gate1, shift2, scale2, gate2 = (mod[:, k] for k in range(6))
    h = rmsnorm(x, g_mix) * (1 + scale1) + shift1
    proj = h @ w_in
    split_at = np.cumsum(SPLIT_SIZES)[:-1].tolist()
    mq, mk, mv, sz, sxbc, sdt, nq, nkv, ngate, mgate = jnp.split(proj, split_at, axis=-1)
    moba_out, new_moba = moba_mixer(mq.reshape(B, L, MOBA_HEADS, HEAD_DIM),
                                    mk.reshape(B, L, MOBA_KV_HEADS, HEAD_DIM),
                                    mv.reshape(B, L, MOBA_KV_HEADS, HEAD_DIM), past_moba, q_pos)
    ssm_out, new_conv, new_ssm = mamba_mixer(sz, sxbc, sdt, conv_state, ssm_state,
                                             conv_w, conv_b, dt_bias, a_log, d_skip, g_ssm)
    nsa_out, new_nsa, new_win = nsa_mixer(
        nq.reshape(B, L, NSA_KV_HEADS, NSA_HEADS // NSA_KV_HEADS, HEAD_DIM),
        nkv.reshape(B, L, 6, NSA_KV_HEADS, HEAD_DIM), ngate.reshape(B, L, 3, NSA_HEADS),
        past_nsa, past_win, q_pos, win_len, cmp_pe, cmp_w1, cmp_b1, cmp_w2, cmp_b2)
    branches = jnp.stack([moba_out, ssm_out, nsa_out], axis=2)
    up = jnp.einsum('blnw,nwd->blnd', branches, w_branch)
    merge_gate = jax.nn.sigmoid(mgate.reshape(B, L, N_BRANCHES, D).astype(F32)).astype(x.dtype)
    x = x + gate1 * (jnp.sum(merge_gate * up, axis=2) @ w_out)
    h2 = rmsnorm(x, g_ffn) * (1 + scale2) + shift2
    f = moe_swiglu(h2, *ffn_weights) if len(ffn_weights) == 4 else swiglu(h2, *ffn_weights)
    x = x + gate2 * f
    return x, (new_moba, new_nsa, new_win, new_ssm, new_conv)


def setup_inputs(seed: int = 0) -> dict:
    key = jax.random.key(seed)
    keys = jax.random.split(key, 64)
    counter = iter(range(64))

    def nk():
        return keys[next(counter)]

    def nrm(shape, scale):
        return jax.random.normal(nk(), shape, F32) * scale

    n_pages = PAST_LEN // PAGE_SIZE
    n_used = DEC_BATCH * n_pages
    n_phys = n_used + n_used // 4
    win_len = min(NSA_WINDOW, PAST_LEN)
    n_dense = (DEPTH + 1) // 2
    n_moe = DEPTH // 2
    proj_width = sum(SPLIT_SIZES)
    D = D_MODEL
    page_table = jax.random.permutation(nk(), n_phys)[:n_used].reshape(DEC_BATCH, n_pages).astype(jnp.int32)
    dt0 = jnp.exp(jax.random.uniform(nk(), (DEPTH, SSM_HEADS), F32) * (math.log(0.1) - math.log(0.001)) + math.log(0.001))
    dt_bias = dt0 + jnp.log(-jnp.expm1(-dt0))
    a_log = jnp.log(jax.random.uniform(nk(), (DEPTH, SSM_HEADS), F32, 1.0, 16.0))
    return {
        "x_prompt": nrm((BATCH, SEQ, D), 1.0),
        "x_sample": nrm((DEC_BATCH, DEC_SEQ, D), 1.0),
        "c_prompt": nrm((BATCH, D), 1.0),
        "c_sample": nrm((DEC_BATCH, D), 1.0),
        "cache_moba_kv": nrm((DEPTH, n_phys, PAGE_SIZE, 2, MOBA_KV_HEADS, HEAD_DIM), 1.0),
        "cache_nsa_kv": nrm((DEPTH, n_phys, PAGE_SIZE, 4, NSA_KV_HEADS, HEAD_DIM), 1.0),
        "state_nsa_win_kv": nrm((DEPTH, DEC_BATCH, win_len, 2, NSA_KV_HEADS, HEAD_DIM), 1.0),
        "state_ssm": nrm((DEPTH, DEC_BATCH, SSM_HEADS, SSM_HEAD_DIM, SSM_STATE), 0.5),
        "state_conv": nrm((DEPTH, DEC_BATCH, SSM_CONV - 1, SSM_CONV_DIM), 1.0),
        "page_table": page_table,
        "w_ada": nrm((DEPTH, D, 6 * D), 0.5 * D ** -0.5),
        "b_ada": nrm((DEPTH, 6 * D), 0.02),
        "g_mix": 1.0 + nrm((DEPTH, D), 0.05),
        "w_in": nrm((DEPTH, D, proj_width), D ** -0.5),
        "conv_w": nrm((DEPTH, SSM_CONV, SSM_CONV_DIM), SSM_CONV ** -0.5),
        "conv_b": nrm((DEPTH, SSM_CONV_DIM), 0.02),
        "dt_bias": dt_bias,
        "a_log": a_log,
        "d_skip": 1.0 + nrm((DEPTH, SSM_HEADS), 0.05),
        "g_ssm": 1.0 + nrm((DEPTH, SSM_INNER), 0.05),
        "cmp_pe": nrm((DEPTH, 2, NSA_CMP_LEN, HEAD_DIM), 0.1),
        "cmp_w1": nrm((DEPTH, 2, NSA_CMP_LEN * HEAD_DIM, NSA_CMP_HIDDEN), (NSA_CMP_LEN * HEAD_DIM) ** -0.5),
        "cmp_b1": nrm((DEPTH, 2, NSA_CMP_HIDDEN), 0.02),
        "cmp_w2": nrm((DEPTH, 2, NSA_CMP_HIDDEN, HEAD_DIM), NSA_CMP_HIDDEN ** -0.5),
        "cmp_b2": nrm((DEPTH, 2, HEAD_DIM), 0.02),
        "w_branch": nrm((DEPTH, N_BRANCHES, BRANCH_WIDTH, D), BRANCH_WIDTH ** -0.5),
        "w_out": nrm((DEPTH, D, D), D ** -0.5),
        "g_ffn": 1.0 + nrm((DEPTH, D), 0.05),
        "w_ffn_gate": nrm((n_dense, D, D_FF), D ** -0.5),
        "w_ffn_up": nrm((n_dense, D, D_FF), D ** -0.5),
        "w_ffn_down": nrm((n_dense, D_FF, D), D_FF ** -0.5),
        "w_router": nrm((n_moe, D, N_EXPERTS), D ** -0.5),
        "w_exp_gate": nrm((n_moe, N_EXPERTS, D, D_FF_EXPERT), D ** -0.5),
        "w_exp_up": nrm((n_moe, N_EXPERTS, D, D_FF_EXPERT), D ** -0.5),
        "w_exp_down": nrm((n_moe, N_EXPERTS, D_FF_EXPERT, D), D_FF_EXPERT ** -0.5),
        "g_final": 1.0 + nrm((D,), 0.05),
    }


def reference(x_prompt, x_sample, c_prompt, c_sample, cache_moba_kv, cache_nsa_kv, state_nsa_win_kv,
              state_ssm, state_conv, page_table, w_ada, b_ada, g_mix, w_in, conv_w, conv_b, dt_bias,
              a_log, d_skip, g_ssm, cmp_pe, cmp_w1, cmp_b1, cmp_w2, cmp_b2, w_branch, w_out, g_ffn,
              w_ffn_gate, w_ffn_up, w_ffn_down, w_router, w_exp_gate, w_exp_up, w_exp_down, g_final):
    B, S = x_prompt.shape[:2]
    DB, DS = x_sample.shape[:2]
    n_pages = page_table.shape[1]
    past_len = n_pages * cache_moba_kv.shape[2]
    win_len = state_nsa_win_kv.shape[2]
    dtype = x_prompt.dtype
    pos_p = jnp.broadcast_to(jnp.arange(S, dtype=jnp.int32)[None, :], (B, S))
    pos_s = jnp.broadcast_to(past_len + jnp.arange(DS, dtype=jnp.int32)[None, :], (DB, DS))
    e_moba = jnp.zeros((B, 0, 2, MOBA_KV_HEADS, HEAD_DIM), dtype)
    e_nsa = jnp.zeros((B, 0, 4, NSA_KV_HEADS, HEAD_DIM), dtype)
    e_win = jnp.zeros((B, 0, 2, NSA_KV_HEADS, HEAD_DIM), dtype)
    z_conv = jnp.zeros((B, SSM_CONV - 1, SSM_CONV_DIM), dtype)
    z_ssm = jnp.zeros((B, SSM_HEADS, SSM_HEAD_DIM, SSM_STATE), dtype)
    hp, hs = x_prompt, x_sample
    st_prompt = [[] for _ in range(5)]
    st_sample = [[] for _ in range(5)]
    for i in range(DEPTH):
        if i % 2:
            ffn_w = (w_router[i // 2], w_exp_gate[i // 2], w_exp_up[i // 2], w_exp_down[i // 2])
        else:
            ffn_w = (w_ffn_gate[i // 2], w_ffn_up[i // 2], w_ffn_down[i // 2])
        layer_w = (w_ada[i], b_ada[i], g_mix[i], w_in[i], conv_w[i], conv_b[i], dt_bias[i], a_log[i],
                   d_skip[i], g_ssm[i], cmp_pe[i], cmp_w1[i], cmp_b1[i], cmp_w2[i], cmp_b2[i],
                   w_branch[i], w_out[i], g_ffn[i], ffn_w)
        hp, new_p = trunk_layer(hp, c_prompt, pos_p, e_moba, e_nsa, e_win, z_conv, z_ssm, win_len, *layer_w)
        past_moba = cache_moba_kv[i][page_table].reshape(DB, past_len, 2, MOBA_KV_HEADS, HEAD_DIM)
        past_nsa = cache_nsa_kv[i][page_table].reshape(DB, past_len, 4, NSA_KV_HEADS, HEAD_DIM)
        hs, new_s = trunk_layer(hs, c_sample, pos_s, past_moba, past_nsa, state_nsa_win_kv[i],
                                state_conv[i], state_ssm[i], win_len, *layer_w)
        for lst, v in zip(st_prompt, new_p):
            lst.append(v)
        for lst, v in zip(st_sample, new_s):
            lst.append(v)
    y_prompt = rmsnorm(hp, g_final)
    y_sample = rmsnorm(hs, g_final)
    new_moba_kv_prompt = jnp.stack(st_prompt[0])
    new_moba_kv_sample = jnp.stack(st_sample[0])
    new_nsa_kv_prompt = jnp.stack(st_prompt[1])
    new_nsa_kv_sample = jnp.stack(st_sample[1])
    new_win_kv_prompt = jnp.stack(st_prompt[2])
    new_win_kv_sample = jnp.stack(st_sample[2])
    new_ssm_prompt = jnp.stack(st_prompt[3])
    new_ssm_sample = jnp.stack(st_sample[3])
    new_conv_prompt = jnp.stack(st_prompt[4])
    new_conv_sample = jnp.stack(st_sample[4])
    return (y_prompt, y_sample, new_moba_kv_prompt, new_moba_kv_sample, new_nsa_kv_prompt, new_nsa_kv_sample,
            new_win_kv_prompt, new_win_kv_sample, new_ssm_prompt, new_ssm_sample, new_conv_prompt, new_conv_sample)
```

```python
import functools

import numpy as np
import jax
import jax.numpy as jnp
from jax import lax
from jax.experimental import pallas as pl
from jax.experimental.pallas import tpu as pltpu

F32 = jnp.float32
BF16 = jnp.bfloat16
I32 = jnp.int32

D_MODEL = 1024
HEAD_DIM = 64
MOBA_HEADS, MOBA_KV_HEADS, MOBA_BLOCK, MOBA_TOPK = 8, 4, 256, 3
SSM_HEADS, SSM_HEAD_DIM, SSM_INNER, SSM_GROUPS, SSM_STATE, SSM_CONV, SSM_CHUNK = 8, 64, 512, 2, 128, 4, 128
SSM_CONV_DIM = SSM_INNER + 2 * SSM_GROUPS * SSM_STATE
NSA_HEADS, NSA_KV_HEADS = 8, 2
NSA_CMP_LEN, NSA_CMP_STRIDE, NSA_CMP_HIDDEN = 32, 16, 256
NSA_SLC_BLOCK, NSA_TOPN, NSA_WINDOW = 64, 16, 512
N_BRANCHES, BRANCH_WIDTH = 3, 512
N_EXPERTS = 8
RMS_EPS = 1e-6
NEG_BIG = -1e30
ATT_SCALE = HEAD_DIM ** -0.5

LANES = 128
SUBLANES = 8
VMEM_LIMIT = 56 * 1024 * 1024

T_MQ, T_MK, T_MV, T_NQ, T_NKV, T_NGATE, T_DT = 0, 512, 768, 1024, 1536, 2304, 2328
T_ROWS = 2560
R_MGATE, R_XBC, R_Z = 0, 3072, 4096
R_COLS = 4608


def _cparams(sem):
    return pltpu.CompilerParams(dimension_semantics=sem, vmem_limit_bytes=VMEM_LIMIT)


def _silu(x):
    return x * (1.0 / (1.0 + jnp.exp(-x)))


def _sigmoid(x):
    return 1.0 / (1.0 + jnp.exp(-x))


def _softplus(x):
    return jnp.maximum(x, 0.0) + jnp.log(1.0 + jnp.exp(-jnp.abs(x)))


def _dot(a, b):
    return jnp.dot(a, b, preferred_element_type=F32)


def _dot_nt(a, b):
    return lax.dot_general(a, b, (((1,), (1,)), ((), ())), preferred_element_type=F32)


def _dot_tn(a, b):
    return lax.dot_general(a, b, (((0,), (0,)), ((), ())), preferred_element_type=F32)


def _dot_exact(a, b):
    return jnp.dot(a, b, preferred_element_type=F32, precision=lax.Precision.HIGHEST)


def _f01(mask):
    return jnp.where(mask, 1.0, 0.0)


def _mod_rows(ref, rows):
    m = ref[...]
    return m if m.shape[0] == rows else m[0:1]


def _rms(x):
    return x * lax.rsqrt(jnp.mean(x * x, axis=-1, keepdims=True) + RMS_EPS)


def _mod_spec(grp, chunk, tm):
    if grp["per_token_mod"]:
        return pl.BlockSpec((tm, D_MODEL), lambda i, *_: (i, chunk))
    tiles_per_seq = grp["L"] // tm
    return pl.BlockSpec((SUBLANES, D_MODEL), lambda i, *_: (i // tiles_per_seq, chunk))


def _ada_kernel(c_ref, w_ref, b_ref, o_ref):
    a = _silu(c_ref[...]).astype(BF16)
    o_ref[0] = _dot(a, w_ref[0].astype(BF16)) + b_ref[0]


def ada_modulation(c_all, w_ada, b_ada):
    depth, d, n6 = w_ada.shape
    rows = c_all.shape[0]
    tn = 1536
    return pl.pallas_call(
        _ada_kernel,
        grid=(depth, n6 // tn),
        in_specs=[pl.BlockSpec((rows, d), lambda l, j: (0, 0)),
                  pl.BlockSpec((1, d, tn), lambda l, j: (l, 0, j)),
                  pl.BlockSpec((1, 1, tn), lambda l, j: (l, 0, j))],
        out_specs=pl.BlockSpec((1, rows, tn), lambda l, j: (l, 0, j)),
        out_shape=jax.ShapeDtypeStruct((depth, rows, n6), F32),
        compiler_params=_cparams(("arbitrary", "arbitrary")),
        name="ada_modulation",
    )(c_all, w_ada, b_ada.reshape(depth, 1, n6))


def _norm_kernel(x_ref, g_ref, sc_ref, sh_ref, o_ref):
    x = x_ref[...]
    rows = x.shape[0]
    y = _rms(x) * g_ref[...]
    o_ref[...] = (y * (1.0 + _mod_rows(sc_ref, rows)) + _mod_rows(sh_ref, rows)).astype(o_ref.dtype)


def norm_modulate(grp, x, g, mod):
    n, d = x.shape
    tm = grp["tm"]
    return pl.pallas_call(
        _norm_kernel,
        grid=(n // tm,),
        in_specs=[pl.BlockSpec((tm, d), lambda i: (i, 0)),
                  pl.BlockSpec((1, d), lambda i: (0, 0)),
                  _mod_spec(grp, 1, tm), _mod_spec(grp, 0, tm)],
        out_specs=pl.BlockSpec((tm, d), lambda i: (i, 0)),
        out_shape=jax.ShapeDtypeStruct((n, d), BF16),
        compiler_params=_cparams(("arbitrary",)),
        name="norm_modulate",
    )(x, g.reshape(1, d), mod, mod)


def _nt_kernel(a_ref, b_ref, o_ref):
    o_ref[...] = _dot_nt(a_ref[...], b_ref[...])


def nt_matmul(a, b, tm, tn):
    m, k = a.shape
    n = b.shape[0]
    assert m % tm == 0 and n % tn == 0, (a.shape, b.shape, tm, tn)
    return pl.pallas_call(
        _nt_kernel,
        grid=(m // tm, n // tn),
        in_specs=[pl.BlockSpec((tm, k), lambda i, j: (i, 0)),
                  pl.BlockSpec((tn, k), lambda i, j: (j, 0))],
        out_specs=pl.BlockSpec((tm, tn), lambda i, j: (i, j)),
        out_shape=jax.ShapeDtypeStruct((m, n), F32),
        compiler_params=_cparams(("arbitrary", "arbitrary")),
        name="nt_matmul",
    )(a, b)


def _merge_kernel(x_ref, b0_ref, b1_ref, b2_ref, mg_ref, wb_ref, wo_ref, g_ref, gate_ref, sc_ref, sh_ref,
                  x1_ref, h2_ref):
    rows = x_ref.shape[0]
    merged = jnp.zeros((rows, D_MODEL), F32)
    for n, b_ref in enumerate((b0_ref, b1_ref, b2_ref)):
        up = _dot(b_ref[...].astype(BF16), wb_ref[n])
        merged = merged + _sigmoid(mg_ref[:, n * D_MODEL:(n + 1) * D_MODEL]) * up
    y = _dot(merged.astype(BF16), wo_ref[...])
    x1 = x_ref[...] + _mod_rows(gate_ref, rows) * y
    x1_ref[...] = x1
    h2 = _rms(x1) * g_ref[...]
    h2_ref[...] = (h2 * (1.0 + _mod_rows(sc_ref, rows)) + _mod_rows(sh_ref, rows)).astype(BF16)


def merge_branches(grp, x, branches, row_proj, wb, wo, g_ffn, mod):
    n, d = x.shape
    tm = grp["tm"]
    bw = BRANCH_WIDTH
    row = lambda i: (i, 0)
    fixed2 = lambda i: (0, 0)
    return pl.pallas_call(
        _merge_kernel,
        grid=(n // tm,),
        in_specs=[pl.BlockSpec((tm, d), row),
                  pl.BlockSpec((tm, bw), row), pl.BlockSpec((tm, bw), row), pl.BlockSpec((tm, bw), row),
                  pl.BlockSpec((tm, N_BRANCHES * d), row),
                  pl.BlockSpec((N_BRANCHES, bw, d), lambda i: (0, 0, 0)),
                  pl.BlockSpec((d, d), fixed2),
                  pl.BlockSpec((1, d), fixed2),
                  _mod_spec(grp, 2, tm), _mod_spec(grp, 4, tm), _mod_spec(grp, 3, tm)],
        out_specs=[pl.BlockSpec((tm, d), row), pl.BlockSpec((tm, d), row)],
        out_shape=[jax.ShapeDtypeStruct((n, d), F32), jax.ShapeDtypeStruct((n, d), BF16)],
        compiler_params=_cparams(("arbitrary",)),
        name="merge_branches",
    )(x, *branches, row_proj, wb, wo, g_ffn.reshape(1, d), mod, mod, mod)


def _router_kernel(h_ref, w_ref, o_ref):
    logits = _dot_nt(h_ref[...], w_ref[...])
    lane = lax.broadcasted_iota(I32, logits.shape, 1)
    logits = jnp.where(lane < N_EXPERTS, logits, -jnp.inf)
    m1 = jnp.max(logits, axis=-1, keepdims=True)
    i1 = jnp.min(jnp.where(logits == m1, lane, LANES), axis=-1, keepdims=True)
    rest = jnp.where(lane == i1, -jnp.inf, logits)
    m2 = jnp.max(rest, axis=-1, keepdims=True)
    i2 = jnp.min(jnp.where(rest == m2, lane, LANES), axis=-1, keepdims=True)
    e2 = jnp.exp(m2 - m1)
    den = 1.0 + e2
    o_ref[...] = jnp.where(lane == i1, 1.0 / den, 0.0) + jnp.where(lane == i2, e2 / den, 0.0)


def moe_router(grp, h2, w_router_t):
    n, d = h2.shape
    tm = grp["tm"]
    return pl.pallas_call(
        _router_kernel,
        grid=(n // tm,),
        in_specs=[pl.BlockSpec((tm, d), lambda i: (i, 0)), pl.BlockSpec((LANES, d), lambda i: (0, 0))],
        out_specs=pl.BlockSpec((tm, LANES), lambda i: (i, 0)),
        out_shape=jax.ShapeDtypeStruct((n, LANES), F32),
        compiler_params=_cparams(("arbitrary",)),
        name="moe_router",
    )(h2, w_router_t)


def _ffn_kernel(x_ref, h_ref, cmb_ref, wg_ref, wu_ref, wd_ref, gate_ref, g_ref, sc_ref, sh_ref,
                x2_ref, hn_ref, acc_ref):
    e, f = pl.program_id(1), pl.program_id(2)
    rows = x_ref.shape[0]

    @pl.when((e == 0) & (f == 0))
    def _():
        acc_ref[...] = jnp.zeros_like(acc_ref)

    h = h_ref[...]
    a = _silu(_dot(h, wg_ref[0])) * _dot(h, wu_ref[0])
    part = _dot(a.astype(BF16), wd_ref[0])
    cmb = cmb_ref[...]
    lane = lax.broadcasted_iota(I32, cmb.shape, 1)
    w = jnp.sum(jnp.where(lane == e, cmb, 0.0), axis=-1, keepdims=True)
    acc_ref[...] += w * part

    @pl.when((e == pl.num_programs(1) - 1) & (f == pl.num_programs(2) - 1))
    def _():
        x2 = x_ref[...] + _mod_rows(gate_ref, rows) * acc_ref[...]
        x2_ref[...] = x2
        hn = _rms(x2) * g_ref[...]
        hn_ref[...] = (hn * (1.0 + _mod_rows(sc_ref, rows)) + _mod_rows(sh_ref, rows)).astype(hn_ref.dtype)


def ffn_block(grp, x1, h2, combine, wg, wu, wd, mod, g_next, mod_next, last):
    n, d = x1.shape
    n_exp, _, ff = wg.shape
    tm = grp["tm"]
    tf = 1408 if ff % 1408 == 0 else ff
    row = lambda i, e, f: (i, 0)
    if last:
        zero_mod = jnp.zeros((SUBLANES, d), F32)
        nspec = pl.BlockSpec((SUBLANES, d), lambda i, e, f: (0, 0))
        next_specs, next_args = [nspec, nspec], (zero_mod, zero_mod)
    else:
        next_specs, next_args = [_mod_spec(grp, 1, tm), _mod_spec(grp, 0, tm)], (mod_next, mod_next)
    return pl.pallas_call(
        _ffn_kernel,
        grid=(n // tm, n_exp, ff // tf),
        in_specs=[pl.BlockSpec((tm, d), row), pl.BlockSpec((tm, d), row), pl.BlockSpec((tm, LANES), row),
                  pl.BlockSpec((1, d, tf), lambda i, e, f: (e, 0, f)),
                  pl.BlockSpec((1, d, tf), lambda i, e, f: (e, 0, f)),
                  pl.BlockSpec((1, tf, d), lambda i, e, f: (e, f, 0)),
                  _mod_spec(grp, 5, tm),
                  pl.BlockSpec((1, d), lambda i, e, f: (0, 0))] + next_specs,
        out_specs=[pl.BlockSpec((tm, d), row), pl.BlockSpec((tm, d), row)],
        out_shape=[jax.ShapeDtypeStruct((n, d), F32), jax.ShapeDtypeStruct((n, d), F32 if last else BF16)],
        scratch_shapes=[pltpu.VMEM((tm, d), F32)],
        compiler_params=_cparams(("arbitrary", "arbitrary", "arbitrary")),
        name="ffn_block",
    )(x1, h2, combine, wg, wu, wd, mod, g_next.reshape(1, d), *next_args)


def _moba_kernel(q_ref, k_ref, v_ref, o_ref, kb_ref, vb_ref, km_ref, sel_ref, *, nb):
    qi = pl.program_id(2)
    blk = MOBA_BLOCK
    nbp = sel_ref.shape[1]

    @pl.when(qi == 0)
    def _():
        km = jnp.zeros((HEAD_DIM, LANES), F32)
        lane = lax.broadcasted_iota(I32, (HEAD_DIM, LANES), 1)
        for j in range(nb):
            kt = k_ref[:, j * blk:(j + 1) * blk]
            kb_ref[j] = kt.T.astype(BF16)
            vb_ref[j] = v_ref[:, j * blk:(j + 1) * blk].astype(BF16)
            mean = jnp.sum(kt, axis=1, keepdims=True) * (1.0 / blk)
            km = jnp.where(lane == j, mean, km)
        km_ref[...] = km.astype(BF16)

    sub = lax.broadcasted_iota(I32, (nbp, blk), 0)
    krow = lax.broadcasted_iota(I32, (blk, blk), 0)
    qcol = lax.broadcasted_iota(I32, (blk, blk), 1)
    causal = _f01(krow <= qcol)
    outs = []
    for g in range(MOBA_HEADS // MOBA_KV_HEADS):
        qt = q_ref[g * HEAD_DIM:(g + 1) * HEAD_DIM, :].astype(BF16)
        gate = _dot_tn(km_ref[...], qt)[:nbp]
        cnt = jnp.zeros((nbp, blk), F32)
        for j2 in range(nb):
            row = gate[j2:j2 + 1, :]
            beats = _f01((row > gate) | ((row == gate) & (j2 < sub)))
            cnt = cnt + beats * _f01(j2 < qi)
        sel_ref[g] = _f01((sub < qi) & (cnt < MOBA_TOPK))

        def body(j, carry, qt=qt, g=g):
            m, l, acc = carry
            s = _dot(kb_ref[j], qt) * ATT_SCALE
            own = _f01(j == qi)
            allowed = own * causal + (1.0 - own) * sel_ref[g, pl.ds(j, 1), :]
            s = jnp.where(allowed > 0.5, s, NEG_BIG)
            m_new = jnp.maximum(m, jnp.max(s, axis=0, keepdims=True))
            alpha = jnp.exp(m - m_new)
            p = jnp.exp(s - m_new) * allowed
            l = alpha * l + jnp.sum(p, axis=0, keepdims=True)
            acc = alpha * acc + _dot(vb_ref[j], p.astype(BF16))
            return m_new, l, acc

        init = (jnp.full((1, blk), NEG_BIG, F32), jnp.zeros((1, blk), F32), jnp.zeros((HEAD_DIM, blk), F32))
        _, l, acc = lax.fori_loop(0, qi + 1, body, init)
        outs.append((acc / l).T)
    o_ref[...] = jnp.concatenate(outs, axis=1).astype(o_ref.dtype)


def moba_prompt(grp, proj_t):
    b_sz, seq = grp["B"], grp["L"]
    assert seq % MOBA_BLOCK == 0
    nb = seq // MOBA_BLOCK
    nbp = -(-nb // SUBLANES) * SUBLANES
    g = MOBA_HEADS // MOBA_KV_HEADS
    return pl.pallas_call(
        functools.partial(_moba_kernel, nb=nb),
        grid=(b_sz, MOBA_KV_HEADS, nb),
        in_specs=[pl.BlockSpec((g * HEAD_DIM, MOBA_BLOCK), lambda b, h, i: (T_MQ // (g * HEAD_DIM) + h, b * nb + i)),
                  pl.BlockSpec((HEAD_DIM, seq), lambda b, h, i: (T_MK // HEAD_DIM + h, b)),
                  pl.BlockSpec((HEAD_DIM, seq), lambda b, h, i: (T_MV // HEAD_DIM + h, b))],
        out_specs=pl.BlockSpec((MOBA_BLOCK, g * HEAD_DIM), lambda b, h, i: (b * nb + i, h)),
        out_shape=jax.ShapeDtypeStruct((b_sz * seq, MOBA_HEADS * HEAD_DIM), BF16),
        scratch_shapes=[pltpu.VMEM((nb, MOBA_BLOCK, HEAD_DIM), BF16), pltpu.VMEM((nb, HEAD_DIM, MOBA_BLOCK), BF16),
                        pltpu.VMEM((HEAD_DIM, LANES), BF16), pltpu.VMEM((g, nbp, MOBA_BLOCK), F32)],
        compiler_params=_cparams(("arbitrary", "arbitrary", "arbitrary")),
        name="moba_prompt",
    )(proj_t, proj_t, proj_t)


def _compress_body(get_tile, n_tiles, pe_ref, w1_ref, b1_ref, w2t_ref, b2r_ref, b2c_ref,
                   orow_ref, ot_ref, r_ref, x_ref):
    ncp = x_ref.shape[0]
    rows = n_tiles * LANES
    for sh in range(2 * NSA_KV_HEADS):
        st = sh // NSA_KV_HEADS
        for t in range(n_tiles):
            r_ref[t * LANES:(t + 1) * LANES, :] = get_tile(sh, t).T
        r_ref[rows:rows + NSA_CMP_STRIDE, :] = jnp.zeros((NSA_CMP_STRIDE, HEAD_DIM), F32)
        for lp in range(NSA_CMP_LEN // 2):
            pieces = []
            for l in (2 * lp, 2 * lp + 1):
                pieces.append(r_ref[pl.ds(l, ncp, stride=NSA_CMP_STRIDE), :] + pe_ref[st, l:l + 1, :])
            x_ref[:, lp * LANES:(lp + 1) * LANES] = jnp.concatenate(pieces, axis=1).astype(BF16)
        pre = _dot(x_ref[...], w1_ref[st]) + b1_ref[st]
        hid = _silu(pre).astype(BF16)
        orow_ref[0, sh] = (_dot_nt(hid, w2t_ref[st]) + b2r_ref[st]).astype(orow_ref.dtype)
        ot_ref[0, sh] = (_dot_nt(w2t_ref[st], hid) + b2c_ref[st]).astype(ot_ref.dtype)


def _compress_prompt_kernel(src_ref, *rest, n_tiles):
    get = lambda sh, t: src_ref[sh * HEAD_DIM:(sh + 1) * HEAD_DIM, t * LANES:(t + 1) * LANES]
    _compress_body(get, n_tiles, *rest)


def _compress_paged_kernel(pt_ref, *rest, n_tiles):
    pages, rest = rest[:n_tiles], rest[n_tiles:]
    get = lambda sh, t: pages[t][0, 0, sh // NSA_KV_HEADS, sh % NSA_KV_HEADS]
    _compress_body(get, n_tiles, *rest)


def _compress_call(kernel, b_sz, n_tiles, src_specs, src_args, cmp_w, prefetch=()):
    pe, w1, b1, w2t, b2 = cmp_w
    ncp = n_tiles * LANES // NSA_CMP_STRIDE
    full = lambda shape: pl.BlockSpec(shape, lambda b, *_: (0,) * len(shape))
    in_specs = src_specs + [full(pe.shape), full(w1.shape), full((2, 1, NSA_CMP_HIDDEN)), full(w2t.shape),
                            full((2, 1, HEAD_DIM)), full((2, HEAD_DIM, 1))]
    nsh = 2 * NSA_KV_HEADS
    out_specs = [pl.BlockSpec((1, nsh, ncp, HEAD_DIM), lambda b, *_: (b, 0, 0, 0)),
                 pl.BlockSpec((1, nsh, HEAD_DIM, ncp), lambda b, *_: (b, 0, 0, 0))]
    out_shape = [jax.ShapeDtypeStruct((b_sz, nsh, ncp, HEAD_DIM), BF16),
                 jax.ShapeDtypeStruct((b_sz, nsh, HEAD_DIM, ncp), BF16)]
    scratch = [pltpu.VMEM((n_tiles * LANES + NSA_CMP_STRIDE, HEAD_DIM), F32),
               pltpu.VMEM((ncp, NSA_CMP_LEN * HEAD_DIM), BF16)]
    args = src_args + [pe, w1, b1.reshape(2, 1, -1), w2t, b2.reshape(2, 1, -1), b2.reshape(2, -1, 1)]
    return pl.pallas_call(
        functools.partial(kernel, n_tiles=n_tiles),
        grid_spec=pltpu.PrefetchScalarGridSpec(num_scalar_prefetch=len(prefetch), grid=(b_sz,), in_specs=in_specs,
                                               out_specs=out_specs, scratch_shapes=scratch),
        out_shape=out_shape,
        compiler_params=_cparams(("arbitrary",)),
        name="nsa_compress",
    )(*prefetch, *args)


def compress_prompt(grp, proj_t, cmp_w):
    seq = grp["L"]
    rows = 2 * NSA_KV_HEADS * HEAD_DIM
    spec = pl.BlockSpec((rows, seq), lambda b: (T_NKV // rows, b))
    return _compress_call(_compress_prompt_kernel, grp["B"], seq // LANES, [spec], [proj_t], cmp_w)


def compress_paged(grp, cache, layer, page_table, cmp_w):
    n_pages = page_table.shape[1]
    specs = _page_specs((1, 1, 2, NSA_KV_HEADS, HEAD_DIM, LANES), n_pages, layer, (0, 0, 0, 0))
    return _compress_call(_compress_paged_kernel, grp["B"], n_pages, specs, [cache] * n_pages, cmp_w,
                          prefetch=(page_table,))


def _nsa_kernel(q_ref, ks_ref, vs_ref, kw_ref, vw_ref, gt_ref, kc_ref, vct_ref, ov_ref, o_ref,
                ksr_ref, vsb_ref, kwr_ref, vwb_ref, sel_ref, *, seq, ns, nc):
    kvh, qc = pl.program_id(1), pl.program_id(2)
    n_tiles = seq // LANES
    g_sz = NSA_HEADS // NSA_KV_HEADS
    width = g_sz * LANES
    ncp = kc_ref.shape[2]
    nsp = sel_ref.shape[0]

    @pl.when(qc == 0)
    def _():
        for t in range(n_tiles):
            sl = slice(t * LANES, (t + 1) * LANES)
            ksr_ref[t] = ks_ref[:, sl].T.astype(BF16)
            vsb_ref[t] = vs_ref[:, sl].astype(BF16)
            kwr_ref[t] = kw_ref[:, sl].T.astype(BF16)
            vwb_ref[t] = vw_ref[:, sl].astype(BF16)

    q4 = q_ref[...]
    qt = jnp.concatenate([q4[g * HEAD_DIM:(g + 1) * HEAD_DIM, :] for g in range(g_sz)], axis=1).astype(BF16)
    lane_w = lax.broadcasted_iota(I32, (1, width), 1)
    qpos_w = qc * LANES + (lane_w & (LANES - 1))
    qpos = qpos_w[:, :LANES]

    s = _dot(kc_ref[0, 0], qt) * ATT_SCALE
    cidx = lax.broadcasted_iota(I32, (ncp, width), 0)
    valid = _f01((cidx * NSA_CMP_STRIDE + (NSA_CMP_LEN - 1) <= qpos_w) & (cidx < nc))
    s = jnp.where(valid > 0.5, s, NEG_BIG)
    p = jnp.exp(s - jnp.max(s, axis=0, keepdims=True)) * valid
    l = jnp.sum(p, axis=0, keepdims=True)
    pb = (p / jnp.where(l > 0.0, l, 1.0)).astype(BF16)
    o_cmp = _dot(vct_ref[0, 0], pb)
    ps = _dot(ov_ref[...], pb)
    p_slc = ps[:, 0:LANES]
    for g in range(1, g_sz):
        p_slc = p_slc + ps[:, g * LANES:(g + 1) * LANES]

    j = lax.broadcasted_iota(I32, (nsp, LANES), 0)
    cur = qpos // NSA_SLC_BLOCK
    forced = (j == 0) | (j == cur) | (j == cur - 1)
    elig = j <= cur
    score = jnp.where(forced, jnp.inf, p_slc)
    score = jnp.where(elig, score, -jnp.inf)
    cnt = jnp.zeros((nsp, LANES), F32)
    for j2 in range(ns):
        row = score[j2:j2 + 1, :]
        cnt = cnt + _f01((row > score) | ((row == score) & (j2 < j)))
    sel_ref[...] = _f01(elig & (cnt < NSA_TOPN))

    sub = lax.broadcasted_iota(I32, (LANES, LANES), 0)

    def attend(k_ref, v_ref, allowed_fn, lo, hi):
        def body(t, carry):
            m, l, acc = carry
            s = _dot(k_ref[t], qt) * ATT_SCALE
            allowed = allowed_fn(t)
            allowed = jnp.concatenate([allowed] * g_sz, axis=1)
            s = jnp.where(allowed > 0.5, s, NEG_BIG)
            m_new = jnp.maximum(m, jnp.max(s, axis=0, keepdims=True))
            alpha = jnp.exp(m - m_new)
            p = jnp.exp(s - m_new) * allowed
            l = alpha * l + jnp.sum(p, axis=0, keepdims=True)
            acc = alpha * acc + _dot(v_ref[t], p.astype(BF16))
            return m_new, l, acc

        init = (jnp.full((1, width), NEG_BIG, F32), jnp.zeros((1, width), F32), jnp.zeros((HEAD_DIM, width), F32))
        _, l, acc = lax.fori_loop(lo, hi, body, init)
        return acc / jnp.where(l > 0.0, l, 1.0)

    def slc_allowed(t):
        blocks_per_tile = LANES // NSA_SLC_BLOCK
        r0 = sel_ref[pl.ds(blocks_per_tile * t, 1), :]
        r1 = sel_ref[pl.ds(blocks_per_tile * t + 1, 1), :]
        chosen = jnp.where(sub < NSA_SLC_BLOCK, r0, r1)
        return chosen * _f01(t * LANES + sub <= qpos)

    def win_allowed(t):
        dist = qpos - (t * LANES + sub)
        return _f01((dist >= 0) & (dist <= NSA_WINDOW))

    o_slc = attend(ksr_ref, vsb_ref, slc_allowed, 0, qc + 1)
    o_win = attend(kwr_ref, vwb_ref, win_allowed, jnp.maximum(qc - NSA_WINDOW // LANES, 0), qc + 1)

    outs = []
    for g in range(g_sz):
        head = kvh * g_sz + g
        gates = [_sigmoid(gt_ref[pl.ds(br * NSA_HEADS + head, 1), :]) for br in range(3)]
        sl = slice(g * LANES, (g + 1) * LANES)
        o = gates[0] * o_cmp[:, sl] + gates[1] * o_slc[:, sl] + gates[2] * o_win[:, sl]
        outs.append(o.T)
    o_ref[...] = jnp.concatenate(outs, axis=1).astype(o_ref.dtype)


def _overlap_matrix(ncp, nsp):
    c0 = np.arange(ncp)[:, None] * NSA_CMP_STRIDE
    s0 = np.arange(nsp)[None, :] * NSA_SLC_BLOCK
    return ((c0 < s0 + NSA_SLC_BLOCK) & (c0 + NSA_CMP_LEN > s0)).astype(np.float32)


def nsa_prompt(grp, proj_t, kc_rows, vc_t):
    b_sz, seq = grp["B"], grp["L"]
    assert seq % LANES == 0 and seq >= NSA_CMP_LEN
    nqc = seq // LANES
    g_sz = NSA_HEADS // NSA_KV_HEADS
    ns = seq // NSA_SLC_BLOCK
    nsp = -(-ns // SUBLANES) * SUBLANES
    nc = (seq - NSA_CMP_LEN) // NSA_CMP_STRIDE + 1
    ncp = kc_rows.shape[2]
    overlap_t = jnp.asarray(_overlap_matrix(ncp, nsp).T, BF16)
    kv_spec = lambda st: pl.BlockSpec((HEAD_DIM, seq), lambda b, h, i: (T_NKV // HEAD_DIM + st * NSA_KV_HEADS + h, b))
    tile_rows = pltpu.VMEM((nqc, LANES, HEAD_DIM), BF16)
    tile_cols = pltpu.VMEM((nqc, HEAD_DIM, LANES), BF16)
    return pl.pallas_call(
        functools.partial(_nsa_kernel, seq=seq, ns=ns, nc=nc),
        grid=(b_sz, NSA_KV_HEADS, nqc),
        in_specs=[pl.BlockSpec((g_sz * HEAD_DIM, LANES), lambda b, h, i: (T_NQ // (g_sz * HEAD_DIM) + h, b * nqc + i)),
                  kv_spec(2), kv_spec(3), kv_spec(4), kv_spec(5),
                  pl.BlockSpec((HEAD_DIM, LANES), lambda b, h, i: (T_NGATE // HEAD_DIM, b * nqc + i)),
                  pl.BlockSpec((1, 1, ncp, HEAD_DIM), lambda b, h, i: (b, h, 0, 0)),
                  pl.BlockSpec((1, 1, HEAD_DIM, ncp), lambda b, h, i: (b, NSA_KV_HEADS + h, 0, 0)),
                  pl.BlockSpec((nsp, ncp), lambda b, h, i: (0, 0))],
        out_specs=pl.BlockSpec((LANES, g_sz * HEAD_DIM), lambda b, h, i: (b * nqc + i, h)),
        out_shape=jax.ShapeDtypeStruct((b_sz * seq, NSA_HEADS * HEAD_DIM), BF16),
        scratch_shapes=[tile_rows, tile_cols, tile_rows, tile_cols, pltpu.VMEM((nsp, LANES), F32)],
        compiler_params=_cparams(("arbitrary", "arbitrary", "arbitrary")),
        name="nsa_prompt",
    )(proj_t, proj_t, proj_t, proj_t, proj_t, proj_t, kc_rows, vc_t, overlap_t)


def _ssm_kernel(xbc_ref, z_ref, dtc_ref, dtr_ref, cs_ref, h0_ref, cw_ref, cb_ref, dbr_ref, alr_ref, dbc_ref, alc_ref,
                gs_ref, dsk_ref, y_ref, hf_ref, tail_ref, h_ref, ybuf_ref, *, n_valid):
    c = pl.program_id(1)
    q_len = xbc_ref.shape[0]

    @pl.when(c == 0)
    def _():
        tail_ref[...] = cs_ref[0]
        h_ref[...] = h0_ref[0]

    x = xbc_ref[...]
    tail = tail_ref[...]
    row8 = lax.broadcasted_iota(I32, tail.shape, 0)
    conv = cb_ref[...] + x * cw_ref[SSM_CONV - 1:SSM_CONV, :]
    for k in range(1, SSM_CONV):
        xs = pltpu.roll(x, k, 0)
        first = jnp.where(row8 < k, pltpu.roll(tail, k, 0), xs[:SUBLANES])
        xs = first if q_len == SUBLANES else jnp.concatenate([first, xs[SUBLANES:]], axis=0)
        conv = conv + xs * cw_ref[SSM_CONV - 1 - k:SSM_CONV - k, :]
    tail_ref[...] = x[q_len - SUBLANES:, :]
    act = _silu(conv)
    gn = SSM_GROUPS * SSM_STATE
    b_all = act[:, SSM_INNER:SSM_INNER + gn].astype(BF16)
    c_all = act[:, SSM_INNER + gn:].astype(BF16)

    rows_q = lax.broadcasted_iota(I32, (q_len, LANES), 0)
    dt_c = _softplus(dtc_ref[...] + dbr_ref[...]) * _f01(rows_q < n_valid)
    a_c = dt_c * (-jnp.exp(alr_ref[...]))
    cols_q = lax.broadcasted_iota(I32, (SSM_HEADS, q_len), 1)
    dt_r = _softplus(dtr_ref[0] + dbc_ref[...]) * _f01(cols_q < n_valid)
    a_r = dt_r * (-jnp.exp(alc_ref[...]))
    ti = lax.broadcasted_iota(I32, (q_len, q_len), 0)
    si = lax.broadcasted_iota(I32, (q_len, q_len), 1)
    causal = ti >= si
    acs_c = _dot_exact(_f01(causal), a_c)
    acs_r = _dot_exact(a_r, _f01(si >= ti))

    cb = []
    for gi in range(SSM_GROUPS):
        sl = slice(gi * SSM_STATE, (gi + 1) * SSM_STATE)
        cb.append(_dot_nt(c_all[:, sl], b_all[:, sl]))
    hpg = SSM_HEADS // SSM_GROUPS
    for h in range(SSM_HEADS):
        gi = h // hpg
        sl = slice(gi * SSM_STATE, (gi + 1) * SSM_STATE)
        col = acs_c[:, h:h + 1]
        decay = jnp.where(causal, jnp.exp(col - acs_r[h:h + 1, :]), 0.0)
        xh = act[:, h * SSM_HEAD_DIM:(h + 1) * SSM_HEAD_DIM]
        xdt = xh * dt_c[:, h:h + 1]
        y = _dot((cb[gi] * decay).astype(BF16), xdt.astype(BF16))
        h_prev = h_ref[h]
        y = y + _dot_nt(c_all[:, sl], h_prev.astype(BF16)) * jnp.exp(col)
        last = acs_c[q_len - 1:q_len, h:h + 1]
        upd = _dot_tn((xdt * jnp.exp(last - col)).astype(BF16), b_all[:, sl])
        h_ref[h] = jnp.exp(last) * h_prev + upd
        ybuf_ref[:, h * SSM_HEAD_DIM:(h + 1) * SSM_HEAD_DIM] = y + xh * dsk_ref[h]

    yz = ybuf_ref[...] * _silu(z_ref[...])
    y_ref[...] = (_rms(yz) * gs_ref[...]).astype(y_ref.dtype)

    @pl.when(c == pl.num_programs(1) - 1)
    def _():
        hf_ref[0] = h_ref[...]


def ssm_mixer(b_sz, q_len, n_chunks, n_valid, xbc, xbc_col, z, z_col, dt_cols, dt_rows, conv_state8, h0, ssm_w,
              out_dtype):
    conv_w, conv_b, dt_bias, a_log, d_skip, g_ssm = ssm_w
    tok = lambda col: (lambda b, c: (b * n_chunks + c, col))
    fixed = lambda shape: pl.BlockSpec(shape, lambda b, c: (0,) * len(shape))
    pad_lanes = lambda v: jnp.pad(v.reshape(1, -1), ((0, 0), (0, LANES - v.shape[0])))
    state_shape = (SSM_HEADS, SSM_HEAD_DIM, SSM_STATE)
    return pl.pallas_call(
        functools.partial(_ssm_kernel, n_valid=n_valid),
        grid=(b_sz, n_chunks),
        in_specs=[pl.BlockSpec((q_len, SSM_CONV_DIM), tok(xbc_col)),
                  pl.BlockSpec((q_len, SSM_INNER), tok(z_col)),
                  pl.BlockSpec((q_len, LANES), tok(0)),
                  pl.BlockSpec((1, SSM_HEADS, q_len), lambda b, c: (b * n_chunks + c, 0, 0)),
                  pl.BlockSpec((1, SUBLANES, SSM_CONV_DIM), lambda b, c: (b, 0, 0)),
                  pl.BlockSpec((1,) + state_shape, lambda b, c: (b, 0, 0, 0)),
                  fixed((SSM_CONV, SSM_CONV_DIM)), fixed((1, SSM_CONV_DIM)),
                  fixed((1, LANES)), fixed((1, LANES)), fixed((SSM_HEADS, 1)), fixed((SSM_HEADS, 1)),
                  fixed((1, SSM_INNER)),
                  pl.BlockSpec(memory_space=pltpu.SMEM)],
        out_specs=[pl.BlockSpec((q_len, SSM_INNER), tok(0)),
                   pl.BlockSpec((1,) + state_shape, lambda b, c: (b, 0, 0, 0))],
        out_shape=[jax.ShapeDtypeStruct((b_sz * n_chunks * q_len, SSM_INNER), out_dtype),
                   jax.ShapeDtypeStruct((b_sz,) + state_shape, F32)],
        scratch_shapes=[pltpu.VMEM((SUBLANES, SSM_CONV_DIM), F32), pltpu.VMEM(state_shape, F32),
                        pltpu.VMEM((q_len, SSM_INNER), F32)],
        compiler_params=_cparams(("arbitrary", "arbitrary")),
        name="ssm_mixer",
    )(xbc, z, dt_cols, dt_rows, conv_state8, h0, conv_w, conv_b.reshape(1, -1), pad_lanes(dt_bias), pad_lanes(a_log),
      dt_bias.reshape(-1, 1), a_log.reshape(-1, 1), g_ssm.reshape(1, -1), d_skip)


def _moba_dec_kernel(pt_ref, q_ref, kn_ref, vn_ref, *rest, n_pages, n_new):
    pages, o_ref = rest[:n_pages], rest[n_pages]
    rows = q_ref.shape[2]
    ppb = MOBA_BLOCK // LANES
    nb_past = n_pages // ppb
    lane = lax.broadcasted_iota(I32, (rows, LANES), 1)
    lane_k = lax.broadcasted_iota(I32, (HEAD_DIM, LANES), 1)
    t_row = lax.broadcasted_iota(I32, (rows, SUBLANES), 0) % n_new
    t_col = lax.broadcasted_iota(I32, (rows, SUBLANES), 1)
    new_ok = _f01((t_col <= t_row) & (t_col < n_new))
    for kvh in range(MOBA_KV_HEADS):
        q = q_ref[0, kvh].astype(BF16)
        scores = []
        km = jnp.zeros((HEAD_DIM, LANES), F32)
        for j in range(nb_past):
            ksum = jnp.zeros((HEAD_DIM, LANES), F32)
            for pp in range(ppb):
                kt = pages[j * ppb + pp][0, 0, 0, kvh]
                scores.append(_dot(q, kt.astype(BF16)) * ATT_SCALE)
                ksum = ksum + kt
            mean = jnp.sum(ksum, axis=1, keepdims=True) * (1.0 / MOBA_BLOCK)
            km = jnp.where(lane_k == j, mean, km)
        gate = jnp.where(lane < nb_past, _dot(q, km.astype(BF16)), -jnp.inf)
        cnt = jnp.zeros((rows, LANES), F32)
        for j2 in range(nb_past):
            col = gate[:, j2:j2 + 1]
            cnt = cnt + _f01((col > gate) | ((col == gate) & (j2 < lane)))
        sel = _f01((lane < nb_past) & (cnt < MOBA_TOPK))
        s_new = jnp.where(new_ok > 0.5, _dot_nt(q, kn_ref[0, kvh].astype(BF16)) * ATT_SCALE, NEG_BIG)
        m = jnp.max(s_new, axis=1, keepdims=True)
        chosen = []
        for p in range(n_pages):
            a = sel[:, p // ppb:p // ppb + 1]
            scores[p] = jnp.where(a > 0.5, scores[p], NEG_BIG)
            chosen.append(a)
            m = jnp.maximum(m, jnp.max(scores[p], axis=1, keepdims=True))
        p_new = jnp.exp(s_new - m) * new_ok
        l = jnp.sum(p_new, axis=1, keepdims=True)
        o = _dot(p_new.astype(BF16), vn_ref[0, kvh].astype(BF16))
        for p in range(n_pages):
            pr = jnp.exp(scores[p] - m) * chosen[p]
            l = l + jnp.sum(pr, axis=1, keepdims=True)
            o = o + _dot_nt(pr.astype(BF16), pages[p][0, 0, 1, kvh].astype(BF16))
        o_ref[0, kvh] = o / l


def _page_specs(block, n_pages, layer, block_idx):
    return [pl.BlockSpec(block, functools.partial(lambda b, pt, j: (layer, pt[b, j]) + block_idx, j=j))
            for j in range(n_pages)]


def moba_decode(grp, q_dec, k_new, v_new, cache, layer, page_table):
    b_sz = grp["B"]
    n_pages = page_table.shape[1]
    per_seq = lambda shape: pl.BlockSpec((1,) + shape, lambda b, pt: (b,) + (0,) * len(shape))
    q_shape, n_shape = q_dec.shape[1:], k_new.shape[1:]
    specs = [per_seq(q_shape), per_seq(n_shape), per_seq(n_shape)]
    specs += _page_specs((1, 1, 2, MOBA_KV_HEADS, HEAD_DIM, LANES), n_pages, layer, (0, 0, 0, 0))
    return pl.pallas_call(
        functools.partial(_moba_dec_kernel, n_pages=n_pages, n_new=grp["L"]),
        grid_spec=pltpu.PrefetchScalarGridSpec(num_scalar_prefetch=1, grid=(b_sz,), in_specs=specs,
                                               out_specs=per_seq(q_shape)),
        out_shape=jax.ShapeDtypeStruct(q_dec.shape, F32),
        compiler_params=_cparams(("arbitrary",)),
        name="moba_decode",
    )(page_table, q_dec, k_new, v_new, *([cache] * n_pages))


def _nsa_dec_kernel(pt_ref, q_ref, kn_ref, vn_ref, kwn_ref, vwn_ref, gt_ref, kc_ref, win_ref, ov_ref, ex_ref, *rest,
                    n_pages, n_new, past, ns, nc):
    pages, o_ref = rest[:n_pages], rest[n_pages]
    rows = q_ref.shape[2]
    ncp = kc_ref.shape[2]
    wlen = win_ref.shape[-1]
    t_rowl = lax.broadcasted_iota(I32, (rows, LANES), 0) % n_new
    lane = lax.broadcasted_iota(I32, (rows, LANES), 1)
    t_row8 = lax.broadcasted_iota(I32, (rows, SUBLANES), 0) % n_new
    t_col8 = lax.broadcasted_iota(I32, (rows, SUBLANES), 1)
    new_ok = _f01((t_col8 <= t_row8) & (t_col8 < n_new))

    def softmax_parts(parts):
        m = None
        masked = []
        for s, a in parts:
            s = jnp.where(a > 0.5, s, NEG_BIG)
            masked.append(s)
            mx = jnp.max(s, axis=1, keepdims=True)
            m = mx if m is None else jnp.maximum(m, mx)
        ps = [jnp.exp(s - m) * a for s, (_, a) in zip(masked, parts)]
        l = ps[0].sum(axis=1, keepdims=True)
        for p in ps[1:]:
            l = l + p.sum(axis=1, keepdims=True)
        inv = 1.0 / jnp.where(l > 0.0, l, 1.0)
        return [p * inv for p in ps]

    for kvh in range(NSA_KV_HEADS):
        q = q_ref[0, kvh].astype(BF16)
        qpos = past + t_rowl
        cidx = lax.broadcasted_iota(I32, (rows, ncp), 1)
        qpos_c = past + lax.broadcasted_iota(I32, (rows, ncp), 0) % n_new
        valid = _f01((cidx * NSA_CMP_STRIDE + (NSA_CMP_LEN - 1) <= qpos_c) & (cidx < nc))
        s = _dot_nt(q, kc_ref[0, kvh]) * ATT_SCALE
        (p_cmp,) = softmax_parts([(s, valid)])
        pb = p_cmp.astype(BF16)
        o_cmp = _dot(pb, kc_ref[0, NSA_KV_HEADS + kvh])
        ps = _dot(pb, ov_ref[...])
        p_slc = ps
        for g in range(1, rows // n_new):
            p_slc = p_slc + pltpu.roll(ps, g * n_new, 0)
        cur = qpos // NSA_SLC_BLOCK
        forced = (lane == 0) | (lane == cur) | (lane == cur - 1)
        elig = lane <= cur
        score = jnp.where(forced, jnp.inf, p_slc)
        score = jnp.where(elig, score, -jnp.inf)
        cnt = jnp.zeros((rows, LANES), F32)
        for j2 in range(ns):
            col = score[:, j2:j2 + 1]
            cnt = cnt + _f01((col > score) | ((col == score) & (j2 < lane)))
        sel = _f01(elig & (cnt < NSA_TOPN))
        key_ok = _dot(sel.astype(BF16), ex_ref[...])
        parts = []
        for p in range(n_pages):
            sp = _dot(q, pages[p][0, 0, 0, kvh].astype(BF16)) * ATT_SCALE
            parts.append((sp, key_ok[:, p * LANES:(p + 1) * LANES]))
        parts.append((_dot_nt(q, kn_ref[0, kvh].astype(BF16)) * ATT_SCALE, new_ok))
        probs = softmax_parts(parts)
        o_slc = _dot(probs[-1].astype(BF16), vn_ref[0, kvh].astype(BF16))
        for p in range(n_pages):
            o_slc = o_slc + _dot_nt(probs[p].astype(BF16), pages[p][0, 0, 1, kvh].astype(BF16))
        jw = lax.broadcasted_iota(I32, (rows, wlen), 1)
        tw = lax.broadcasted_iota(I32, (rows, wlen), 0) % n_new
        dist = wlen + tw - jw
        win_okay = _f01((dist >= 0) & (dist <= NSA_WINDOW) & (past - wlen + jw >= 0))
        parts = [(_dot(q, win_ref[0, 0, 0, kvh].astype(BF16)) * ATT_SCALE, win_okay),
                 (_dot_nt(q, kwn_ref[0, kvh].astype(BF16)) * ATT_SCALE, new_ok)]
        probs = softmax_parts(parts)
        o_win = (_dot_nt(probs[0].astype(BF16), win_ref[0, 0, 1, kvh].astype(BF16))
                 + _dot(probs[1].astype(BF16), vwn_ref[0, kvh].astype(BF16)))
        gates = _sigmoid(gt_ref[0, kvh])
        o_ref[0, kvh] = gates[:, 0:1] * o_cmp + gates[:, 1:2] * o_slc + gates[:, 2:3] * o_win


def nsa_decode(grp, q_dec, new_rows, gates_dec, kc_rows, win_state, cache, layer, page_table, past):
    b_sz, n_new = grp["B"], grp["L"]
    n_pages = page_table.shape[1]
    total = past + n_new
    ns = -(-total // NSA_SLC_BLOCK)
    nc = (total - NSA_CMP_LEN) // NSA_CMP_STRIDE + 1
    ncp = kc_rows.shape[2]
    assert ns <= LANES and past % NSA_SLC_BLOCK == 0 and n_new <= NSA_SLC_BLOCK
    assert (nc - 1) * NSA_CMP_STRIDE + NSA_CMP_LEN <= past, "compressed blocks must not reach the new rows"
    overlap = jnp.asarray(_overlap_matrix(ncp, LANES), BF16)
    expand = jnp.asarray((np.arange(past)[None, :] // NSA_SLC_BLOCK == np.arange(LANES)[:, None]), BF16)
    per_seq = lambda shape: pl.BlockSpec((1,) + shape, lambda b, pt: (b,) + (0,) * len(shape))
    fixed = lambda shape: pl.BlockSpec(shape, lambda b, pt: (0,) * len(shape))
    k_new, v_new, kw_new, vw_new = new_rows
    specs = [per_seq(q_dec.shape[1:])] + [per_seq(k_new.shape[1:])] * 4
    specs += [per_seq(gates_dec.shape[1:]), per_seq(kc_rows.shape[1:]),
              pl.BlockSpec((1, 1) + win_state.shape[2:], lambda b, pt: (layer, b, 0, 0, 0, 0)),
              fixed(overlap.shape), fixed(expand.shape)]
    specs += _page_specs((1, 1, 2, NSA_KV_HEADS, HEAD_DIM, LANES), n_pages, layer, (1, 0, 0, 0))
    return pl.pallas_call(
        functools.partial(_nsa_dec_kernel, n_pages=n_pages, n_new=n_new, past=past, ns=ns, nc=nc),
        grid_spec=pltpu.PrefetchScalarGridSpec(num_scalar_prefetch=1, grid=(b_sz,), in_specs=specs,
                                               out_specs=per_seq(q_dec.shape[1:])),
        out_shape=jax.ShapeDtypeStruct(q_dec.shape, F32),
        compiler_params=_cparams(("arbitrary",)),
        name="nsa_decode",
    )(page_table, q_dec, k_new, v_new, kw_new, vw_new, gates_dec, kc_rows, win_state, overlap, expand,
      *([cache] * n_pages))


def _tile(n, prefs):
    for t in prefs:
        if n % t == 0:
            return t
    return n


def _prep_layer_weights(w_in, w_branch, w_out, cmp_w1, cmp_w2):
    wt = jnp.swapaxes(w_in, 1, 2)
    depth = wt.shape[0]
    zeros = lambda r: jnp.zeros((depth, r, wt.shape[2]), wt.dtype)
    w_att = jnp.concatenate([wt[:, 0:1024], wt[:, 2568:3872], wt[:, 2560:2568], zeros(T_ROWS - 2336)], axis=1)
    w_row = jnp.concatenate([wt[:, 3872:6944], wt[:, 1536:2560], wt[:, 1024:1536]], axis=1)
    w_dt = jnp.concatenate([wt[:, 2560:2568], zeros(LANES - SSM_HEADS)], axis=1)
    return (w_att.astype(BF16), w_row.astype(BF16), w_dt.astype(BF16), w_branch.astype(BF16), w_out.astype(BF16),
            cmp_w1.astype(BF16), jnp.swapaxes(cmp_w2, 2, 3).astype(BF16))


def _heads_first(a, b_sz, n_new, dims, pad_to=None):
    a = a.reshape((b_sz, n_new) + dims + (HEAD_DIM,))
    nd = len(dims)
    a = jnp.transpose(a, (0,) + tuple(range(2, 2 + nd)) + (1, 2 + nd))
    if pad_to is not None and pad_to > n_new:
        a = jnp.pad(a, [(0, 0)] * (1 + nd) + [(0, pad_to - n_new), (0, 0)])
    return a


def _layer_prompt(grp, x, h, mod, mod_next, lw, ffn_w, g_next, last):
    (w_att, w_row, w_dt, w_branch, w_out, cmp_w, ssm_w, g_ffn) = lw
    b_sz, seq, n = grp["B"], grp["L"], grp["N"]
    proj_t = nt_matmul(w_att, h, _tile(T_ROWS, (640, 512)), _tile(n, (1024, 512)))
    row_proj = nt_matmul(h, w_row, grp["tm"], _tile(R_COLS, (1536, 512)))
    dt_cols = nt_matmul(h, w_dt, grp["tm"], LANES)

    moba_out = moba_prompt(grp, proj_t)

    q_len = SSM_CHUNK
    n_chunks = seq // q_len
    dt_rows = proj_t[T_DT:T_DT + SSM_HEADS].reshape(SSM_HEADS, b_sz * n_chunks, q_len).transpose(1, 0, 2)
    ssm_out, new_ssm = ssm_mixer(
        b_sz, q_len, n_chunks, q_len, row_proj, R_XBC // SSM_CONV_DIM, row_proj, R_Z // SSM_INNER, dt_cols, dt_rows,
        jnp.zeros((b_sz, SUBLANES, SSM_CONV_DIM), F32), jnp.zeros((b_sz, SSM_HEADS, SSM_HEAD_DIM, SSM_STATE), F32),
        ssm_w, BF16)

    kc_rows, vc_t = compress_prompt(grp, proj_t, cmp_w)
    nsa_out = nsa_prompt(grp, proj_t, kc_rows, vc_t)

    x1, h2 = merge_branches(grp, x, (moba_out, ssm_out, nsa_out), row_proj, w_branch, w_out, g_ffn, mod)
    x2, hn = _ffn(grp, x1, h2, ffn_w, mod, g_next, mod_next, last)

    def rows_of(lo, hi, dims):
        return proj_t[lo:hi].reshape(dims + (HEAD_DIM, b_sz, seq)).transpose(3, 4, 0, 1, 2)

    new_moba = rows_of(T_MK, T_NQ, (2, MOBA_KV_HEADS))
    new_nsa = rows_of(T_NKV, T_NKV + 4 * NSA_KV_HEADS * HEAD_DIM, (4, NSA_KV_HEADS))
    win_len = grp["win_len"]
    assert seq >= win_len
    new_win = rows_of(T_NKV + 4 * NSA_KV_HEADS * HEAD_DIM, T_NGATE, (2, NSA_KV_HEADS))[:, seq - win_len:]
    new_conv = row_proj[:, R_XBC:R_XBC + SSM_CONV_DIM].reshape(b_sz, seq, SSM_CONV_DIM)[:, seq - (SSM_CONV - 1):]
    return x2, hn, (new_moba, new_nsa, new_win, new_ssm, new_conv)


def _ffn(grp, x1, h2, ffn_w, mod, g_next, mod_next, last):
    if len(ffn_w) == 4:
        w_router_t, wg, wu, wd = ffn_w
        combine = moe_router(grp, h2, w_router_t)
    else:
        wg, wu, wd = ffn_w
        combine = jnp.ones((x1.shape[0], LANES), F32)
    return ffn_block(grp, x1, h2, combine, wg, wu, wd, mod, g_next, mod_next, last)


def _layer_sample(grp, x, h, mod, mod_next, lw, ffn_w, g_next, last, caches):
    (w_att, w_row, w_dt, w_branch, w_out, cmp_w, ssm_w, g_ffn) = lw
    moba_cache, nsa_cache, win_state, win_prev, conv_state, ssm_state, layer, page_table, past = caches
    b_sz, n_new, n = grp["B"], grp["L"], grp["N"]
    att = nt_matmul(h, w_att, grp["tm"], _tile(T_ROWS, (640, 512)))
    row_proj = nt_matmul(h, w_row, grp["tm"], _tile(R_COLS, (1536, 512)))
    dt_cols = nt_matmul(h, w_dt, grp["tm"], LANES)

    g_m = MOBA_HEADS // MOBA_KV_HEADS
    q_dec = _heads_first(att[:, T_MQ:T_MK], b_sz, n_new, (MOBA_KV_HEADS, g_m)).reshape(b_sz, MOBA_KV_HEADS, g_m * n_new, HEAD_DIM)
    k_new = _heads_first(att[:, T_MK:T_MV], b_sz, n_new, (MOBA_KV_HEADS,), SUBLANES)
    v_new = _heads_first(att[:, T_MV:T_NQ], b_sz, n_new, (MOBA_KV_HEADS,), SUBLANES)
    o = moba_decode(grp, q_dec, k_new, v_new, moba_cache, layer, page_table)
    moba_out = o.reshape(b_sz, MOBA_KV_HEADS, g_m, n_new, HEAD_DIM).transpose(0, 3, 1, 2, 4).reshape(n, -1)

    q_len = SUBLANES
    pad_t = lambda a: jnp.pad(a.reshape(b_sz, n_new, -1), ((0, 0), (0, q_len - n_new), (0, 0))).reshape(b_sz * q_len, -1)
    xbc = row_proj[:, R_XBC:R_XBC + SSM_CONV_DIM]
    dt_rows = jnp.pad(att[:, T_DT:T_DT + SSM_HEADS].reshape(b_sz, n_new, SSM_HEADS).transpose(0, 2, 1),
                      ((0, 0), (0, 0), (0, q_len - n_new)))
    conv8 = jnp.pad(conv_state, ((0, 0), (SUBLANES - (SSM_CONV - 1), 0), (0, 0)))
    ssm_pad, new_ssm = ssm_mixer(b_sz, q_len, 1, n_new, pad_t(xbc), 0, pad_t(row_proj[:, R_Z:R_Z + SSM_INNER]), 0,
                                 pad_t(dt_cols), dt_rows, conv8, ssm_state, ssm_w, F32)
    ssm_out = ssm_pad.reshape(b_sz, q_len, SSM_INNER)[:, :n_new].reshape(n, SSM_INNER)

    g_n = NSA_HEADS // NSA_KV_HEADS
    nq_dec = _heads_first(att[:, T_NQ:T_NKV], b_sz, n_new, (NSA_KV_HEADS, g_n)).reshape(b_sz, NSA_KV_HEADS, g_n * n_new, HEAD_DIM)
    sets = _heads_first(att[:, T_NKV:T_NGATE], b_sz, n_new, (6, NSA_KV_HEADS), SUBLANES)
    gates = att[:, T_NGATE:T_NGATE + 3 * NSA_HEADS].reshape(b_sz, n_new, 3, NSA_KV_HEADS, g_n)
    gates = gates.transpose(0, 3, 4, 1, 2).reshape(b_sz, NSA_KV_HEADS, g_n * n_new, 3)
    gates = jnp.pad(gates, ((0, 0), (0, 0), (0, 0), (0, LANES - 3)))
    kc_rows, _ = compress_paged(grp, nsa_cache, layer, page_table, cmp_w)
    o = nsa_decode(grp, nq_dec, (sets[:, 2], sets[:, 3], sets[:, 4], sets[:, 5]), gates, kc_rows, win_state,
                   nsa_cache, layer, page_table, past)
    nsa_out = o.reshape(b_sz, NSA_KV_HEADS, g_n, n_new, HEAD_DIM).transpose(0, 3, 1, 2, 4).reshape(n, -1)

    x1, h2 = merge_branches(grp, x, (moba_out, ssm_out, nsa_out), row_proj, w_branch, w_out, g_ffn, mod)
    x2, hn = _ffn(grp, x1, h2, ffn_w, mod, g_next, mod_next, last)

    new_moba = att[:, T_MK:T_NQ].reshape(b_sz, n_new, 2, MOBA_KV_HEADS, HEAD_DIM)
    new_nsa = att[:, T_NKV:T_NKV + 4 * NSA_KV_HEADS * HEAD_DIM].reshape(b_sz, n_new, 4, NSA_KV_HEADS, HEAD_DIM)
    win_rows = att[:, T_NKV + 4 * NSA_KV_HEADS * HEAD_DIM:T_NGATE].reshape(b_sz, n_new, 2, NSA_KV_HEADS, HEAD_DIM)
    win_len = grp["win_len"]
    new_win = jnp.concatenate([win_prev, win_rows], axis=1)[:, -win_len:]
    new_conv = jnp.concatenate([conv_state, xbc.reshape(b_sz, n_new, -1)], axis=1)[:, -(SSM_CONV - 1):]
    return x2, hn, (new_moba, new_nsa, new_win, new_ssm, new_conv)


def kernel(x_prompt, x_sample, c_prompt, c_sample, cache_moba_kv, cache_nsa_kv, state_nsa_win_kv, state_ssm, state_conv, page_table, w_ada, b_ada, g_mix, w_in, conv_w, conv_b, dt_bias, a_log, d_skip, g_ssm, cmp_pe, cmp_w1, cmp_b1, cmp_w2, cmp_b2, w_branch, w_out, g_ffn, w_ffn_gate, w_ffn_up, w_ffn_down, w_router, w_exp_gate, w_exp_up, w_exp_down, g_final):
    bp, seq, d = x_prompt.shape
    bs, n_new, _ = x_sample.shape
    depth = w_in.shape[0]
    n_pages, page = page_table.shape[1], cache_moba_kv.shape[2]
    past = n_pages * page
    win_len = state_nsa_win_kv.shape[2]
    assert page == LANES and past % MOBA_BLOCK == 0 and n_new <= SUBLANES and win_len == min(NSA_WINDOW, past)

    n_p, n_s = bp * seq, bs * n_new
    grp_p = dict(B=bp, L=seq, N=n_p, tm=_tile(seq, (512, 256, 128)), per_token_mod=False, win_len=win_len)
    grp_s = dict(B=bs, L=n_new, N=n_s, tm=_tile(n_s, (512, 256, 128)), per_token_mod=True, win_len=win_len)

    rows = bp + bs
    rows_pad = -(-rows // SUBLANES) * SUBLANES
    c_all = jnp.pad(jnp.concatenate([c_prompt, c_sample], axis=0), ((0, rows_pad - rows), (0, 0)))
    mod_all = ada_modulation(c_all, w_ada, b_ada)
    mod_p = [jnp.repeat(mod_all[l, :bp], SUBLANES, axis=0) for l in range(depth)]
    mod_s = [jnp.repeat(mod_all[l, bp:rows], n_new, axis=0) for l in range(depth)]

    w_att, w_row, w_dt, wb, wo, w1, w2t = _prep_layer_weights(w_in, w_branch, w_out, cmp_w1, cmp_w2)
    w_router_t = jnp.pad(jnp.swapaxes(w_router, 1, 2), ((0, 0), (0, LANES - N_EXPERTS), (0, 0))).astype(BF16)
    dense_w = (w_ffn_gate.astype(BF16), w_ffn_up.astype(BF16), w_ffn_down.astype(BF16))
    moe_w = (w_exp_gate.astype(BF16), w_exp_up.astype(BF16), w_exp_down.astype(BF16))

    moba_cache = jnp.transpose(cache_moba_kv, (0, 1, 3, 4, 5, 2))
    nsa_cache = jnp.transpose(cache_nsa_kv, (0, 1, 3, 4, 5, 2))
    win_state = jnp.transpose(state_nsa_win_kv, (0, 1, 3, 4, 5, 2))

    xp = x_prompt.reshape(n_p, d)
    xs = x_sample.reshape(n_s, d)
    hp = norm_modulate(grp_p, xp, g_mix[0], mod_p[0])
    hs = norm_modulate(grp_s, xs, g_mix[0], mod_s[0])
    st_p, st_s = [], []
    for l in range(depth):
        last = l == depth - 1
        if l % 2:
            ffn_w = (w_router_t[l // 2],) + tuple(w[l // 2] for w in moe_w)
        else:
            ffn_w = tuple(w[l // 2][None] for w in dense_w)
        cmp_w = (cmp_pe[l], w1[l], cmp_b1[l], w2t[l], cmp_b2[l])
        ssm_w = (conv_w[l], conv_b[l], dt_bias[l], a_log[l], d_skip[l], g_ssm[l])
        lw = (w_att[l], w_row[l], w_dt[l], wb[l], wo[l], cmp_w, ssm_w, g_ffn[l])
        g_next = g_final if last else g_mix[l + 1]
        xp, hp, new_p = _layer_prompt(grp_p, xp, hp, mod_p[l], None if last else mod_p[l + 1], lw, ffn_w, g_next, last)
        caches = (moba_cache, nsa_cache, win_state, state_nsa_win_kv[l], state_conv[l], state_ssm[l], l, page_table, past)
        xs, hs, new_s = _layer_sample(grp_s, xs, hs, mod_s[l], None if last else mod_s[l + 1], lw, ffn_w, g_next, last,
                                      caches)
        st_p.append(new_p)
        st_s.append(new_s)

    stack = lambda sts, k: jnp.stack([s[k] for s in sts])
    y_prompt = hp.reshape(bp, seq, d)
    y_sample = hs.reshape(bs, n_new, d)
    return (y_prompt, y_sample, stack(st_p, 0), stack(st_s, 0), stack(st_p, 1), stack(st_s, 1), stack(st_p, 2),
            stack(st_s, 2), stack(st_p, 3), stack(st_s, 3), stack(st_p, 4), stack(st_s, 4))
```

```python
import functools

import numpy as np
import jax
import jax.numpy as jnp
from jax import lax
from jax.experimental import pallas as pl
from jax.experimental.pallas import tpu as pltpu

F32 = jnp.float32
BF16 = jnp.bfloat16
I32 = jnp.int32

D_MODEL = 1024
HEAD_DIM = 64
MOBA_HEADS, MOBA_KV_HEADS, MOBA_BLOCK, MOBA_TOPK = 8, 4, 256, 3
SSM_HEADS, SSM_HEAD_DIM, SSM_INNER, SSM_GROUPS, SSM_STATE, SSM_CONV, SSM_CHUNK = 8, 64, 512, 2, 128, 4, 128
SSM_CONV_DIM = SSM_INNER + 2 * SSM_GROUPS * SSM_STATE
NSA_HEADS, NSA_KV_HEADS = 8, 2
NSA_CMP_LEN, NSA_CMP_STRIDE, NSA_CMP_HIDDEN = 32, 16, 256
NSA_SLC_BLOCK, NSA_TOPN, NSA_WINDOW = 64, 16, 512
N_BRANCHES, BRANCH_WIDTH = 3, 512
N_EXPERTS = 8
RMS_EPS = 1e-6
NEG_BIG = -1e30
ATT_SCALE = HEAD_DIM ** -0.5

LANES = 128
SUBLANES = 8
VMEM_LIMIT = 56 * 1024 * 1024

T_MQ, T_MK, T_MV, T_NQ, T_NKV, T_NGATE, T_DT = 0, 512, 768, 1024, 1536, 2304, 2328
T_ROWS = 2560
R_MGATE, R_XBC, R_Z = 0, 3072, 4096
R_COLS = 4608


def _cparams(sem):
    return pltpu.CompilerParams(dimension_semantics=sem, vmem_limit_bytes=VMEM_LIMIT)


def _silu(x):
    return x * (1.0 / (1.0 + jnp.exp(-x)))


def _sigmoid(x):
    return 1.0 / (1.0 + jnp.exp(-x))


def _softplus(x):
    return jnp.maximum(x, 0.0) + jnp.log(1.0 + jnp.exp(-jnp.abs(x)))


def _dot(a, b):
    return jnp.dot(a, b, preferred_element_type=F32)


def _dot_nt(a, b):
    return lax.dot_general(a, b, (((1,), (1,)), ((), ())), preferred_element_type=F32)


def _dot_tn(a, b):
    return lax.dot_general(a, b, (((0,), (0,)), ((), ())), preferred_element_type=F32)


def _dot_exact(a, b):
    return jnp.dot(a, b, preferred_element_type=F32, precision=lax.Precision.HIGHEST)


def _f01(mask):
    return jnp.where(mask, 1.0, 0.0)


def _mod_rows(ref, rows):
    m = ref[...]
    return m if m.shape[0] == rows else m[0:1]


def _rms(x):
    return x * lax.rsqrt(jnp.mean(x * x, axis=-1, keepdims=True) + RMS_EPS)


def _softmax_init(width):
    return (jnp.full((1, width), NEG_BIG, F32), jnp.zeros((1, width), F32), jnp.zeros((HEAD_DIM, width), F32))


def _softmax_steps(states, scores, values):
    partial = []
    for (m, l, acc), s in zip(states, scores):
        m_new = jnp.maximum(m, jnp.max(s, axis=0, keepdims=True))
        alpha = jnp.exp(m - m_new)
        p = jnp.exp(s - m_new)
        partial.append((m_new, alpha * l + jnp.sum(p, axis=0, keepdims=True), alpha * acc, p.astype(BF16)))
    return tuple((m, l, acc + _dot(v_t, p)) for (m, l, acc, p), v_t in zip(partial, values))


def _softmax_merge(states):
    m = states[0][0]
    for st in states[1:]:
        m = jnp.maximum(m, st[0])
    l = jnp.zeros_like(states[0][1])
    acc = jnp.zeros_like(states[0][2])
    for m_s, l_s, acc_s in states:
        w = jnp.exp(m_s - m)
        l = l + w * l_s
        acc = acc + w * acc_s
    return acc / l


def _mod_spec(grp, chunk, tm):
    if grp["per_token_mod"]:
        return pl.BlockSpec((tm, D_MODEL), lambda i, *_: (i, chunk))
    tiles_per_seq = grp["L"] // tm
    return pl.BlockSpec((SUBLANES, D_MODEL), lambda i, *_: (i // tiles_per_seq, chunk))


def _ada_kernel(c_ref, w_ref, b_ref, o_ref):
    a = _silu(c_ref[...]).astype(BF16)
    o_ref[0] = _dot(a, w_ref[0].astype(BF16)) + b_ref[0]


def ada_modulation(c_all, w_ada, b_ada):
    depth, d, n6 = w_ada.shape
    rows = c_all.shape[0]
    tn = 1536
    return pl.pallas_call(
        _ada_kernel,
        grid=(depth, n6 // tn),
        in_specs=[pl.BlockSpec((rows, d), lambda l, j: (0, 0)),
                  pl.BlockSpec((1, d, tn), lambda l, j: (l, 0, j)),
                  pl.BlockSpec((1, 1, tn), lambda l, j: (l, 0, j))],
        out_specs=pl.BlockSpec((1, rows, tn), lambda l, j: (l, 0, j)),
        out_shape=jax.ShapeDtypeStruct((depth, rows, n6), F32),
        compiler_params=_cparams(("arbitrary", "arbitrary")),
        name="ada_modulation",
    )(c_all, w_ada, b_ada.reshape(depth, 1, n6))


def _norm_kernel(x_ref, g_ref, sc_ref, sh_ref, o_ref):
    x = x_ref[...]
    rows = x.shape[0]
    y = _rms(x) * g_ref[...]
    o_ref[...] = (y * (1.0 + _mod_rows(sc_ref, rows)) + _mod_rows(sh_ref, rows)).astype(o_ref.dtype)


def norm_modulate(grp, x, g, mod):
    n, d = x.shape
    tm = grp["tm"]
    return pl.pallas_call(
        _norm_kernel,
        grid=(n // tm,),
        in_specs=[pl.BlockSpec((tm, d), lambda i: (i, 0)),
                  pl.BlockSpec((1, d), lambda i: (0, 0)),
                  _mod_spec(grp, 1, tm), _mod_spec(grp, 0, tm)],
        out_specs=pl.BlockSpec((tm, d), lambda i: (i, 0)),
        out_shape=jax.ShapeDtypeStruct((n, d), BF16),
        compiler_params=_cparams(("arbitrary",)),
        name="norm_modulate",
    )(x, g.reshape(1, d), mod, mod)


def _nt_kernel(a_ref, b_ref, o_ref):
    o_ref[...] = _dot_nt(a_ref[...], b_ref[...])


def nt_matmul(a, b, tm, tn):
    m, k = a.shape
    n = b.shape[0]
    assert m % tm == 0 and n % tn == 0, (a.shape, b.shape, tm, tn)
    return pl.pallas_call(
        _nt_kernel,
        grid=(m // tm, n // tn),
        in_specs=[pl.BlockSpec((tm, k), lambda i, j: (i, 0)),
                  pl.BlockSpec((tn, k), lambda i, j: (j, 0))],
        out_specs=pl.BlockSpec((tm, tn), lambda i, j: (i, j)),
        out_shape=jax.ShapeDtypeStruct((m, n), F32),
        compiler_params=_cparams(("arbitrary", "arbitrary")),
        name="nt_matmul",
    )(a, b)


def _merge_kernel(x_ref, b0_ref, b1_ref, b2_ref, mg_ref, wb_ref, wo_ref, g_ref, gate_ref, sc_ref, sh_ref,
                  x1_ref, h2_ref):
    rows = x_ref.shape[0]
    merged = jnp.zeros((rows, D_MODEL), F32)
    for n, b_ref in enumerate((b0_ref, b1_ref, b2_ref)):
        up = _dot(b_ref[...].astype(BF16), wb_ref[n])
        merged = merged + _sigmoid(mg_ref[:, n * D_MODEL:(n + 1) * D_MODEL]) * up
    y = _dot(merged.astype(BF16), wo_ref[...])
    x1 = x_ref[...] + _mod_rows(gate_ref, rows) * y
    x1_ref[...] = x1
    h2 = _rms(x1) * g_ref[...]
    h2_ref[...] = (h2 * (1.0 + _mod_rows(sc_ref, rows)) + _mod_rows(sh_ref, rows)).astype(BF16)


def merge_branches(grp, x, branches, row_proj, wb, wo, g_ffn, mod):
    n, d = x.shape
    tm = grp["tm"]
    bw = BRANCH_WIDTH
    row = lambda i: (i, 0)
    fixed2 = lambda i: (0, 0)
    return pl.pallas_call(
        _merge_kernel,
        grid=(n // tm,),
        in_specs=[pl.BlockSpec((tm, d), row),
                  pl.BlockSpec((tm, bw), row), pl.BlockSpec((tm, bw), row), pl.BlockSpec((tm, bw), row),
                  pl.BlockSpec((tm, N_BRANCHES * d), row),
                  pl.BlockSpec((N_BRANCHES, bw, d), lambda i: (0, 0, 0)),
                  pl.BlockSpec((d, d), fixed2),
                  pl.BlockSpec((1, d), fixed2),
                  _mod_spec(grp, 2, tm), _mod_spec(grp, 4, tm), _mod_spec(grp, 3, tm)],
        out_specs=[pl.BlockSpec((tm, d), row), pl.BlockSpec((tm, d), row)],
        out_shape=[jax.ShapeDtypeStruct((n, d), F32), jax.ShapeDtypeStruct((n, d), BF16)],
        compiler_params=_cparams(("arbitrary",)),
        name="merge_branches",
    )(x, *branches, row_proj, wb, wo, g_ffn.reshape(1, d), mod, mod, mod)


def _router_kernel(h_ref, w_ref, o_ref, ot_ref, slot_ref, slott_ref):
    logits = _dot_nt(h_ref[...], w_ref[...])
    lane = lax.broadcasted_iota(I32, logits.shape, 1)
    logits = jnp.where(lane < N_EXPERTS, logits, -jnp.inf)
    m1 = jnp.max(logits, axis=-1, keepdims=True)
    i1 = jnp.min(jnp.where(logits == m1, lane, LANES), axis=-1, keepdims=True)
    rest = jnp.where(lane == i1, -jnp.inf, logits)
    m2 = jnp.max(rest, axis=-1, keepdims=True)
    i2 = jnp.min(jnp.where(rest == m2, lane, LANES), axis=-1, keepdims=True)
    e2 = jnp.exp(m2 - m1)
    den = 1.0 + e2
    combine = jnp.where(lane == i1, 1.0 / den, 0.0) + jnp.where(lane == i2, e2 / den, 0.0)
    o_ref[...] = combine
    ot_ref[...] = combine.T
    tm = combine.shape[0]
    routed = combine > 0.0
    earlier = lax.broadcasted_iota(I32, (tm, tm), 1) < lax.broadcasted_iota(I32, (tm, tm), 0)
    before = _dot(_f01(earlier).astype(BF16), _f01(routed).astype(BF16))
    slot = jnp.where(routed, before, -1.0)
    slot_ref[...] = slot
    slott_ref[...] = slot.T


def moe_router(grp, h2, w_router_t):
    n, d = h2.shape
    tm = grp["tm"]
    by_tok = pl.BlockSpec((tm, LANES), lambda i: (i, 0))
    by_exp = pl.BlockSpec((LANES, tm), lambda i: (0, i))
    tok_shape, exp_shape = jax.ShapeDtypeStruct((n, LANES), F32), jax.ShapeDtypeStruct((LANES, n), F32)
    return pl.pallas_call(
        _router_kernel,
        grid=(n // tm,),
        in_specs=[pl.BlockSpec((tm, d), lambda i: (i, 0)), pl.BlockSpec((LANES, d), lambda i: (0, 0))],
        out_specs=[by_tok, by_exp, by_tok, by_exp],
        out_shape=[tok_shape, exp_shape, tok_shape, exp_shape],
        compiler_params=_cparams(("arbitrary",)),
        name="moe_router",
    )(h2, w_router_t)


MOE_CAP = 192


def _moe_kernel(cnt_ref, h_ref, cmbt_ref, slot_ref, slott_ref, wg_ref, wu_ref, wd_ref, y_ref):
    s_idx, e, f, j = (pl.program_id(k) for k in range(4))
    tm = h_ref.shape[0]
    i = s_idx * pl.num_programs(3) + j
    cap = MOE_CAP
    w_row = cmbt_ref[pl.ds(e, 1), :]
    slot_row = slott_ref[pl.ds(e, 1), :]
    slots = slot_ref[...]
    lane = lax.broadcasted_iota(I32, slots.shape, 1)
    slot_col = jnp.sum(jnp.where(lane == e, slots, 0.0), axis=-1, keepdims=True)
    h = h_ref[...]
    slot_r = lax.broadcasted_iota(I32, (cap, tm), 0).astype(F32)
    slot_c = lax.broadcasted_iota(I32, (tm, cap), 1).astype(F32)

    def chunk(k, acc):
        base = (k * cap).astype(F32)
        p = _f01(slot_row - base == slot_r)
        pt = _f01(slot_col - base == slot_c).astype(BF16)
        xc = _dot(p.astype(BF16), h).astype(BF16)
        a = _silu(_dot(xc, wg_ref[0])) * _dot(xc, wu_ref[0])
        out = _dot(a.astype(BF16), wd_ref[0])
        out = out * jnp.sum(p * w_row, axis=-1, keepdims=True)
        hi = out.astype(BF16)
        lo = (out - hi.astype(F32)).astype(BF16)
        return acc + _dot(pt, hi) + _dot(pt, lo)

    n_chunks = (cnt_ref[i, e] + cap - 1) // cap
    acc = lax.fori_loop(0, n_chunks, chunk, jnp.zeros((tm, D_MODEL), F32))
    rows = pl.ds(pl.multiple_of(j * tm, tm), tm)

    @pl.when((e == 0) & (f == 0))
    def _():
        y_ref[rows, :] = acc

    @pl.when((e > 0) | (f > 0))
    def _():
        y_ref[rows, :] += acc


def moe_experts(grp, h2, routing, wg, wu, wd):
    combine, combine_t, slot, slot_t = routing
    n, d = h2.shape
    n_exp, _, ff = wg.shape
    tm = grp["tm"]
    sup = _tile(n, (4 * tm, 2 * tm))
    n_j = sup // tm
    tf = _tile(ff, (1408,))
    counts = jnp.sum((combine[:, :n_exp] > 0.0).reshape(n // tm, tm, n_exp), axis=1).astype(I32)
    tok = lambda s, e, f, j, cnt: (s * n_j + j, 0)
    by_exp = pl.BlockSpec((LANES, tm), lambda s, e, f, j, cnt: (0, s * n_j + j))
    return pl.pallas_call(
        _moe_kernel,
        grid_spec=pltpu.PrefetchScalarGridSpec(
            num_scalar_prefetch=1, grid=(n // sup, n_exp, ff // tf, n_j),
            in_specs=[pl.BlockSpec((tm, d), tok), by_exp, pl.BlockSpec((tm, LANES), tok), by_exp,
                      pl.BlockSpec((1, d, tf), lambda s, e, f, j, cnt: (e, 0, f)),
                      pl.BlockSpec((1, d, tf), lambda s, e, f, j, cnt: (e, 0, f)),
                      pl.BlockSpec((1, tf, d), lambda s, e, f, j, cnt: (e, f, 0))],
            out_specs=pl.BlockSpec((sup, d), lambda s, e, f, j, cnt: (s, 0))),
        out_shape=jax.ShapeDtypeStruct((n, d), F32),
        compiler_params=_cparams(("arbitrary", "arbitrary", "arbitrary", "arbitrary")),
        name="moe_experts",
    )(counts, h2, combine_t, slot, slot_t, wg, wu, wd)


def _finish_kernel(x_ref, y_ref, gate_ref, g_ref, sc_ref, sh_ref, x2_ref, hn_ref):
    rows = x_ref.shape[0]
    x2 = x_ref[...] + _mod_rows(gate_ref, rows) * y_ref[...]
    x2_ref[...] = x2
    hn = _rms(x2) * g_ref[...]
    hn_ref[...] = (hn * (1.0 + _mod_rows(sc_ref, rows)) + _mod_rows(sh_ref, rows)).astype(hn_ref.dtype)


def _next_norm_specs(grp, tm, mod_next, last):
    if last:
        zero_mod = jnp.zeros((SUBLANES, D_MODEL), F32)
        spec = pl.BlockSpec((SUBLANES, D_MODEL), lambda i, *_: (0, 0))
        return [spec, spec], (zero_mod, zero_mod)
    return [_mod_spec(grp, 1, tm), _mod_spec(grp, 0, tm)], (mod_next, mod_next)


def ffn_finish(grp, x1, y, mod, g_next, mod_next, last):
    n, d = x1.shape
    tm = grp["tm"]
    row = lambda i: (i, 0)
    next_specs, next_args = _next_norm_specs(grp, tm, mod_next, last)
    return pl.pallas_call(
        _finish_kernel,
        grid=(n // tm,),
        in_specs=[pl.BlockSpec((tm, d), row), pl.BlockSpec((tm, d), row), _mod_spec(grp, 5, tm),
                  pl.BlockSpec((1, d), lambda i: (0, 0))] + next_specs,
        out_specs=[pl.BlockSpec((tm, d), row), pl.BlockSpec((tm, d), row)],
        out_shape=[jax.ShapeDtypeStruct((n, d), F32), jax.ShapeDtypeStruct((n, d), F32 if last else BF16)],
        compiler_params=_cparams(("arbitrary",)),
        name="ffn_finish",
    )(x1, y, mod, g_next.reshape(1, d), *next_args)


def _ffn_kernel(x_ref, h_ref, cmb_ref, wg_ref, wu_ref, wd_ref, gate_ref, g_ref, sc_ref, sh_ref,
                x2_ref, hn_ref, acc_ref):
    e, f = pl.program_id(1), pl.program_id(2)
    rows = x_ref.shape[0]

    @pl.when((e == 0) & (f == 0))
    def _():
        acc_ref[...] = jnp.zeros_like(acc_ref)

    h = h_ref[...]
    a = _silu(_dot(h, wg_ref[0])) * _dot(h, wu_ref[0])
    part = _dot(a.astype(BF16), wd_ref[0])
    cmb = cmb_ref[...]
    lane = lax.broadcasted_iota(I32, cmb.shape, 1)
    w = jnp.sum(jnp.where(lane == e, cmb, 0.0), axis=-1, keepdims=True)
    acc_ref[...] += w * part

    @pl.when((e == pl.num_programs(1) - 1) & (f == pl.num_programs(2) - 1))
    def _():
        x2 = x_ref[...] + _mod_rows(gate_ref, rows) * acc_ref[...]
        x2_ref[...] = x2
        hn = _rms(x2) * g_ref[...]
        hn_ref[...] = (hn * (1.0 + _mod_rows(sc_ref, rows)) + _mod_rows(sh_ref, rows)).astype(hn_ref.dtype)


def ffn_block(grp, x1, h2, combine, wg, wu, wd, mod, g_next, mod_next, last):
    n, d = x1.shape
    n_exp, _, ff = wg.shape
    tm = grp["tm"]
    tf = 1408 if ff % 1408 == 0 else ff
    row = lambda i, e, f: (i, 0)
    if last:
        zero_mod = jnp.zeros((SUBLANES, d), F32)
        nspec = pl.BlockSpec((SUBLANES, d), lambda i, e, f: (0, 0))
        next_specs, next_args = [nspec, nspec], (zero_mod, zero_mod)
    else:
        next_specs, next_args = [_mod_spec(grp, 1, tm), _mod_spec(grp, 0, tm)], (mod_next, mod_next)
    return pl.pallas_call(
        _ffn_kernel,
        grid=(n // tm, n_exp, ff // tf),
        in_specs=[pl.BlockSpec((tm, d), row), pl.BlockSpec((tm, d), row), pl.BlockSpec((tm, LANES), row),
                  pl.BlockSpec((1, d, tf), lambda i, e, f: (e, 0, f)),
                  pl.BlockSpec((1, d, tf), lambda i, e, f: (e, 0, f)),
                  pl.BlockSpec((1, tf, d), lambda i, e, f: (e, f, 0)),
                  _mod_spec(grp, 5, tm),
                  pl.BlockSpec((1, d), lambda i, e, f: (0, 0))] + next_specs,
        out_specs=[pl.BlockSpec((tm, d), row), pl.BlockSpec((tm, d), row)],
        out_shape=[jax.ShapeDtypeStruct((n, d), F32), jax.ShapeDtypeStruct((n, d), F32 if last else BF16)],
        scratch_shapes=[pltpu.VMEM((tm, d), F32)],
        compiler_params=_cparams(("arbitrary", "arbitrary", "arbitrary")),
        name="ffn_block",
    )(x1, h2, combine, wg, wu, wd, mod, g_next.reshape(1, d), *next_args)


def _moba_kernel(q_ref, k_ref, v_ref, o_ref, kb_ref, vb_ref, km_ref, sel_ref, *, nb):
    qi = pl.program_id(2)
    blk = MOBA_BLOCK
    nbp = sel_ref.shape[1]

    @pl.when(qi == 0)
    def _():
        km = jnp.zeros((HEAD_DIM, LANES), F32)
        lane = lax.broadcasted_iota(I32, (HEAD_DIM, LANES), 1)
        for j in range(nb):
            kt = k_ref[:, j * blk:(j + 1) * blk]
            kb_ref[j] = kt.T.astype(BF16)
            vb_ref[j] = v_ref[:, j * blk:(j + 1) * blk].astype(BF16)
            mean = jnp.sum(kt, axis=1, keepdims=True) * (1.0 / blk)
            km = jnp.where(lane == j, mean, km)
        km_ref[...] = km.astype(BF16)

    g_sz = MOBA_HEADS // MOBA_KV_HEADS
    sub = lax.broadcasted_iota(I32, (nbp, blk), 0)
    qts = []
    for g in range(g_sz):
        q = q_ref[g * HEAD_DIM:(g + 1) * HEAD_DIM, :]
        gate = _dot_tn(km_ref[...], q.astype(BF16))[:nbp]
        cnt = jnp.zeros((nbp, blk), F32)
        for j2 in range(nb):
            row = gate[j2:j2 + 1, :]
            beats = _f01((row > gate) | ((row == gate) & (j2 < sub)))
            cnt = cnt + beats * _f01(j2 < qi)
        sel_ref[g] = _f01((sub < qi) & (cnt < MOBA_TOPK))
        qts.append((q * ATT_SCALE).astype(BF16))

    init = _softmax_init(blk)
    streams = [(g, par) for g in range(g_sz) for par in range(2)]

    def past_pair(i, carry):
        scores, values = [], []
        for g, par in streams:
            j = 2 * i + par
            jc = jnp.minimum(j, qi - 1)
            chosen = sel_ref[g, pl.ds(jc, 1), :] * _f01(j < qi)
            scores.append(_dot(kb_ref[jc], qts[g]) + (chosen - 1.0) * (-NEG_BIG))
            values.append(vb_ref[jc])
        return _softmax_steps(carry, scores, values)

    states = lax.fori_loop(0, (qi + 1) // 2, past_pair, (init,) * len(streams))
    krow = lax.broadcasted_iota(I32, (blk, blk), 0)
    qcol = lax.broadcasted_iota(I32, (blk, blk), 1)
    causal_bias = jnp.where(krow <= qcol, 0.0, NEG_BIG)
    own = _softmax_steps((init,) * g_sz, [_dot(kb_ref[qi], qts[g]) + causal_bias for g in range(g_sz)],
                         [vb_ref[qi]] * g_sz)
    outs = [_softmax_merge([own[g], states[2 * g], states[2 * g + 1]]).T for g in range(g_sz)]
    o_ref[...] = jnp.concatenate(outs, axis=1).astype(o_ref.dtype)


def moba_prompt(grp, proj_t):
    b_sz, seq = grp["B"], grp["L"]
    assert seq % MOBA_BLOCK == 0
    nb = seq // MOBA_BLOCK
    nbp = -(-nb // SUBLANES) * SUBLANES
    g = MOBA_HEADS // MOBA_KV_HEADS
    return pl.pallas_call(
        functools.partial(_moba_kernel, nb=nb),
        grid=(b_sz, MOBA_KV_HEADS, nb),
        in_specs=[pl.BlockSpec((g * HEAD_DIM, MOBA_BLOCK), lambda b, h, i: (T_MQ // (g * HEAD_DIM) + h, b * nb + i)),
                  pl.BlockSpec((HEAD_DIM, seq), lambda b, h, i: (T_MK // HEAD_DIM + h, b)),
                  pl.BlockSpec((HEAD_DIM, seq), lambda b, h, i: (T_MV // HEAD_DIM + h, b))],
        out_specs=pl.BlockSpec((MOBA_BLOCK, g * HEAD_DIM), lambda b, h, i: (b * nb + i, h)),
        out_shape=jax.ShapeDtypeStruct((b_sz * seq, MOBA_HEADS * HEAD_DIM), BF16),
        scratch_shapes=[pltpu.VMEM((nb, MOBA_BLOCK, HEAD_DIM), BF16), pltpu.VMEM((nb, HEAD_DIM, MOBA_BLOCK), BF16),
                        pltpu.VMEM((HEAD_DIM, LANES), BF16), pltpu.VMEM((g, nbp, MOBA_BLOCK), F32)],
        compiler_params=_cparams(("arbitrary", "arbitrary", "arbitrary")),
        name="moba_prompt",
    )(proj_t, proj_t, proj_t)


def _compress_body(get_tile, n_tiles, pe_ref, w1_ref, b1_ref, w2t_ref, b2r_ref, b2c_ref,
                   orow_ref, ot_ref, r_ref, x_ref):
    ncp = x_ref.shape[0]
    rows = n_tiles * LANES
    for sh in range(2 * NSA_KV_HEADS):
        st = sh // NSA_KV_HEADS
        for t in range(n_tiles):
            r_ref[t * LANES:(t + 1) * LANES, :] = get_tile(sh, t).T
        r_ref[rows:rows + NSA_CMP_STRIDE, :] = jnp.zeros((NSA_CMP_STRIDE, HEAD_DIM), F32)
        for lp in range(NSA_CMP_LEN // 2):
            pieces = []
            for l in (2 * lp, 2 * lp + 1):
                pieces.append(r_ref[pl.ds(l, ncp, stride=NSA_CMP_STRIDE), :] + pe_ref[st, l:l + 1, :])
            x_ref[:, lp * LANES:(lp + 1) * LANES] = jnp.concatenate(pieces, axis=1).astype(BF16)
        pre = _dot(x_ref[...], w1_ref[st]) + b1_ref[st]
        hid = _silu(pre).astype(BF16)
        orow_ref[0, sh] = (_dot_nt(hid, w2t_ref[st]) + b2r_ref[st]).astype(orow_ref.dtype)
        ot_ref[0, sh] = (_dot_nt(w2t_ref[st], hid) + b2c_ref[st]).astype(ot_ref.dtype)


def _compress_prompt_kernel(src_ref, *rest, n_tiles):
    get = lambda sh, t: src_ref[sh * HEAD_DIM:(sh + 1) * HEAD_DIM, t * LANES:(t + 1) * LANES]
    _compress_body(get, n_tiles, *rest)


def _compress_paged_kernel(pt_ref, *rest, n_tiles):
    pages, rest = rest[:n_tiles], rest[n_tiles:]
    get = lambda sh, t: pages[t][0, 0, sh // NSA_KV_HEADS, sh % NSA_KV_HEADS]
    _compress_body(get, n_tiles, *rest)


def _compress_call(kernel, b_sz, n_tiles, src_specs, src_args, cmp_w, prefetch=()):
    pe, w1, b1, w2t, b2 = cmp_w
    ncp = n_tiles * LANES // NSA_CMP_STRIDE
    full = lambda shape: pl.BlockSpec(shape, lambda b, *_: (0,) * len(shape))
    in_specs = src_specs + [full(pe.shape), full(w1.shape), full((2, 1, NSA_CMP_HIDDEN)), full(w2t.shape),
                            full((2, 1, HEAD_DIM)), full((2, HEAD_DIM, 1))]
    nsh = 2 * NSA_KV_HEADS
    out_specs = [pl.BlockSpec((1, nsh, ncp, HEAD_DIM), lambda b, *_: (b, 0, 0, 0)),
                 pl.BlockSpec((1, nsh, HEAD_DIM, ncp), lambda b, *_: (b, 0, 0, 0))]
    out_shape = [jax.ShapeDtypeStruct((b_sz, nsh, ncp, HEAD_DIM), BF16),
                 jax.ShapeDtypeStruct((b_sz, nsh, HEAD_DIM, ncp), BF16)]
    scratch = [pltpu.VMEM((n_tiles * LANES + NSA_CMP_STRIDE, HEAD_DIM), F32),
               pltpu.VMEM((ncp, NSA_CMP_LEN * HEAD_DIM), BF16)]
    args = src_args + [pe, w1, b1.reshape(2, 1, -1), w2t, b2.reshape(2, 1, -1), b2.reshape(2, -1, 1)]
    return pl.pallas_call(
        functools.partial(kernel, n_tiles=n_tiles),
        grid_spec=pltpu.PrefetchScalarGridSpec(num_scalar_prefetch=len(prefetch), grid=(b_sz,), in_specs=in_specs,
                                               out_specs=out_specs, scratch_shapes=scratch),
        out_shape=out_shape,
        compiler_params=_cparams(("arbitrary",)),
        name="nsa_compress",
    )(*prefetch, *args)


def compress_prompt(grp, proj_t, cmp_w):
    seq = grp["L"]
    rows = 2 * NSA_KV_HEADS * HEAD_DIM
    spec = pl.BlockSpec((rows, seq), lambda b: (T_NKV // rows, b))
    return _compress_call(_compress_prompt_kernel, grp["B"], seq // LANES, [spec], [proj_t], cmp_w)


def compress_paged(grp, cache, layer, page_table, cmp_w):
    n_pages = page_table.shape[1]
    specs = _page_specs((1, 1, 2, NSA_KV_HEADS, HEAD_DIM, LANES), n_pages, layer, (0, 0, 0, 0))
    return _compress_call(_compress_paged_kernel, grp["B"], n_pages, specs, [cache] * n_pages, cmp_w,
                          prefetch=(page_table,))


def _nsa_kernel(q_ref, ks_ref, vs_ref, kw_ref, vw_ref, gt_ref, kc_ref, vct_ref, ov_ref, o_ref,
                ksr_ref, vsb_ref, kwr_ref, vwb_ref, sel_ref, *, seq, ns, nc):
    kvh, qc = pl.program_id(1), pl.program_id(2)
    n_tiles = seq // LANES
    g_sz = NSA_HEADS // NSA_KV_HEADS
    width = g_sz * LANES
    ncp = kc_ref.shape[2]
    nsp = sel_ref.shape[0]

    @pl.when(qc == 0)
    def _():
        for t in range(n_tiles):
            sl = slice(t * LANES, (t + 1) * LANES)
            ksr_ref[t] = ks_ref[:, sl].T.astype(BF16)
            vsb_ref[t] = vs_ref[:, sl].astype(BF16)
            kwr_ref[t] = kw_ref[:, sl].T.astype(BF16)
            vwb_ref[t] = vw_ref[:, sl].astype(BF16)

    q4 = q_ref[...]
    qts = jnp.concatenate([q4[g * HEAD_DIM:(g + 1) * HEAD_DIM, :] for g in range(g_sz)], axis=1)
    qts = (qts * ATT_SCALE).astype(BF16)
    lane_w = lax.broadcasted_iota(I32, (1, width), 1)
    qpos_w = qc * LANES + (lane_w & (LANES - 1))
    qpos = qpos_w[:, :LANES]

    s = _dot(kc_ref[0, 0], qts)
    cidx = lax.broadcasted_iota(I32, (ncp, width), 0)
    valid = _f01((cidx * NSA_CMP_STRIDE + (NSA_CMP_LEN - 1) <= qpos_w) & (cidx < nc))
    s = jnp.where(valid > 0.5, s, NEG_BIG)
    p = jnp.exp(s - jnp.max(s, axis=0, keepdims=True)) * valid
    l = jnp.sum(p, axis=0, keepdims=True)
    pb = (p / jnp.where(l > 0.0, l, 1.0)).astype(BF16)
    o_cmp = _dot(vct_ref[0, 0], pb)
    ps = _dot(ov_ref[...], pb)
    p_slc = ps[:, 0:LANES]
    for g in range(1, g_sz):
        p_slc = p_slc + ps[:, g * LANES:(g + 1) * LANES]

    j = lax.broadcasted_iota(I32, (nsp, LANES), 0)
    cur = qpos // NSA_SLC_BLOCK
    forced = (j == 0) | (j == cur) | (j == cur - 1)
    elig = j <= cur
    score = jnp.where(forced, jnp.inf, p_slc)
    score = jnp.where(elig, score, -jnp.inf)
    cnt = jnp.zeros((nsp, LANES), F32)
    for j2 in range(ns):
        row = score[j2:j2 + 1, :]
        cnt = cnt + _f01((row > score) | ((row == score) & (j2 < j)))
    sel_ref[...] = _f01(elig & (cnt < NSA_TOPN))

    sub = lax.broadcasted_iota(I32, (LANES, LANES), 0)
    init = _softmax_init(width)

    def tile_scores(k_ref, t, allowed):
        bias = jnp.concatenate([(allowed - 1.0) * (-NEG_BIG)] * g_sz, axis=1)
        return _dot(k_ref[t], qts) + bias

    def slc_allowed(t, in_range):
        blocks_per_tile = LANES // NSA_SLC_BLOCK
        r0 = sel_ref[pl.ds(blocks_per_tile * t, 1), :]
        r1 = sel_ref[pl.ds(blocks_per_tile * t + 1, 1), :]
        chosen = jnp.where(sub < NSA_SLC_BLOCK, r0, r1)
        return chosen * _f01(t * LANES + sub <= qpos) * in_range

    n_streams = 4

    def slc_group(i, carry):
        scores, values = [], []
        for k in range(n_streams):
            t = i * n_streams + k
            tc = jnp.minimum(t, qc)
            scores.append(tile_scores(ksr_ref, tc, slc_allowed(tc, _f01(t <= qc))))
            values.append(vsb_ref[tc])
        return _softmax_steps(carry, scores, values)

    slc_states = lax.fori_loop(0, (qc + n_streams) // n_streams, slc_group, (init,) * n_streams)
    o_slc = _softmax_merge(list(slc_states))

    scores, values = [], []
    for k in range(NSA_WINDOW // LANES + 1):
        t = qc - k
        tc = jnp.maximum(t, 0)
        dist = qpos - (tc * LANES + sub)
        allowed = _f01((dist >= 0) & (dist <= NSA_WINDOW)) * _f01(t >= 0)
        scores.append(tile_scores(kwr_ref, tc, allowed))
        values.append(vwb_ref[tc])
    o_win = _softmax_merge(list(_softmax_steps((init,) * len(scores), scores, values)))

    outs = []
    for g in range(g_sz):
        head = kvh * g_sz + g
        gates = [_sigmoid(gt_ref[pl.ds(br * NSA_HEADS + head, 1), :]) for br in range(3)]
        sl = slice(g * LANES, (g + 1) * LANES)
        o = gates[0] * o_cmp[:, sl] + gates[1] * o_slc[:, sl] + gates[2] * o_win[:, sl]
        outs.append(o.T)
    o_ref[...] = jnp.concatenate(outs, axis=1).astype(o_ref.dtype)


def _overlap_matrix(ncp, nsp):
    c0 = np.arange(ncp)[:, None] * NSA_CMP_STRIDE
    s0 = np.arange(nsp)[None, :] * NSA_SLC_BLOCK
    return ((c0 < s0 + NSA_SLC_BLOCK) & (c0 + NSA_CMP_LEN > s0)).astype(np.float32)


def nsa_prompt(grp, proj_t, kc_rows, vc_t):
    b_sz, seq = grp["B"], grp["L"]
    assert seq % LANES == 0 and seq >= NSA_CMP_LEN
    nqc = seq // LANES
    g_sz = NSA_HEADS // NSA_KV_HEADS
    ns = seq // NSA_SLC_BLOCK
    nsp = -(-ns // SUBLANES) * SUBLANES
    nc = (seq - NSA_CMP_LEN) // NSA_CMP_STRIDE + 1
    ncp = kc_rows.shape[2]
    overlap_t = jnp.asarray(_overlap_matrix(ncp, nsp).T, BF16)
    kv_spec = lambda st: pl.BlockSpec((HEAD_DIM, seq), lambda b, h, i: (T_NKV // HEAD_DIM + st * NSA_KV_HEADS + h, b))
    tile_rows = pltpu.VMEM((nqc, LANES, HEAD_DIM), BF16)
    tile_cols = pltpu.VMEM((nqc, HEAD_DIM, LANES), BF16)
    return pl.pallas_call(
        functools.partial(_nsa_kernel, seq=seq, ns=ns, nc=nc),
        grid=(b_sz, NSA_KV_HEADS, nqc),
        in_specs=[pl.BlockSpec((g_sz * HEAD_DIM, LANES), lambda b, h, i: (T_NQ // (g_sz * HEAD_DIM) + h, b * nqc + i)),
                  kv_spec(2), kv_spec(3), kv_spec(4), kv_spec(5),
                  pl.BlockSpec((HEAD_DIM, LANES), lambda b, h, i: (T_NGATE // HEAD_DIM, b * nqc + i)),
                  pl.BlockSpec((1, 1, ncp, HEAD_DIM), lambda b, h, i: (b, h, 0, 0)),
                  pl.BlockSpec((1, 1, HEAD_DIM, ncp), lambda b, h, i: (b, NSA_KV_HEADS + h, 0, 0)),
                  pl.BlockSpec((nsp, ncp), lambda b, h, i: (0, 0))],
        out_specs=pl.BlockSpec((LANES, g_sz * HEAD_DIM), lambda b, h, i: (b * nqc + i, h)),
        out_shape=jax.ShapeDtypeStruct((b_sz * seq, NSA_HEADS * HEAD_DIM), BF16),
        scratch_shapes=[tile_rows, tile_cols, tile_rows, tile_cols, pltpu.VMEM((nsp, LANES), F32)],
        compiler_params=_cparams(("arbitrary", "arbitrary", "arbitrary")),
        name="nsa_prompt",
    )(proj_t, proj_t, proj_t, proj_t, proj_t, proj_t, kc_rows, vc_t, overlap_t)


def _ssm_kernel(xbc_ref, z_ref, dtc_ref, dtr_ref, cs_ref, h0_ref, cw_ref, cb_ref, dbr_ref, alr_ref, dbc_ref, alc_ref,
                gs_ref, dsk_ref, y_ref, hf_ref, tail_ref, h_ref, ybuf_ref, *, n_valid):
    c = pl.program_id(1)
    q_len = xbc_ref.shape[0]

    @pl.when(c == 0)
    def _():
        tail_ref[...] = cs_ref[0]
        h_ref[...] = h0_ref[0]

    x = xbc_ref[...]
    tail = tail_ref[...]
    row8 = lax.broadcasted_iota(I32, tail.shape, 0)
    conv = cb_ref[...] + x * cw_ref[SSM_CONV - 1:SSM_CONV, :]
    for k in range(1, SSM_CONV):
        xs = pltpu.roll(x, k, 0)
        first = jnp.where(row8 < k, pltpu.roll(tail, k, 0), xs[:SUBLANES])
        xs = first if q_len == SUBLANES else jnp.concatenate([first, xs[SUBLANES:]], axis=0)
        conv = conv + xs * cw_ref[SSM_CONV - 1 - k:SSM_CONV - k, :]
    tail_ref[...] = x[q_len - SUBLANES:, :]
    act = _silu(conv)
    gn = SSM_GROUPS * SSM_STATE
    b_all = act[:, SSM_INNER:SSM_INNER + gn].astype(BF16)
    c_all = act[:, SSM_INNER + gn:].astype(BF16)

    rows_q = lax.broadcasted_iota(I32, (q_len, LANES), 0)
    dt_c = _softplus(dtc_ref[...] + dbr_ref[...]) * _f01(rows_q < n_valid)
    a_c = dt_c * (-jnp.exp(alr_ref[...]))
    cols_q = lax.broadcasted_iota(I32, (SSM_HEADS, q_len), 1)
    dt_r = _softplus(dtr_ref[0] + dbc_ref[...]) * _f01(cols_q < n_valid)
    a_r = dt_r * (-jnp.exp(alc_ref[...]))
    ti = lax.broadcasted_iota(I32, (q_len, q_len), 0)
    si = lax.broadcasted_iota(I32, (q_len, q_len), 1)
    causal = ti >= si
    acs_c = _dot_exact(_f01(causal), a_c)
    acs_r = _dot_exact(a_r, _f01(si >= ti))

    cb = []
    for gi in range(SSM_GROUPS):
        sl = slice(gi * SSM_STATE, (gi + 1) * SSM_STATE)
        cb.append(_dot_nt(c_all[:, sl], b_all[:, sl]))
    hpg = SSM_HEADS // SSM_GROUPS
    for h in range(SSM_HEADS):
        gi = h // hpg
        sl = slice(gi * SSM_STATE, (gi + 1) * SSM_STATE)
        col = acs_c[:, h:h + 1]
        decay = jnp.where(causal, jnp.exp(col - acs_r[h:h + 1, :]), 0.0)
        xh = act[:, h * SSM_HEAD_DIM:(h + 1) * SSM_HEAD_DIM]
        xdt = xh * dt_c[:, h:h + 1]
        y = _dot((cb[gi] * decay).astype(BF16), xdt.astype(BF16))
        h_prev = h_ref[h]
        y = y + _dot_nt(c_all[:, sl], h_prev.astype(BF16)) * jnp.exp(col)
        last = acs_c[q_len - 1:q_len, h:h + 1]
        upd = _dot_tn((xdt * jnp.exp(last - col)).astype(BF16), b_all[:, sl])
        h_ref[h] = jnp.exp(last) * h_prev + upd
        ybuf_ref[:, h * SSM_HEAD_DIM:(h + 1) * SSM_HEAD_DIM] = y + xh * dsk_ref[h]

    yz = ybuf_ref[...] * _silu(z_ref[...])
    y_ref[...] = (_rms(yz) * gs_ref[...]).astype(y_ref.dtype)

    @pl.when(c == pl.num_programs(1) - 1)
    def _():
        hf_ref[0] = h_ref[...]


def ssm_mixer(b_sz, q_len, n_chunks, n_valid, xbc, xbc_col, z, z_col, dt_cols, dt_rows, conv_state8, h0, ssm_w,
              out_dtype):
    conv_w, conv_b, dt_bias, a_log, d_skip, g_ssm = ssm_w
    tok = lambda col: (lambda b, c: (b * n_chunks + c, col))
    fixed = lambda shape: pl.BlockSpec(shape, lambda b, c: (0,) * len(shape))
    pad_lanes = lambda v: jnp.pad(v.reshape(1, -1), ((0, 0), (0, LANES - v.shape[0])))
    state_shape = (SSM_HEADS, SSM_HEAD_DIM, SSM_STATE)
    return pl.pallas_call(
        functools.partial(_ssm_kernel, n_valid=n_valid),
        grid=(b_sz, n_chunks),
        in_specs=[pl.BlockSpec((q_len, SSM_CONV_DIM), tok(xbc_col)),
                  pl.BlockSpec((q_len, SSM_INNER), tok(z_col)),
                  pl.BlockSpec((q_len, LANES), tok(0)),
                  pl.BlockSpec((1, SSM_HEADS, q_len), lambda b, c: (b * n_chunks + c, 0, 0)),
                  pl.BlockSpec((1, SUBLANES, SSM_CONV_DIM), lambda b, c: (b, 0, 0)),
                  pl.BlockSpec((1,) + state_shape, lambda b, c: (b, 0, 0, 0)),
                  fixed((SSM_CONV, SSM_CONV_DIM)), fixed((1, SSM_CONV_DIM)),
                  fixed((1, LANES)), fixed((1, LANES)), fixed((SSM_HEADS, 1)), fixed((SSM_HEADS, 1)),
                  fixed((1, SSM_INNER)),
                  pl.BlockSpec(memory_space=pltpu.SMEM)],
        out_specs=[pl.BlockSpec((q_len, SSM_INNER), tok(0)),
                   pl.BlockSpec((1,) + state_shape, lambda b, c: (b, 0, 0, 0))],
        out_shape=[jax.ShapeDtypeStruct((b_sz * n_chunks * q_len, SSM_INNER), out_dtype),
                   jax.ShapeDtypeStruct((b_sz,) + state_shape, F32)],
        scratch_shapes=[pltpu.VMEM((SUBLANES, SSM_CONV_DIM), F32), pltpu.VMEM(state_shape, F32),
                        pltpu.VMEM((q_len, SSM_INNER), F32)],
        compiler_params=_cparams(("arbitrary", "arbitrary")),
        name="ssm_mixer",
    )(xbc, z, dt_cols, dt_rows, conv_state8, h0, conv_w, conv_b.reshape(1, -1), pad_lanes(dt_bias), pad_lanes(a_log),
      dt_bias.reshape(-1, 1), a_log.reshape(-1, 1), g_ssm.reshape(1, -1), d_skip)


def _moba_dec_kernel(pt_ref, q_ref, kn_ref, vn_ref, *rest, n_pages, n_new):
    pages, o_ref = rest[:n_pages], rest[n_pages]
    rows = q_ref.shape[2]
    ppb = MOBA_BLOCK // LANES
    nb_past = n_pages // ppb
    lane = lax.broadcasted_iota(I32, (rows, LANES), 1)
    lane_k = lax.broadcasted_iota(I32, (HEAD_DIM, LANES), 1)
    t_row = lax.broadcasted_iota(I32, (rows, SUBLANES), 0) % n_new
    t_col = lax.broadcasted_iota(I32, (rows, SUBLANES), 1)
    new_ok = _f01((t_col <= t_row) & (t_col < n_new))
    for kvh in range(MOBA_KV_HEADS):
        q = q_ref[0, kvh].astype(BF16)
        scores = []
        km = jnp.zeros((HEAD_DIM, LANES), F32)
        for j in range(nb_past):
            ksum = jnp.zeros((HEAD_DIM, LANES), F32)
            for pp in range(ppb):
                kt = pages[j * ppb + pp][0, 0, 0, kvh]
                scores.append(_dot(q, kt.astype(BF16)) * ATT_SCALE)
                ksum = ksum + kt
            mean = jnp.sum(ksum, axis=1, keepdims=True) * (1.0 / MOBA_BLOCK)
            km = jnp.where(lane_k == j, mean, km)
        gate = jnp.where(lane < nb_past, _dot(q, km.astype(BF16)), -jnp.inf)
        cnt = jnp.zeros((rows, LANES), F32)
        for j2 in range(nb_past):
            col = gate[:, j2:j2 + 1]
            cnt = cnt + _f01((col > gate) | ((col == gate) & (j2 < lane)))
        sel = _f01((lane < nb_past) & (cnt < MOBA_TOPK))
        s_new = jnp.where(new_ok > 0.5, _dot_nt(q, kn_ref[0, kvh].astype(BF16)) * ATT_SCALE, NEG_BIG)
        m = jnp.max(s_new, axis=1, keepdims=True)
        chosen = []
        for p in range(n_pages):
            a = sel[:, p // ppb:p // ppb + 1]
            scores[p] = jnp.where(a > 0.5, scores[p], NEG_BIG)
            chosen.append(a)
            m = jnp.maximum(m, jnp.max(scores[p], axis=1, keepdims=True))
        p_new = jnp.exp(s_new - m) * new_ok
        l = jnp.sum(p_new, axis=1, keepdims=True)
        o = _dot(p_new.astype(BF16), vn_ref[0, kvh].astype(BF16))
        for p in range(n_pages):
            pr = jnp.exp(scores[p] - m) * chosen[p]
            l = l + jnp.sum(pr, axis=1, keepdims=True)
            o = o + _dot_nt(pr.astype(BF16), pages[p][0, 0, 1, kvh].astype(BF16))
        o_ref[0, kvh] = o / l


def _page_specs(block, n_pages, layer, block_idx):
    return [pl.BlockSpec(block, functools.partial(lambda b, pt, j: (layer, pt[b, j]) + block_idx, j=j))
            for j in range(n_pages)]


def moba_decode(grp, q_dec, k_new, v_new, cache, layer, page_table):
    b_sz = grp["B"]
    n_pages = page_table.shape[1]
    per_seq = lambda shape: pl.BlockSpec((1,) + shape, lambda b, pt: (b,) + (0,) * len(shape))
    q_shape, n_shape = q_dec.shape[1:], k_new.shape[1:]
    specs = [per_seq(q_shape), per_seq(n_shape), per_seq(n_shape)]
    specs += _page_specs((1, 1, 2, MOBA_KV_HEADS, HEAD_DIM, LANES), n_pages, layer, (0, 0, 0, 0))
    return pl.pallas_call(
        functools.partial(_moba_dec_kernel, n_pages=n_pages, n_new=grp["L"]),
        grid_spec=pltpu.PrefetchScalarGridSpec(num_scalar_prefetch=1, grid=(b_sz,), in_specs=specs,
                                               out_specs=per_seq(q_shape)),
        out_shape=jax.ShapeDtypeStruct(q_dec.shape, F32),
        compiler_params=_cparams(("arbitrary",)),
        name="moba_decode",
    )(page_table, q_dec, k_new, v_new, *([cache] * n_pages))


def _nsa_dec_kernel(pt_ref, q_ref, kn_ref, vn_ref, kwn_ref, vwn_ref, gt_ref, kc_ref, win_ref, ov_ref, ex_ref, *rest,
                    n_pages, n_new, past, ns, nc):
    pages, o_ref = rest[:n_pages], rest[n_pages]
    rows = q_ref.shape[2]
    ncp = kc_ref.shape[2]
    wlen = win_ref.shape[-1]
    t_rowl = lax.broadcasted_iota(I32, (rows, LANES), 0) % n_new
    lane = lax.broadcasted_iota(I32, (rows, LANES), 1)
    t_row8 = lax.broadcasted_iota(I32, (rows, SUBLANES), 0) % n_new
    t_col8 = lax.broadcasted_iota(I32, (rows, SUBLANES), 1)
    new_ok = _f01((t_col8 <= t_row8) & (t_col8 < n_new))

    def softmax_parts(parts):
        m = None
        masked = []
        for s, a in parts:
            s = jnp.where(a > 0.5, s, NEG_BIG)
            masked.append(s)
            mx = jnp.max(s, axis=1, keepdims=True)
            m = mx if m is None else jnp.maximum(m, mx)
        ps = [jnp.exp(s - m) * a for s, (_, a) in zip(masked, parts)]
        l = ps[0].sum(axis=1, keepdims=True)
        for p in ps[1:]:
            l = l + p.sum(axis=1, keepdims=True)
        inv = 1.0 / jnp.where(l > 0.0, l, 1.0)
        return [p * inv for p in ps]

    for kvh in range(NSA_KV_HEADS):
        q = q_ref[0, kvh].astype(BF16)
        qpos = past + t_rowl
        cidx = lax.broadcasted_iota(I32, (rows, ncp), 1)
        qpos_c = past + lax.broadcasted_iota(I32, (rows, ncp), 0) % n_new
        valid = _f01((cidx * NSA_CMP_STRIDE + (NSA_CMP_LEN - 1) <= qpos_c) & (cidx < nc))
        s = _dot_nt(q, kc_ref[0, kvh]) * ATT_SCALE
        (p_cmp,) = softmax_parts([(s, valid)])
        pb = p_cmp.astype(BF16)
        o_cmp = _dot(pb, kc_ref[0, NSA_KV_HEADS + kvh])
        ps = _dot(pb, ov_ref[...])
        p_slc = ps
        for g in range(1, rows // n_new):
            p_slc = p_slc + pltpu.roll(ps, g * n_new, 0)
        cur = qpos // NSA_SLC_BLOCK
        forced = (lane == 0) | (lane == cur) | (lane == cur - 1)
        elig = lane <= cur
        score = jnp.where(forced, jnp.inf, p_slc)
        score = jnp.where(elig, score, -jnp.inf)
        cnt = jnp.zeros((rows, LANES), F32)
        for j2 in range(ns):
            col = score[:, j2:j2 + 1]
            cnt = cnt + _f01((col > score) | ((col == score) & (j2 < lane)))
        sel = _f01(elig & (cnt < NSA_TOPN))
        key_ok = _dot(sel.astype(BF16), ex_ref[...])
        parts = []
        for p in range(n_pages):
            sp = _dot(q, pages[p][0, 0, 0, kvh].astype(BF16)) * ATT_SCALE
            parts.append((sp, key_ok[:, p * LANES:(p + 1) * LANES]))
        parts.append((_dot_nt(q, kn_ref[0, kvh].astype(BF16)) * ATT_SCALE, new_ok))
        probs = softmax_parts(parts)
        o_slc = _dot(probs[-1].astype(BF16), vn_ref[0, kvh].astype(BF16))
        for p in range(n_pages):
            o_slc = o_slc + _dot_nt(probs[p].astype(BF16), pages[p][0, 0, 1, kvh].astype(BF16))
        jw = lax.broadcasted_iota(I32, (rows, wlen), 1)
        tw = lax.broadcasted_iota(I32, (rows, wlen), 0) % n_new
        dist = wlen + tw - jw
        win_okay = _f01((dist >= 0) & (dist <= NSA_WINDOW) & (past - wlen + jw >= 0))
        parts = [(_dot(q, win_ref[0, 0, 0, kvh].astype(BF16)) * ATT_SCALE, win_okay),
                 (_dot_nt(q, kwn_ref[0, kvh].astype(BF16)) * ATT_SCALE, new_ok)]
        probs = softmax_parts(parts)
        o_win = (_dot_nt(probs[0].astype(BF16), win_ref[0, 0, 1, kvh].astype(BF16))
                 + _dot(probs[1].astype(BF16), vwn_ref[0, kvh].astype(BF16)))
        gates = _sigmoid(gt_ref[0, kvh])
        o_ref[0, kvh] = gates[:, 0:1] * o_cmp + gates[:, 1:2] * o_slc + gates[:, 2:3] * o_win


def nsa_decode(grp, q_dec, new_rows, gates_dec, kc_rows, win_state, cache, layer, page_table, past):
    b_sz, n_new = grp["B"], grp["L"]
    n_pages = page_table.shape[1]
    total = past + n_new
    ns = -(-total // NSA_SLC_BLOCK)
    nc = (total - NSA_CMP_LEN) // NSA_CMP_STRIDE + 1
    ncp = kc_rows.shape[2]
    assert ns <= LANES and past % NSA_SLC_BLOCK == 0 and n_new <= NSA_SLC_BLOCK
    assert (nc - 1) * NSA_CMP_STRIDE + NSA_CMP_LEN <= past, "compressed blocks must not reach the new rows"
    overlap = jnp.asarray(_overlap_matrix(ncp, LANES), BF16)
    expand = jnp.asarray((np.arange(past)[None, :] // NSA_SLC_BLOCK == np.arange(LANES)[:, None]), BF16)
    per_seq = lambda shape: pl.BlockSpec((1,) + shape, lambda b, pt: (b,) + (0,) * len(shape))
    fixed = lambda shape: pl.BlockSpec(shape, lambda b, pt: (0,) * len(shape))
    k_new, v_new, kw_new, vw_new = new_rows
    specs = [per_seq(q_dec.shape[1:])] + [per_seq(k_new.shape[1:])] * 4
    specs += [per_seq(gates_dec.shape[1:]), per_seq(kc_rows.shape[1:]),
              pl.BlockSpec((1, 1) + win_state.shape[2:], lambda b, pt: (layer, b, 0, 0, 0, 0)),
              fixed(overlap.shape), fixed(expand.shape)]
    specs += _page_specs((1, 1, 2, NSA_KV_HEADS, HEAD_DIM, LANES), n_pages, layer, (1, 0, 0, 0))
    return pl.pallas_call(
        functools.partial(_nsa_dec_kernel, n_pages=n_pages, n_new=n_new, past=past, ns=ns, nc=nc),
        grid_spec=pltpu.PrefetchScalarGridSpec(num_scalar_prefetch=1, grid=(b_sz,), in_specs=specs,
                                               out_specs=per_seq(q_dec.shape[1:])),
        out_shape=jax.ShapeDtypeStruct(q_dec.shape, F32),
        compiler_params=_cparams(("arbitrary",)),
        name="nsa_decode",
    )(page_table, q_dec, k_new, v_new, kw_new, vw_new, gates_dec, kc_rows, win_state, overlap, expand,
      *([cache] * n_pages))


def _tile(n, prefs):
    for t in prefs:
        if n % t == 0:
            return t
    return n


def _prep_layer_weights(w_in, w_branch, w_out, cmp_w1, cmp_w2):
    wt = jnp.swapaxes(w_in, 1, 2)
    depth = wt.shape[0]
    zeros = lambda r: jnp.zeros((depth, r, wt.shape[2]), wt.dtype)
    w_att = jnp.concatenate([wt[:, 0:1024], wt[:, 2568:3872], wt[:, 2560:2568], zeros(T_ROWS - 2336)], axis=1)
    w_row = jnp.concatenate([wt[:, 3872:6944], wt[:, 1536:2560], wt[:, 1024:1536]], axis=1)
    w_dt = jnp.concatenate([wt[:, 2560:2568], zeros(LANES - SSM_HEADS)], axis=1)
    return (w_att.astype(BF16), w_row.astype(BF16), w_dt.astype(BF16), w_branch.astype(BF16), w_out.astype(BF16),
            cmp_w1.astype(BF16), jnp.swapaxes(cmp_w2, 2, 3).astype(BF16))


def _heads_first(a, b_sz, n_new, dims, pad_to=None):
    a = a.reshape((b_sz, n_new) + dims + (HEAD_DIM,))
    nd = len(dims)
    a = jnp.transpose(a, (0,) + tuple(range(2, 2 + nd)) + (1, 2 + nd))
    if pad_to is not None and pad_to > n_new:
        a = jnp.pad(a, [(0, 0)] * (1 + nd) + [(0, pad_to - n_new), (0, 0)])
    return a


def _layer_prompt(grp, x, h, mod, mod_next, lw, ffn_w, g_next, last):
    (w_att, w_row, w_dt, w_branch, w_out, cmp_w, ssm_w, g_ffn) = lw
    b_sz, seq, n = grp["B"], grp["L"], grp["N"]
    proj_t = nt_matmul(w_att, h, _tile(T_ROWS, (640, 512)), _tile(n, (1024, 512)))
    row_proj = nt_matmul(h, w_row, grp["tm"], _tile(R_COLS, (1536, 512)))
    dt_cols = nt_matmul(h, w_dt, grp["tm"], LANES)

    moba_out = moba_prompt(grp, proj_t)

    q_len = SSM_CHUNK
    n_chunks = seq // q_len
    dt_rows = proj_t[T_DT:T_DT + SSM_HEADS].reshape(SSM_HEADS, b_sz * n_chunks, q_len).transpose(1, 0, 2)
    ssm_out, new_ssm = ssm_mixer(
        b_sz, q_len, n_chunks, q_len, row_proj, R_XBC // SSM_CONV_DIM, row_proj, R_Z // SSM_INNER, dt_cols, dt_rows,
        jnp.zeros((b_sz, SUBLANES, SSM_CONV_DIM), F32), jnp.zeros((b_sz, SSM_HEADS, SSM_HEAD_DIM, SSM_STATE), F32),
        ssm_w, BF16)

    kc_rows, vc_t = compress_prompt(grp, proj_t, cmp_w)
    nsa_out = nsa_prompt(grp, proj_t, kc_rows, vc_t)

    x1, h2 = merge_branches(grp, x, (moba_out, ssm_out, nsa_out), row_proj, w_branch, w_out, g_ffn, mod)
    x2, hn = _ffn(grp, x1, h2, ffn_w, mod, g_next, mod_next, last)

    def rows_of(lo, hi, dims):
        return proj_t[lo:hi].reshape(dims + (HEAD_DIM, b_sz, seq)).transpose(3, 4, 0, 1, 2)

    new_moba = rows_of(T_MK, T_NQ, (2, MOBA_KV_HEADS))
    new_nsa = rows_of(T_NKV, T_NKV + 4 * NSA_KV_HEADS * HEAD_DIM, (4, NSA_KV_HEADS))
    win_len = grp["win_len"]
    assert seq >= win_len
    new_win = rows_of(T_NKV + 4 * NSA_KV_HEADS * HEAD_DIM, T_NGATE, (2, NSA_KV_HEADS))[:, seq - win_len:]
    new_conv = row_proj[:, R_XBC:R_XBC + SSM_CONV_DIM].reshape(b_sz, seq, SSM_CONV_DIM)[:, seq - (SSM_CONV - 1):]
    return x2, hn, (new_moba, new_nsa, new_win, new_ssm, new_conv)


def _ffn(grp, x1, h2, ffn_w, mod, g_next, mod_next, last):
    if len(ffn_w) == 4:
        w_router_t, wg, wu, wd = ffn_w
        y = moe_experts(grp, h2, moe_router(grp, h2, w_router_t), wg, wu, wd)
        return ffn_finish(grp, x1, y, mod, g_next, mod_next, last)
    wg, wu, wd = ffn_w
    combine = jnp.ones((x1.shape[0], LANES), F32)
    return ffn_block(grp, x1, h2, combine, wg, wu, wd, mod, g_next, mod_next, last)


def _layer_sample(grp, x, h, mod, mod_next, lw, ffn_w, g_next, last, caches):
    (w_att, w_row, w_dt, w_branch, w_out, cmp_w, ssm_w, g_ffn) = lw
    moba_cache, nsa_cache, win_state, win_prev, conv_state, ssm_state, layer, page_table, past = caches
    b_sz, n_new, n = grp["B"], grp["L"], grp["N"]
    att = nt_matmul(h, w_att, grp["tm"], _tile(T_ROWS, (640, 512)))
    row_proj = nt_matmul(h, w_row, grp["tm"], _tile(R_COLS, (1536, 512)))
    dt_cols = nt_matmul(h, w_dt, grp["tm"], LANES)

    g_m = MOBA_HEADS // MOBA_KV_HEADS
    q_dec = _heads_first(att[:, T_MQ:T_MK], b_sz, n_new, (MOBA_KV_HEADS, g_m)).reshape(b_sz, MOBA_KV_HEADS, g_m * n_new, HEAD_DIM)
    k_new = _heads_first(att[:, T_MK:T_MV], b_sz, n_new, (MOBA_KV_HEADS,), SUBLANES)
    v_new = _heads_first(att[:, T_MV:T_NQ], b_sz, n_new, (MOBA_KV_HEADS,), SUBLANES)
    o = moba_decode(grp, q_dec, k_new, v_new, moba_cache, layer, page_table)
    moba_out = o.reshape(b_sz, MOBA_KV_HEADS, g_m, n_new, HEAD_DIM).transpose(0, 3, 1, 2, 4).reshape(n, -1)

    q_len = SUBLANES
    pad_t = lambda a: jnp.pad(a.reshape(b_sz, n_new, -1), ((0, 0), (0, q_len - n_new), (0, 0))).reshape(b_sz * q_len, -1)
    xbc = row_proj[:, R_XBC:R_XBC + SSM_CONV_DIM]
    dt_rows = jnp.pad(att[:, T_DT:T_DT + SSM_HEADS].reshape(b_sz, n_new, SSM_HEADS).transpose(0, 2, 1),
                      ((0, 0), (0, 0), (0, q_len - n_new)))
    conv8 = jnp.pad(conv_state, ((0, 0), (SUBLANES - (SSM_CONV - 1), 0), (0, 0)))
    ssm_pad, new_ssm = ssm_mixer(b_sz, q_len, 1, n_new, pad_t(xbc), 0, pad_t(row_proj[:, R_Z:R_Z + SSM_INNER]), 0,
                                 pad_t(dt_cols), dt_rows, conv8, ssm_state, ssm_w, F32)
    ssm_out = ssm_pad.reshape(b_sz, q_len, SSM_INNER)[:, :n_new].reshape(n, SSM_INNER)

    g_n = NSA_HEADS // NSA_KV_HEADS
    nq_dec = _heads_first(att[:, T_NQ:T_NKV], b_sz, n_new, (NSA_KV_HEADS, g_n)).reshape(b_sz, NSA_KV_HEADS, g_n * n_new, HEAD_DIM)
    sets = _heads_first(att[:, T_NKV:T_NGATE], b_sz, n_new, (6, NSA_KV_HEADS), SUBLANES)
    gates = att[:, T_NGATE:T_NGATE + 3 * NSA_HEADS].reshape(b_sz, n_new, 3, NSA_KV_HEADS, g_n)
    gates = gates.transpose(0, 3, 4, 1, 2).reshape(b_sz, NSA_KV_HEADS, g_n * n_new, 3)
    gates = jnp.pad(gates, ((0, 0), (0, 0), (0, 0), (0, LANES - 3)))
    kc_rows, _ = compress_paged(grp, nsa_cache, layer, page_table, cmp_w)
    o = nsa_decode(grp, nq_dec, (sets[:, 2], sets[:, 3], sets[:, 4], sets[:, 5]), gates, kc_rows, win_state,
                   nsa_cache, layer, page_table, past)
    nsa_out = o.reshape(b_sz, NSA_KV_HEADS, g_n, n_new, HEAD_DIM).transpose(0, 3, 1, 2, 4).reshape(n, -1)

    x1, h2 = merge_branches(grp, x, (moba_out, ssm_out, nsa_out), row_proj, w_branch, w_out, g_ffn, mod)
    x2, hn = _ffn(grp, x1, h2, ffn_w, mod, g_next, mod_next, last)

    new_moba = att[:, T_MK:T_NQ].reshape(b_sz, n_new, 2, MOBA_KV_HEADS, HEAD_DIM)
    new_nsa = att[:, T_NKV:T_NKV + 4 * NSA_KV_HEADS * HEAD_DIM].reshape(b_sz, n_new, 4, NSA_KV_HEADS, HEAD_DIM)
    win_rows = att[:, T_NKV + 4 * NSA_KV_HEADS * HEAD_DIM:T_NGATE].reshape(b_sz, n_new, 2, NSA_KV_HEADS, HEAD_DIM)
    win_len = grp["win_len"]
    new_win = jnp.concatenate([win_prev, win_rows], axis=1)[:, -win_len:]
    new_conv = jnp.concatenate([conv_state, xbc.reshape(b_sz, n_new, -1)], axis=1)[:, -(SSM_CONV - 1):]
    return x2, hn, (new_moba, new_nsa, new_win, new_ssm, new_conv)


def kernel(x_prompt, x_sample, c_prompt, c_sample, cache_moba_kv, cache_nsa_kv, state_nsa_win_kv, state_ssm, state_conv, page_table, w_ada, b_ada, g_mix, w_in, conv_w, conv_b, dt_bias, a_log, d_skip, g_ssm, cmp_pe, cmp_w1, cmp_b1, cmp_w2, cmp_b2, w_branch, w_out, g_ffn, w_ffn_gate, w_ffn_up, w_ffn_down, w_router, w_exp_gate, w_exp_up, w_exp_down, g_final):
    bp, seq, d = x_prompt.shape
    bs, n_new, _ = x_sample.shape
    depth = w_in.shape[0]
    n_pages, page = page_table.shape[1], cache_moba_kv.shape[2]
    past = n_pages * page
    win_len = state_nsa_win_kv.shape[2]
    assert page == LANES and past % MOBA_BLOCK == 0 and n_new <= SUBLANES and win_len == min(NSA_WINDOW, past)

    n_p, n_s = bp * seq, bs * n_new
    grp_p = dict(B=bp, L=seq, N=n_p, tm=_tile(seq, (512, 256, 128)), per_token_mod=False, win_len=win_len)
    grp_s = dict(B=bs, L=n_new, N=n_s, tm=_tile(n_s, (512, 256, 128)), per_token_mod=True, win_len=win_len)

    rows = bp + bs
    rows_pad = -(-rows // SUBLANES) * SUBLANES
    c_all = jnp.pad(jnp.concatenate([c_prompt, c_sample], axis=0), ((0, rows_pad - rows), (0, 0)))
    mod_all = ada_modulation(c_all, w_ada, b_ada)
    mod_p = [jnp.repeat(mod_all[l, :bp], SUBLANES, axis=0) for l in range(depth)]
    mod_s = [jnp.repeat(mod_all[l, bp:rows], n_new, axis=0) for l in range(depth)]

    w_att, w_row, w_dt, wb, wo, w1, w2t = _prep_layer_weights(w_in, w_branch, w_out, cmp_w1, cmp_w2)
    w_router_t = jnp.pad(jnp.swapaxes(w_router, 1, 2), ((0, 0), (0, LANES - N_EXPERTS), (0, 0))).astype(BF16)
    dense_w = (w_ffn_gate.astype(BF16), w_ffn_up.astype(BF16), w_ffn_down.astype(BF16))
    moe_w = (w_exp_gate.astype(BF16), w_exp_up.astype(BF16), w_exp_down.astype(BF16))

    moba_cache = jnp.transpose(cache_moba_kv, (0, 1, 3, 4, 5, 2))
    nsa_cache = jnp.transpose(cache_nsa_kv, (0, 1, 3, 4, 5, 2))
    win_state = jnp.transpose(state_nsa_win_kv, (0, 1, 3, 4, 5, 2))

    xp = x_prompt.reshape(n_p, d)
    xs = x_sample.reshape(n_s, d)
    hp = norm_modulate(grp_p, xp, g_mix[0], mod_p[0])
    hs = norm_modulate(grp_s, xs, g_mix[0], mod_s[0])
    st_p, st_s = [], []
    for l in range(depth):
        last = l == depth - 1
        if l % 2:
            ffn_w = (w_router_t[l // 2],) + tuple(w[l // 2] for w in moe_w)
        else:
            ffn_w = tuple(w[l // 2][None] for w in dense_w)
        cmp_w = (cmp_pe[l], w1[l], cmp_b1[l], w2t[l], cmp_b2[l])
        ssm_w = (conv_w[l], conv_b[l], dt_bias[l], a_log[l], d_skip[l], g_ssm[l])
        lw = (w_att[l], w_row[l], w_dt[l], wb[l], wo[l], cmp_w, ssm_w, g_ffn[l])
        g_next = g_final if last else g_mix[l + 1]
        xp, hp, new_p = _layer_prompt(grp_p, xp, hp, mod_p[l], None if last else mod_p[l + 1], lw, ffn_w, g_next, last)
        caches = (moba_cache, nsa_cache, win_state, state_nsa_win_kv[l], state_conv[l], state_ssm[l], l, page_table, past)
        xs, hs, new_s = _layer_sample(grp_s, xs, hs, mod_s[l], None if last else mod_s[l + 1], lw, ffn_w, g_next, last,
                                      caches)
        st_p.append(new_p)
        st_s.append(new_s)

    stack = lambda sts, k: jnp.stack([s[k] for s in sts])
    y_prompt = hp.reshape(bp, seq, d)
    y_sample = hs.reshape(bs, n_new, d)
    return (y_prompt, y_sample, stack(st_p, 0), stack(st_s, 0), stack(st_p, 1), stack(st_s, 1), stack(st_p, 2),
            stack(st_s, 2), stack(st_p, 3), stack(st_s, 3), stack(st_p, 4), stack(st_s, 4))
```

```python
import functools

import numpy as np
import jax
import jax.numpy as jnp
from jax import lax
from jax.experimental import pallas as pl
from jax.experimental.pallas import tpu as pltpu

F32 = jnp.float32
BF16 = jnp.bfloat16
I32 = jnp.int32

D_MODEL = 1024
HEAD_DIM = 64
MOBA_HEADS, MOBA_KV_HEADS, MOBA_BLOCK, MOBA_TOPK = 8, 4, 256, 3
SSM_HEADS, SSM_HEAD_DIM, SSM_INNER, SSM_GROUPS, SSM_STATE, SSM_CONV, SSM_CHUNK = 8, 64, 512, 2, 128, 4, 128
SSM_CONV_DIM = SSM_INNER + 2 * SSM_GROUPS * SSM_STATE
NSA_HEADS, NSA_KV_HEADS = 8, 2
NSA_CMP_LEN, NSA_CMP_STRIDE, NSA_CMP_HIDDEN = 32, 16, 256
NSA_SLC_BLOCK, NSA_TOPN, NSA_WINDOW = 64, 16, 512
N_BRANCHES, BRANCH_WIDTH = 3, 512
N_EXPERTS = 8
RMS_EPS = 1e-6
NEG_BIG = -1e30
ATT_SCALE = HEAD_DIM ** -0.5

LANES = 128
SUBLANES = 8
VMEM_LIMIT = 56 * 1024 * 1024

T_MQ, T_MK, T_MV, T_NQ, T_NKV, T_NGATE, T_DT = 0, 512, 768, 1024, 1536, 2304, 2328
T_ROWS = 2560
R_MGATE, R_XBC, R_Z = 0, 3072, 4096
R_COLS = 4608


def _cparams(sem):
    return pltpu.CompilerParams(dimension_semantics=sem, vmem_limit_bytes=VMEM_LIMIT)


def _silu(x):
    return x * (1.0 / (1.0 + jnp.exp(-x)))


def _sigmoid(x):
    return 1.0 / (1.0 + jnp.exp(-x))


def _softplus(x):
    return jnp.maximum(x, 0.0) + jnp.log(1.0 + jnp.exp(-jnp.abs(x)))


def _dot(a, b):
    return jnp.dot(a, b, preferred_element_type=F32)


def _dot_nt(a, b):
    return lax.dot_general(a, b, (((1,), (1,)), ((), ())), preferred_element_type=F32)


def _dot_tn(a, b):
    return lax.dot_general(a, b, (((0,), (0,)), ((), ())), preferred_element_type=F32)


def _dot_exact(a, b):
    return jnp.dot(a, b, preferred_element_type=F32, precision=lax.Precision.HIGHEST)


def _f01(mask):
    return jnp.where(mask, 1.0, 0.0)


def _mod_rows(ref, rows):
    m = ref[...]
    return m if m.shape[0] == rows else m[0:1]


def _rms(x):
    return x * lax.rsqrt(jnp.mean(x * x, axis=-1, keepdims=True) + RMS_EPS)


def _softmax_init(width):
    return (jnp.full((1, width), NEG_BIG, F32), jnp.zeros((1, width), F32), jnp.zeros((HEAD_DIM, width), F32))


def _softmax_steps(states, scores, values):
    partial = []
    for (m, l, acc), s in zip(states, scores):
        m_new = jnp.maximum(m, jnp.max(s, axis=0, keepdims=True))
        alpha = jnp.exp(m - m_new)
        p = jnp.exp(s - m_new)
        partial.append((m_new, alpha * l + jnp.sum(p, axis=0, keepdims=True), alpha * acc, p.astype(BF16)))
    return tuple((m, l, acc + _dot(v_t, p)) for (m, l, acc, p), v_t in zip(partial, values))


def _softmax_merge(states):
    m = states[0][0]
    for st in states[1:]:
        m = jnp.maximum(m, st[0])
    l = jnp.zeros_like(states[0][1])
    acc = jnp.zeros_like(states[0][2])
    for m_s, l_s, acc_s in states:
        w = jnp.exp(m_s - m)
        l = l + w * l_s
        acc = acc + w * acc_s
    return acc / l


def _mod_spec(grp, chunk, tm):
    if grp["per_token_mod"]:
        return pl.BlockSpec((tm, D_MODEL), lambda i, *_: (i, chunk))
    tiles_per_seq = grp["L"] // tm
    return pl.BlockSpec((SUBLANES, D_MODEL), lambda i, *_: (i // tiles_per_seq, chunk))


def _ada_kernel(c_ref, w_ref, b_ref, o_ref):
    a = _silu(c_ref[...]).astype(BF16)
    o_ref[0] = _dot(a, w_ref[0].astype(BF16)) + b_ref[0]


def ada_modulation(c_all, w_ada, b_ada):
    depth, d, n6 = w_ada.shape
    rows = c_all.shape[0]
    tn = 1536
    return pl.pallas_call(
        _ada_kernel,
        grid=(depth, n6 // tn),
        in_specs=[pl.BlockSpec((rows, d), lambda l, j: (0, 0)),
                  pl.BlockSpec((1, d, tn), lambda l, j: (l, 0, j)),
                  pl.BlockSpec((1, 1, tn), lambda l, j: (l, 0, j))],
        out_specs=pl.BlockSpec((1, rows, tn), lambda l, j: (l, 0, j)),
        out_shape=jax.ShapeDtypeStruct((depth, rows, n6), F32),
        compiler_params=_cparams(("arbitrary", "arbitrary")),
        name="ada_modulation",
    )(c_all, w_ada, b_ada.reshape(depth, 1, n6))


def _norm_kernel(x_ref, g_ref, sc_ref, sh_ref, o_ref):
    x = x_ref[...]
    rows = x.shape[0]
    y = _rms(x) * g_ref[...]
    o_ref[...] = (y * (1.0 + _mod_rows(sc_ref, rows)) + _mod_rows(sh_ref, rows)).astype(o_ref.dtype)


def norm_modulate(grp, x, g, mod):
    n, d = x.shape
    tm = grp["tm"]
    return pl.pallas_call(
        _norm_kernel,
        grid=(n // tm,),
        in_specs=[pl.BlockSpec((tm, d), lambda i: (i, 0)),
                  pl.BlockSpec((1, d), lambda i: (0, 0)),
                  _mod_spec(grp, 1, tm), _mod_spec(grp, 0, tm)],
        out_specs=pl.BlockSpec((tm, d), lambda i: (i, 0)),
        out_shape=jax.ShapeDtypeStruct((n, d), BF16),
        compiler_params=_cparams(("arbitrary",)),
        name="norm_modulate",
    )(x, g.reshape(1, d), mod, mod)


def _nt_kernel(a_ref, b_ref, o_ref):
    o_ref[...] = _dot_nt(a_ref[...], b_ref[...])


def nt_matmul(a, b, tm, tn):
    m, k = a.shape
    n = b.shape[0]
    assert m % tm == 0 and n % tn == 0, (a.shape, b.shape, tm, tn)
    return pl.pallas_call(
        _nt_kernel,
        grid=(m // tm, n // tn),
        in_specs=[pl.BlockSpec((tm, k), lambda i, j: (i, 0)),
                  pl.BlockSpec((tn, k), lambda i, j: (j, 0))],
        out_specs=pl.BlockSpec((tm, tn), lambda i, j: (i, j)),
        out_shape=jax.ShapeDtypeStruct((m, n), F32),
        compiler_params=_cparams(("arbitrary", "arbitrary")),
        name="nt_matmul",
    )(a, b)


def _merge_kernel(x_ref, b0_ref, b1_ref, b2_ref, mg_ref, wb_ref, wo_ref, g_ref, gate_ref, sc_ref, sh_ref,
                  x1_ref, h2_ref):
    rows = x_ref.shape[0]
    merged = jnp.zeros((rows, D_MODEL), F32)
    for n, b_ref in enumerate((b0_ref, b1_ref, b2_ref)):
        up = _dot(b_ref[...].astype(BF16), wb_ref[n])
        merged = merged + _sigmoid(mg_ref[:, n * D_MODEL:(n + 1) * D_MODEL]) * up
    y = _dot(merged.astype(BF16), wo_ref[...])
    x1 = x_ref[...] + _mod_rows(gate_ref, rows) * y
    x1_ref[...] = x1
    h2 = _rms(x1) * g_ref[...]
    h2_ref[...] = (h2 * (1.0 + _mod_rows(sc_ref, rows)) + _mod_rows(sh_ref, rows)).astype(BF16)


def merge_branches(grp, x, branches, row_proj, wb, wo, g_ffn, mod):
    n, d = x.shape
    tm = grp["tm"]
    bw = BRANCH_WIDTH
    row = lambda i: (i, 0)
    fixed2 = lambda i: (0, 0)
    return pl.pallas_call(
        _merge_kernel,
        grid=(n // tm,),
        in_specs=[pl.BlockSpec((tm, d), row),
                  pl.BlockSpec((tm, bw), row), pl.BlockSpec((tm, bw), row), pl.BlockSpec((tm, bw), row),
                  pl.BlockSpec((tm, N_BRANCHES * d), row),
                  pl.BlockSpec((N_BRANCHES, bw, d), lambda i: (0, 0, 0)),
                  pl.BlockSpec((d, d), fixed2),
                  pl.BlockSpec((1, d), fixed2),
                  _mod_spec(grp, 2, tm), _mod_spec(grp, 4, tm), _mod_spec(grp, 3, tm)],
        out_specs=[pl.BlockSpec((tm, d), row), pl.BlockSpec((tm, d), row)],
        out_shape=[jax.ShapeDtypeStruct((n, d), F32), jax.ShapeDtypeStruct((n, d), BF16)],
        compiler_params=_cparams(("arbitrary",)),
        name="merge_branches",
    )(x, *branches, row_proj, wb, wo, g_ffn.reshape(1, d), mod, mod, mod)


def _router_kernel(h_ref, w_ref, o_ref, ot_ref, slot_ref, slott_ref):
    logits = _dot_nt(h_ref[...], w_ref[...])
    lane = lax.broadcasted_iota(I32, logits.shape, 1)
    logits = jnp.where(lane < N_EXPERTS, logits, -jnp.inf)
    m1 = jnp.max(logits, axis=-1, keepdims=True)
    i1 = jnp.min(jnp.where(logits == m1, lane, LANES), axis=-1, keepdims=True)
    rest = jnp.where(lane == i1, -jnp.inf, logits)
    m2 = jnp.max(rest, axis=-1, keepdims=True)
    i2 = jnp.min(jnp.where(rest == m2, lane, LANES), axis=-1, keepdims=True)
    e2 = jnp.exp(m2 - m1)
    den = 1.0 + e2
    combine = jnp.where(lane == i1, 1.0 / den, 0.0) + jnp.where(lane == i2, e2 / den, 0.0)
    o_ref[...] = combine
    ot_ref[...] = combine.T
    tm = combine.shape[0]
    routed = combine > 0.0
    earlier = lax.broadcasted_iota(I32, (tm, tm), 1) < lax.broadcasted_iota(I32, (tm, tm), 0)
    before = _dot(_f01(earlier).astype(BF16), _f01(routed).astype(BF16))
    slot = jnp.where(routed, before, -1.0)
    slot_ref[...] = slot
    slott_ref[...] = slot.T


def moe_router(grp, h2, w_router_t):
    n, d = h2.shape
    tm = grp["tm"]
    by_tok = pl.BlockSpec((tm, LANES), lambda i: (i, 0))
    by_exp = pl.BlockSpec((LANES, tm), lambda i: (0, i))
    tok_shape, exp_shape = jax.ShapeDtypeStruct((n, LANES), F32), jax.ShapeDtypeStruct((LANES, n), F32)
    return pl.pallas_call(
        _router_kernel,
        grid=(n // tm,),
        in_specs=[pl.BlockSpec((tm, d), lambda i: (i, 0)), pl.BlockSpec((LANES, d), lambda i: (0, 0))],
        out_specs=[by_tok, by_exp, by_tok, by_exp],
        out_shape=[tok_shape, exp_shape, tok_shape, exp_shape],
        compiler_params=_cparams(("arbitrary",)),
        name="moe_router",
    )(h2, w_router_t)


MOE_CAP = 192


def _moe_kernel(cnt_ref, h_ref, cmbt_ref, slot_ref, slott_ref, wg_ref, wu_ref, wd_ref, y_ref):
    s_idx, e, f, j = (pl.program_id(k) for k in range(4))
    tm = h_ref.shape[0]
    i = s_idx * pl.num_programs(3) + j
    cap = MOE_CAP
    w_row = cmbt_ref[pl.ds(e, 1), :]
    slot_row = slott_ref[pl.ds(e, 1), :]
    slots = slot_ref[...]
    lane = lax.broadcasted_iota(I32, slots.shape, 1)
    slot_col = jnp.sum(jnp.where(lane == e, slots, 0.0), axis=-1, keepdims=True)
    h = h_ref[...]
    slot_r = lax.broadcasted_iota(I32, (cap, tm), 0).astype(F32)
    slot_c = lax.broadcasted_iota(I32, (tm, cap), 1).astype(F32)

    rows = pl.ds(pl.multiple_of(j * tm, tm), tm)

    @pl.when((e == 0) & (f == 0))
    def _():
        y_ref[rows, :] = jnp.zeros((tm, D_MODEL), F32)

    def chunk(k, carry):
        base = (k * cap).astype(F32)
        p = _f01(slot_row - base == slot_r)
        pt = _f01(slot_col - base == slot_c).astype(BF16)
        xc = _dot(p.astype(BF16), h).astype(BF16)
        a = _silu(_dot(xc, wg_ref[0])) * _dot(xc, wu_ref[0])
        out = _dot(a.astype(BF16), wd_ref[0])
        out = out * jnp.sum(p * w_row, axis=-1, keepdims=True)
        hi = out.astype(BF16)
        lo = (out - hi.astype(F32)).astype(BF16)
        y_ref[rows, :] += _dot(pt, hi) + _dot(pt, lo)
        return carry

    lax.fori_loop(0, (cnt_ref[i, e] + cap - 1) // cap, chunk, 0)


def moe_experts(grp, h2, routing, wg, wu, wd):
    combine, combine_t, slot, slot_t = routing
    n, d = h2.shape
    n_exp, _, ff = wg.shape
    tm = grp["tm"]
    sup = _tile(n, (4 * tm, 2 * tm))
    n_j = sup // tm
    tf = _tile(ff, (1408,))
    counts = jnp.sum((combine[:, :n_exp] > 0.0).reshape(n // tm, tm, n_exp), axis=1).astype(I32)
    tok = lambda s, e, f, j, cnt: (s * n_j + j, 0)
    by_exp = pl.BlockSpec((LANES, tm), lambda s, e, f, j, cnt: (0, s * n_j + j))
    return pl.pallas_call(
        _moe_kernel,
        grid_spec=pltpu.PrefetchScalarGridSpec(
            num_scalar_prefetch=1, grid=(n // sup, n_exp, ff // tf, n_j),
            in_specs=[pl.BlockSpec((tm, d), tok), by_exp, pl.BlockSpec((tm, LANES), tok), by_exp,
                      pl.BlockSpec((1, d, tf), lambda s, e, f, j, cnt: (e, 0, f)),
                      pl.BlockSpec((1, d, tf), lambda s, e, f, j, cnt: (e, 0, f)),
                      pl.BlockSpec((1, tf, d), lambda s, e, f, j, cnt: (e, f, 0))],
            out_specs=pl.BlockSpec((sup, d), lambda s, e, f, j, cnt: (s, 0))),
        out_shape=jax.ShapeDtypeStruct((n, d), F32),
        compiler_params=_cparams(("arbitrary", "arbitrary", "arbitrary", "arbitrary")),
        name="moe_experts",
    )(counts, h2, combine_t, slot, slot_t, wg, wu, wd)


def _finish_kernel(x_ref, y_ref, gate_ref, g_ref, sc_ref, sh_ref, x2_ref, hn_ref):
    rows = x_ref.shape[0]
    x2 = x_ref[...] + _mod_rows(gate_ref, rows) * y_ref[...]
    x2_ref[...] = x2
    hn = _rms(x2) * g_ref[...]
    hn_ref[...] = (hn * (1.0 + _mod_rows(sc_ref, rows)) + _mod_rows(sh_ref, rows)).astype(hn_ref.dtype)


def _next_norm_specs(grp, tm, mod_next, last):
    if last:
        zero_mod = jnp.zeros((SUBLANES, D_MODEL), F32)
        spec = pl.BlockSpec((SUBLANES, D_MODEL), lambda i, *_: (0, 0))
        return [spec, spec], (zero_mod, zero_mod)
    return [_mod_spec(grp, 1, tm), _mod_spec(grp, 0, tm)], (mod_next, mod_next)


def ffn_finish(grp, x1, y, mod, g_next, mod_next, last):
    n, d = x1.shape
    tm = grp["tm"]
    row = lambda i: (i, 0)
    next_specs, next_args = _next_norm_specs(grp, tm, mod_next, last)
    return pl.pallas_call(
        _finish_kernel,
        grid=(n // tm,),
        in_specs=[pl.BlockSpec((tm, d), row), pl.BlockSpec((tm, d), row), _mod_spec(grp, 5, tm),
                  pl.BlockSpec((1, d), lambda i: (0, 0))] + next_specs,
        out_specs=[pl.BlockSpec((tm, d), row), pl.BlockSpec((tm, d), row)],
        out_shape=[jax.ShapeDtypeStruct((n, d), F32), jax.ShapeDtypeStruct((n, d), F32 if last else BF16)],
        compiler_params=_cparams(("arbitrary",)),
        name="ffn_finish",
    )(x1, y, mod, g_next.reshape(1, d), *next_args)


def _ffn_kernel(x_ref, h_ref, cmb_ref, wg_ref, wu_ref, wd_ref, gate_ref, g_ref, sc_ref, sh_ref,
                x2_ref, hn_ref, acc_ref):
    e, f = pl.program_id(1), pl.program_id(2)
    rows = x_ref.shape[0]

    @pl.when((e == 0) & (f == 0))
    def _():
        acc_ref[...] = jnp.zeros_like(acc_ref)

    h = h_ref[...]
    a = _silu(_dot(h, wg_ref[0])) * _dot(h, wu_ref[0])
    part = _dot(a.astype(BF16), wd_ref[0])
    cmb = cmb_ref[...]
    lane = lax.broadcasted_iota(I32, cmb.shape, 1)
    w = jnp.sum(jnp.where(lane == e, cmb, 0.0), axis=-1, keepdims=True)
    acc_ref[...] += w * part

    @pl.when((e == pl.num_programs(1) - 1) & (f == pl.num_programs(2) - 1))
    def _():
        x2 = x_ref[...] + _mod_rows(gate_ref, rows) * acc_ref[...]
        x2_ref[...] = x2
        hn = _rms(x2) * g_ref[...]
        hn_ref[...] = (hn * (1.0 + _mod_rows(sc_ref, rows)) + _mod_rows(sh_ref, rows)).astype(hn_ref.dtype)


def ffn_block(grp, x1, h2, combine, wg, wu, wd, mod, g_next, mod_next, last):
    n, d = x1.shape
    n_exp, _, ff = wg.shape
    tm = grp["tm"]
    tf = 1408 if ff % 1408 == 0 else ff
    row = lambda i, e, f: (i, 0)
    if last:
        zero_mod = jnp.zeros((SUBLANES, d), F32)
        nspec = pl.BlockSpec((SUBLANES, d), lambda i, e, f: (0, 0))
        next_specs, next_args = [nspec, nspec], (zero_mod, zero_mod)
    else:
        next_specs, next_args = [_mod_spec(grp, 1, tm), _mod_spec(grp, 0, tm)], (mod_next, mod_next)
    return pl.pallas_call(
        _ffn_kernel,
        grid=(n // tm, n_exp, ff // tf),
        in_specs=[pl.BlockSpec((tm, d), row), pl.BlockSpec((tm, d), row), pl.BlockSpec((tm, LANES), row),
                  pl.BlockSpec((1, d, tf), lambda i, e, f: (e, 0, f)),
                  pl.BlockSpec((1, d, tf), lambda i, e, f: (e, 0, f)),
                  pl.BlockSpec((1, tf, d), lambda i, e, f: (e, f, 0)),
                  _mod_spec(grp, 5, tm),
                  pl.BlockSpec((1, d), lambda i, e, f: (0, 0))] + next_specs,
        out_specs=[pl.BlockSpec((tm, d), row), pl.BlockSpec((tm, d), row)],
        out_shape=[jax.ShapeDtypeStruct((n, d), F32), jax.ShapeDtypeStruct((n, d), F32 if last else BF16)],
        scratch_shapes=[pltpu.VMEM((tm, d), F32)],
        compiler_params=_cparams(("arbitrary", "arbitrary", "arbitrary")),
        name="ffn_block",
    )(x1, h2, combine, wg, wu, wd, mod, g_next.reshape(1, d), *next_args)


def _moba_kernel(q_ref, k_ref, v_ref, o_ref, kb_ref, vb_ref, km_ref, sel_ref, *, nb):
    qi = pl.program_id(2)
    blk = MOBA_BLOCK
    nbp = sel_ref.shape[1]

    @pl.when(qi == 0)
    def _():
        km = jnp.zeros((HEAD_DIM, LANES), F32)
        lane = lax.broadcasted_iota(I32, (HEAD_DIM, LANES), 1)
        for j in range(nb):
            kt = k_ref[:, j * blk:(j + 1) * blk]
            kb_ref[j] = kt.T.astype(BF16)
            vb_ref[j] = v_ref[:, j * blk:(j + 1) * blk].astype(BF16)
            mean = jnp.sum(kt, axis=1, keepdims=True) * (1.0 / blk)
            km = jnp.where(lane == j, mean, km)
        km_ref[...] = km.astype(BF16)

    g_sz = MOBA_HEADS // MOBA_KV_HEADS
    sub = lax.broadcasted_iota(I32, (nbp, blk), 0)
    qts = []
    for g in range(g_sz):
        q = q_ref[g * HEAD_DIM:(g + 1) * HEAD_DIM, :]
        gate = _dot_tn(km_ref[...], q.astype(BF16))[:nbp]
        cnt = jnp.zeros((nbp, blk), F32)
        for j2 in range(nb):
            row = gate[j2:j2 + 1, :]
            beats = _f01((row > gate) | ((row == gate) & (j2 < sub)))
            cnt = cnt + beats * _f01(j2 < qi)
        sel_ref[g] = _f01((sub < qi) & (cnt < MOBA_TOPK))
        qts.append((q * ATT_SCALE).astype(BF16))

    init = _softmax_init(blk)
    streams = [(g, par) for g in range(g_sz) for par in range(2)]

    def past_pair(i, carry):
        scores, values = [], []
        for g, par in streams:
            j = 2 * i + par
            jc = jnp.minimum(j, qi - 1)
            chosen = sel_ref[g, pl.ds(jc, 1), :] * _f01(j < qi)
            scores.append(_dot(kb_ref[jc], qts[g]) + (chosen - 1.0) * (-NEG_BIG))
            values.append(vb_ref[jc])
        return _softmax_steps(carry, scores, values)

    states = lax.fori_loop(0, (qi + 1) // 2, past_pair, (init,) * len(streams))
    krow = lax.broadcasted_iota(I32, (blk, blk), 0)
    qcol = lax.broadcasted_iota(I32, (blk, blk), 1)
    causal_bias = jnp.where(krow <= qcol, 0.0, NEG_BIG)
    own = _softmax_steps((init,) * g_sz, [_dot(kb_ref[qi], qts[g]) + causal_bias for g in range(g_sz)],
                         [vb_ref[qi]] * g_sz)
    outs = [_softmax_merge([own[g], states[2 * g], states[2 * g + 1]]).T for g in range(g_sz)]
    o_ref[...] = jnp.concatenate(outs, axis=1).astype(o_ref.dtype)


def moba_prompt(grp, proj_t):
    b_sz, seq = grp["B"], grp["L"]
    assert seq % MOBA_BLOCK == 0
    nb = seq // MOBA_BLOCK
    nbp = -(-nb // SUBLANES) * SUBLANES
    g = MOBA_HEADS // MOBA_KV_HEADS
    return pl.pallas_call(
        functools.partial(_moba_kernel, nb=nb),
        grid=(b_sz, MOBA_KV_HEADS, nb),
        in_specs=[pl.BlockSpec((g * HEAD_DIM, MOBA_BLOCK), lambda b, h, i: (T_MQ // (g * HEAD_DIM) + h, b * nb + i)),
                  pl.BlockSpec((HEAD_DIM, seq), lambda b, h, i: (T_MK // HEAD_DIM + h, b)),
                  pl.BlockSpec((HEAD_DIM, seq), lambda b, h, i: (T_MV // HEAD_DIM + h, b))],
        out_specs=pl.BlockSpec((MOBA_BLOCK, g * HEAD_DIM), lambda b, h, i: (b * nb + i, h)),
        out_shape=jax.ShapeDtypeStruct((b_sz * seq, MOBA_HEADS * HEAD_DIM), BF16),
        scratch_shapes=[pltpu.VMEM((nb, MOBA_BLOCK, HEAD_DIM), BF16), pltpu.VMEM((nb, HEAD_DIM, MOBA_BLOCK), BF16),
                        pltpu.VMEM((HEAD_DIM, LANES), BF16), pltpu.VMEM((g, nbp, MOBA_BLOCK), F32)],
        compiler_params=_cparams(("arbitrary", "arbitrary", "arbitrary")),
        name="moba_prompt",
    )(proj_t, proj_t, proj_t)


def _compress_body(get_tile, n_tiles, pe_ref, w1_ref, b1_ref, w2t_ref, b2r_ref, b2c_ref,
                   orow_ref, ot_ref, xt_ref, xb_ref):
    ncp = xt_ref.shape[0]
    half = NSA_CMP_STRIDE
    gpt = LANES // half
    half_w = half * HEAD_DIM
    o_idx = lax.broadcasted_iota(I32, (LANES, LANES), 0)
    i_idx = lax.broadcasted_iota(I32, (LANES, LANES), 1)
    perm = _f01(i_idx == (o_idx % gpt) * half + o_idx // gpt).astype(BF16)
    for sh in range(2 * NSA_KV_HEADS):
        st = sh // NSA_KV_HEADS
        for t in range(n_tiles):
            kt = get_tile(sh, t)
            for x_ref, code in ((xt_ref, pe_ref[st, 0]), (xb_ref, pe_ref[st, 1])):
                rp = _dot_nt(perm, (kt + code).astype(BF16))
                for lp in range(half // 2):
                    pair = [rp[l * gpt:(l + 1) * gpt, :] for l in (2 * lp, 2 * lp + 1)]
                    x_ref[t * gpt:(t + 1) * gpt, lp * LANES:(lp + 1) * LANES] = jnp.concatenate(pair, axis=1)
        first = _dot(xt_ref[...].astype(BF16), w1_ref[st, :half_w, :])
        second = _dot(xb_ref[...].astype(BF16), w1_ref[st, half_w:, :])
        pre = first + pltpu.roll(second, ncp - 1, 0) + b1_ref[st]
        hid = _silu(pre).astype(BF16)
        orow_ref[0, sh] = (_dot_nt(hid, w2t_ref[st]) + b2r_ref[st]).astype(orow_ref.dtype)
        ot_ref[0, sh] = (_dot_nt(w2t_ref[st], hid) + b2c_ref[st]).astype(ot_ref.dtype)


def _compress_prompt_kernel(src_ref, *rest, n_tiles):
    get = lambda sh, t: src_ref[sh * HEAD_DIM:(sh + 1) * HEAD_DIM, t * LANES:(t + 1) * LANES]
    _compress_body(get, n_tiles, *rest)


def _compress_paged_kernel(pt_ref, *rest, n_tiles):
    pages, rest = rest[:n_tiles], rest[n_tiles:]
    get = lambda sh, t: pages[t][0, 0, sh // NSA_KV_HEADS, sh % NSA_KV_HEADS]
    _compress_body(get, n_tiles, *rest)


def _compress_call(kernel, b_sz, n_tiles, src_specs, src_args, cmp_w, prefetch=()):
    pe, w1, b1, w2t, b2 = cmp_w
    ncp = n_tiles * LANES // NSA_CMP_STRIDE
    full = lambda shape: pl.BlockSpec(shape, lambda b, *_: (0,) * len(shape))
    in_specs = src_specs + [full(pe.shape), full(w1.shape), full((2, 1, NSA_CMP_HIDDEN)), full(w2t.shape),
                            full((2, 1, HEAD_DIM)), full((2, HEAD_DIM, 1))]
    nsh = 2 * NSA_KV_HEADS
    out_specs = [pl.BlockSpec((1, nsh, ncp, HEAD_DIM), lambda b, *_: (b, 0, 0, 0)),
                 pl.BlockSpec((1, nsh, HEAD_DIM, ncp), lambda b, *_: (b, 0, 0, 0))]
    out_shape = [jax.ShapeDtypeStruct((b_sz, nsh, ncp, HEAD_DIM), BF16),
                 jax.ShapeDtypeStruct((b_sz, nsh, HEAD_DIM, ncp), BF16)]
    scratch = [pltpu.VMEM((ncp, NSA_CMP_STRIDE * HEAD_DIM), F32)] * 2
    args = src_args + [pe, w1, b1.reshape(2, 1, -1), w2t, b2.reshape(2, 1, -1), b2.reshape(2, -1, 1)]
    return pl.pallas_call(
        functools.partial(kernel, n_tiles=n_tiles),
        grid_spec=pltpu.PrefetchScalarGridSpec(num_scalar_prefetch=len(prefetch), grid=(b_sz,), in_specs=in_specs,
                                               out_specs=out_specs, scratch_shapes=scratch),
        out_shape=out_shape,
        compiler_params=_cparams(("arbitrary",)),
        name="nsa_compress",
    )(*prefetch, *args)


def compress_prompt(grp, proj_t, cmp_w):
    seq = grp["L"]
    rows = 2 * NSA_KV_HEADS * HEAD_DIM
    spec = pl.BlockSpec((rows, seq), lambda b: (T_NKV // rows, b))
    return _compress_call(_compress_prompt_kernel, grp["B"], seq // LANES, [spec], [proj_t], cmp_w)


def compress_paged(grp, cache, layer, page_table, cmp_w):
    n_pages = page_table.shape[1]
    specs = _page_specs((1, 1, 2, NSA_KV_HEADS, HEAD_DIM, LANES), n_pages, layer, (0, 0, 0, 0))
    return _compress_call(_compress_paged_kernel, grp["B"], n_pages, specs, [cache] * n_pages, cmp_w,
                          prefetch=(page_table,))


def _nsa_kernel(q_ref, ks_ref, vs_ref, kw_ref, vw_ref, gt_ref, kc_ref, vct_ref, ov_ref, o_ref,
                ksr_ref, vsb_ref, kwr_ref, vwb_ref, sel_ref, *, seq, ns, nc):
    kvh, qc = pl.program_id(1), pl.program_id(2)
    n_tiles = seq // LANES
    g_sz = NSA_HEADS // NSA_KV_HEADS
    width = g_sz * LANES
    ncp = kc_ref.shape[2]
    nsp = sel_ref.shape[0]

    @pl.when(qc == 0)
    def _():
        for t in range(n_tiles):
            sl = slice(t * LANES, (t + 1) * LANES)
            ksr_ref[t] = ks_ref[:, sl].T.astype(BF16)
            vsb_ref[t] = vs_ref[:, sl].astype(BF16)
            kwr_ref[t] = kw_ref[:, sl].T.astype(BF16)
            vwb_ref[t] = vw_ref[:, sl].astype(BF16)

    q4 = q_ref[...]
    qts = jnp.concatenate([q4[g * HEAD_DIM:(g + 1) * HEAD_DIM, :] for g in range(g_sz)], axis=1)
    qts = (qts * ATT_SCALE).astype(BF16)
    lane_w = lax.broadcasted_iota(I32, (1, width), 1)
    qpos_w = qc * LANES + (lane_w & (LANES - 1))
    qpos = qpos_w[:, :LANES]

    sub = lax.broadcasted_iota(I32, (LANES, LANES), 0)
    init = _softmax_init(width)

    def tile_scores(k_ref, t, allowed):
        bias = jnp.concatenate([(allowed - 1.0) * (-NEG_BIG)] * g_sz, axis=1)
        return _dot(k_ref[t], qts) + bias

    s = _dot(kc_ref[0, 0], qts)
    win_scores, win_values = [], []
    for k in range(NSA_WINDOW // LANES + 1):
        t = qc - k
        tc = jnp.maximum(t, 0)
        dist = qpos - (tc * LANES + sub)
        allowed = _f01((dist >= 0) & (dist <= NSA_WINDOW)) * _f01(t >= 0)
        win_scores.append(tile_scores(kwr_ref, tc, allowed))
        win_values.append(vwb_ref[tc])

    cidx = lax.broadcasted_iota(I32, (ncp, width), 0)
    valid = _f01((cidx * NSA_CMP_STRIDE + (NSA_CMP_LEN - 1) <= qpos_w) & (cidx < nc))
    s = jnp.where(valid > 0.5, s, NEG_BIG)
    p = jnp.exp(s - jnp.max(s, axis=0, keepdims=True)) * valid
    l = jnp.sum(p, axis=0, keepdims=True)
    pb = (p / jnp.where(l > 0.0, l, 1.0)).astype(BF16)
    o_cmp = _dot(vct_ref[0, 0], pb)
    ps = _dot(ov_ref[...], pb)
    o_win = _softmax_merge(list(_softmax_steps((init,) * len(win_scores), win_scores, win_values)))
    p_slc = ps[:, 0:LANES]
    for g in range(1, g_sz):
        p_slc = p_slc + ps[:, g * LANES:(g + 1) * LANES]

    j = lax.broadcasted_iota(I32, (nsp, LANES), 0)
    cur = qpos // NSA_SLC_BLOCK
    forced = (j == 0) | (j == cur) | (j == cur - 1)
    elig = j <= cur
    score = jnp.where(forced, jnp.inf, p_slc)
    score = jnp.where(elig, score, -jnp.inf)
    cnt = jnp.zeros((nsp, LANES), F32)
    for j2 in range(ns):
        row = score[j2:j2 + 1, :]
        cnt = cnt + _f01((row > score) | ((row == score) & (j2 < j)))
    sel_ref[...] = _f01(elig & (cnt < NSA_TOPN))

    def slc_allowed(t, in_range):
        blocks_per_tile = LANES // NSA_SLC_BLOCK
        r0 = sel_ref[pl.ds(blocks_per_tile * t, 1), :]
        r1 = sel_ref[pl.ds(blocks_per_tile * t + 1, 1), :]
        chosen = jnp.where(sub < NSA_SLC_BLOCK, r0, r1)
        return chosen * _f01(t * LANES + sub <= qpos) * in_range

    n_streams = 4

    def slc_group(i, carry):
        scores, values = [], []
        for k in range(n_streams):
            t = i * n_streams + k
            tc = jnp.minimum(t, qc)
            scores.append(tile_scores(ksr_ref, tc, slc_allowed(tc, _f01(t <= qc))))
            values.append(vsb_ref[tc])
        return _softmax_steps(carry, scores, values)

    slc_states = lax.fori_loop(0, (qc + n_streams) // n_streams, slc_group, (init,) * n_streams)
    o_slc = _softmax_merge(list(slc_states))

    outs = []
    for g in range(g_sz):
        head = kvh * g_sz + g
        gates = [_sigmoid(gt_ref[pl.ds(br * NSA_HEADS + head, 1), :]) for br in range(3)]
        sl = slice(g * LANES, (g + 1) * LANES)
        o = gates[0] * o_cmp[:, sl] + gates[1] * o_slc[:, sl] + gates[2] * o_win[:, sl]
        outs.append(o.T)
    o_ref[...] = jnp.concatenate(outs, axis=1).astype(o_ref.dtype)


def _overlap_matrix(ncp, nsp):
    c0 = np.arange(ncp)[:, None] * NSA_CMP_STRIDE
    s0 = np.arange(nsp)[None, :] * NSA_SLC_BLOCK
    return ((c0 < s0 + NSA_SLC_BLOCK) & (c0 + NSA_CMP_LEN > s0)).astype(np.float32)


def nsa_prompt(grp, proj_t, kc_rows, vc_t):
    b_sz, seq = grp["B"], grp["L"]
    assert seq % LANES == 0 and seq >= NSA_CMP_LEN
    nqc = seq // LANES
    g_sz = NSA_HEADS // NSA_KV_HEADS
    ns = seq // NSA_SLC_BLOCK
    nsp = -(-ns // SUBLANES) * SUBLANES
    nc = (seq - NSA_CMP_LEN) // NSA_CMP_STRIDE + 1
    ncp = kc_rows.shape[2]
    overlap_t = jnp.asarray(_overlap_matrix(ncp, nsp).T, BF16)
    kv_spec = lambda st: pl.BlockSpec((HEAD_DIM, seq), lambda b, h, i: (T_NKV // HEAD_DIM + st * NSA_KV_HEADS + h, b))
    tile_rows = pltpu.VMEM((nqc, LANES, HEAD_DIM), BF16)
    tile_cols = pltpu.VMEM((nqc, HEAD_DIM, LANES), BF16)
    return pl.pallas_call(
        functools.partial(_nsa_kernel, seq=seq, ns=ns, nc=nc),
        grid=(b_sz, NSA_KV_HEADS, nqc),
        in_specs=[pl.BlockSpec((g_sz * HEAD_DIM, LANES), lambda b, h, i: (T_NQ // (g_sz * HEAD_DIM) + h, b * nqc + i)),
                  kv_spec(2), kv_spec(3), kv_spec(4), kv_spec(5),
                  pl.BlockSpec((HEAD_DIM, LANES), lambda b, h, i: (T_NGATE // HEAD_DIM, b * nqc + i)),
                  pl.BlockSpec((1, 1, ncp, HEAD_DIM), lambda b, h, i: (b, h, 0, 0)),
                  pl.BlockSpec((1, 1, HEAD_DIM, ncp), lambda b, h, i: (b, NSA_KV_HEADS + h, 0, 0)),
                  pl.BlockSpec((nsp, ncp), lambda b, h, i: (0, 0))],
        out_specs=pl.BlockSpec((LANES, g_sz * HEAD_DIM), lambda b, h, i: (b * nqc + i, h)),
        out_shape=jax.ShapeDtypeStruct((b_sz * seq, NSA_HEADS * HEAD_DIM), BF16),
        scratch_shapes=[tile_rows, tile_cols, tile_rows, tile_cols, pltpu.VMEM((nsp, LANES), F32)],
        compiler_params=_cparams(("arbitrary", "arbitrary", "arbitrary")),
        name="nsa_prompt",
    )(proj_t, proj_t, proj_t, proj_t, proj_t, proj_t, kc_rows, vc_t, overlap_t)


def _ssm_kernel(xbc_ref, z_ref, dtc_ref, dtr_ref, cs_ref, h0_ref, cw_ref, cb_ref, dbr_ref, alr_ref, dbc_ref, alc_ref,
                gs_ref, dsk_ref, y_ref, hf_ref, tail_ref, h_ref, ybuf_ref, *, n_valid):
    c = pl.program_id(1)
    q_len = xbc_ref.shape[0]

    @pl.when(c == 0)
    def _():
        tail_ref[...] = cs_ref[0]
        h_ref[...] = h0_ref[0]

    x = xbc_ref[...]
    tail = tail_ref[...]
    row8 = lax.broadcasted_iota(I32, tail.shape, 0)
    conv = cb_ref[...] + x * cw_ref[SSM_CONV - 1:SSM_CONV, :]
    for k in range(1, SSM_CONV):
        xs = pltpu.roll(x, k, 0)
        first = jnp.where(row8 < k, pltpu.roll(tail, k, 0), xs[:SUBLANES])
        xs = first if q_len == SUBLANES else jnp.concatenate([first, xs[SUBLANES:]], axis=0)
        conv = conv + xs * cw_ref[SSM_CONV - 1 - k:SSM_CONV - k, :]
    tail_ref[...] = x[q_len - SUBLANES:, :]
    act = _silu(conv)
    gn = SSM_GROUPS * SSM_STATE
    b_all = act[:, SSM_INNER:SSM_INNER + gn].astype(BF16)
    c_all = act[:, SSM_INNER + gn:].astype(BF16)

    rows_q = lax.broadcasted_iota(I32, (q_len, LANES), 0)
    dt_c = _softplus(dtc_ref[...] + dbr_ref[...]) * _f01(rows_q < n_valid)
    a_c = dt_c * (-jnp.exp(alr_ref[...]))
    cols_q = lax.broadcasted_iota(I32, (SSM_HEADS, q_len), 1)
    dt_r = _softplus(dtr_ref[0] + dbc_ref[...]) * _f01(cols_q < n_valid)
    a_r = dt_r * (-jnp.exp(alc_ref[...]))
    ti = lax.broadcasted_iota(I32, (q_len, q_len), 0)
    si = lax.broadcasted_iota(I32, (q_len, q_len), 1)
    causal = ti >= si
    acs_c = _dot_exact(_f01(causal), a_c)
    acs_r = _dot_exact(a_r, _f01(si >= ti))

    cb = []
    for gi in range(SSM_GROUPS):
        sl = slice(gi * SSM_STATE, (gi + 1) * SSM_STATE)
        cb.append(_dot_nt(c_all[:, sl], b_all[:, sl]))
    hpg = SSM_HEADS // SSM_GROUPS
    for h in range(SSM_HEADS):
        gi = h // hpg
        sl = slice(gi * SSM_STATE, (gi + 1) * SSM_STATE)
        col = acs_c[:, h:h + 1]
        decay = jnp.where(causal, jnp.exp(col - acs_r[h:h + 1, :]), 0.0)
        xh = act[:, h * SSM_HEAD_DIM:(h + 1) * SSM_HEAD_DIM]
        xdt = xh * dt_c[:, h:h + 1]
        y = _dot((cb[gi] * decay).astype(BF16), xdt.astype(BF16))
        h_prev = h_ref[h]
        y = y + _dot_nt(c_all[:, sl], h_prev.astype(BF16)) * jnp.exp(col)
        last = acs_c[q_len - 1:q_len, h:h + 1]
        upd = _dot_tn((xdt * jnp.exp(last - col)).astype(BF16), b_all[:, sl])
        h_ref[h] = jnp.exp(last) * h_prev + upd
        ybuf_ref[:, h * SSM_HEAD_DIM:(h + 1) * SSM_HEAD_DIM] = y + xh * dsk_ref[h]

    yz = ybuf_ref[...] * _silu(z_ref[...])
    y_ref[...] = (_rms(yz) * gs_ref[...]).astype(y_ref.dtype)

    @pl.when(c == pl.num_programs(1) - 1)
    def _():
        hf_ref[0] = h_ref[...]


def ssm_mixer(b_sz, q_len, n_chunks, n_valid, xbc, xbc_col, z, z_col, dt_cols, dt_rows, conv_state8, h0, ssm_w,
              out_dtype):
    conv_w, conv_b, dt_bias, a_log, d_skip, g_ssm = ssm_w
    tok = lambda col: (lambda b, c: (b * n_chunks + c, col))
    fixed = lambda shape: pl.BlockSpec(shape, lambda b, c: (0,) * len(shape))
    pad_lanes = lambda v: jnp.pad(v.reshape(1, -1), ((0, 0), (0, LANES - v.shape[0])))
    state_shape = (SSM_HEADS, SSM_HEAD_DIM, SSM_STATE)
    return pl.pallas_call(
        functools.partial(_ssm_kernel, n_valid=n_valid),
        grid=(b_sz, n_chunks),
        in_specs=[pl.BlockSpec((q_len, SSM_CONV_DIM), tok(xbc_col)),
                  pl.BlockSpec((q_len, SSM_INNER), tok(z_col)),
                  pl.BlockSpec((q_len, LANES), tok(0)),
                  pl.BlockSpec((1, SSM_HEADS, q_len), lambda b, c: (b * n_chunks + c, 0, 0)),
                  pl.BlockSpec((1, SUBLANES, SSM_CONV_DIM), lambda b, c: (b, 0, 0)),
                  pl.BlockSpec((1,) + state_shape, lambda b, c: (b, 0, 0, 0)),
                  fixed((SSM_CONV, SSM_CONV_DIM)), fixed((1, SSM_CONV_DIM)),
                  fixed((1, LANES)), fixed((1, LANES)), fixed((SSM_HEADS, 1)), fixed((SSM_HEADS, 1)),
                  fixed((1, SSM_INNER)),
                  pl.BlockSpec(memory_space=pltpu.SMEM)],
        out_specs=[pl.BlockSpec((q_len, SSM_INNER), tok(0)),
                   pl.BlockSpec((1,) + state_shape, lambda b, c: (b, 0, 0, 0))],
        out_shape=[jax.ShapeDtypeStruct((b_sz * n_chunks * q_len, SSM_INNER), out_dtype),
                   jax.ShapeDtypeStruct((b_sz,) + state_shape, F32)],
        scratch_shapes=[pltpu.VMEM((SUBLANES, SSM_CONV_DIM), F32), pltpu.VMEM(state_shape, F32),
                        pltpu.VMEM((q_len, SSM_INNER), F32)],
        compiler_params=_cparams(("arbitrary", "arbitrary")),
        name="ssm_mixer",
    )(xbc, z, dt_cols, dt_rows, conv_state8, h0, conv_w, conv_b.reshape(1, -1), pad_lanes(dt_bias), pad_lanes(a_log),
      dt_bias.reshape(-1, 1), a_log.reshape(-1, 1), g_ssm.reshape(1, -1), d_skip)


def _moba_dec_kernel(pt_ref, q_ref, kn_ref, vn_ref, *rest, n_pages, n_new):
    pages, o_ref = rest[:n_pages], rest[n_pages]
    rows = q_ref.shape[2]
    ppb = MOBA_BLOCK // LANES
    nb_past = n_pages // ppb
    lane = lax.broadcasted_iota(I32, (rows, LANES), 1)
    lane_k = lax.broadcasted_iota(I32, (HEAD_DIM, LANES), 1)
    t_row = lax.broadcasted_iota(I32, (rows, SUBLANES), 0) % n_new
    t_col = lax.broadcasted_iota(I32, (rows, SUBLANES), 1)
    new_ok = _f01((t_col <= t_row) & (t_col < n_new))
    for kvh in range(MOBA_KV_HEADS):
        q = q_ref[0, kvh].astype(BF16)
        scores = []
        km = jnp.zeros((HEAD_DIM, LANES), F32)
        for j in range(nb_past):
            ksum = jnp.zeros((HEAD_DIM, LANES), F32)
            for pp in range(ppb):
                kt = pages[j * ppb + pp][0, 0, 0, kvh]
                scores.append(_dot(q, kt.astype(BF16)) * ATT_SCALE)
                ksum = ksum + kt
            mean = jnp.sum(ksum, axis=1, keepdims=True) * (1.0 / MOBA_BLOCK)
            km = jnp.where(lane_k == j, mean, km)
        gate = jnp.where(lane < nb_past, _dot(q, km.astype(BF16)), -jnp.inf)
        cnt = jnp.zeros((rows, LANES), F32)
        for j2 in range(nb_past):
            col = gate[:, j2:j2 + 1]
            cnt = cnt + _f01((col > gate) | ((col == gate) & (j2 < lane)))
        sel = _f01((lane < nb_past) & (cnt < MOBA_TOPK))
        s_new = jnp.where(new_ok > 0.5, _dot_nt(q, kn_ref[0, kvh].astype(BF16)) * ATT_SCALE, NEG_BIG)
        m = jnp.max(s_new, axis=1, keepdims=True)
        chosen = []
        for p in range(n_pages):
            a = sel[:, p // ppb:p // ppb + 1]
            scores[p] = jnp.where(a > 0.5, scores[p], NEG_BIG)
            chosen.append(a)
            m = jnp.maximum(m, jnp.max(scores[p], axis=1, keepdims=True))
        p_new = jnp.exp(s_new - m) * new_ok
        l = jnp.sum(p_new, axis=1, keepdims=True)
        o = _dot(p_new.astype(BF16), vn_ref[0, kvh].astype(BF16))
        for p in range(n_pages):
            pr = jnp.exp(scores[p] - m) * chosen[p]
            l = l + jnp.sum(pr, axis=1, keepdims=True)
            o = o + _dot_nt(pr.astype(BF16), pages[p][0, 0, 1, kvh].astype(BF16))
        o_ref[0, kvh] = o / l


def _page_specs(block, n_pages, layer, block_idx):
    return [pl.BlockSpec(block, functools.partial(lambda b, pt, j: (layer, pt[b, j]) + block_idx, j=j))
            for j in range(n_pages)]


def moba_decode(grp, q_dec, k_new, v_new, cache, layer, page_table):
    b_sz = grp["B"]
    n_pages = page_table.shape[1]
    per_seq = lambda shape: pl.BlockSpec((1,) + shape, lambda b, pt: (b,) + (0,) * len(shape))
    q_shape, n_shape = q_dec.shape[1:], k_new.shape[1:]
    specs = [per_seq(q_shape), per_seq(n_shape), per_seq(n_shape)]
    specs += _page_specs((1, 1, 2, MOBA_KV_HEADS, HEAD_DIM, LANES), n_pages, layer, (0, 0, 0, 0))
    return pl.pallas_call(
        functools.partial(_moba_dec_kernel, n_pages=n_pages, n_new=grp["L"]),
        grid_spec=pltpu.PrefetchScalarGridSpec(num_scalar_prefetch=1, grid=(b_sz,), in_specs=specs,
                                               out_specs=per_seq(q_shape)),
        out_shape=jax.ShapeDtypeStruct(q_dec.shape, F32),
        compiler_params=_cparams(("arbitrary",)),
        name="moba_decode",
    )(page_table, q_dec, k_new, v_new, *([cache] * n_pages))


def _nsa_dec_kernel(pt_ref, q_ref, kn_ref, vn_ref, kwn_ref, vwn_ref, gt_ref, kc_ref, win_ref, ov_ref, ex_ref, *rest,
                    n_pages, n_new, past, ns, nc):
    pages, o_ref = rest[:n_pages], rest[n_pages]
    rows = q_ref.shape[2]
    ncp = kc_ref.shape[2]
    wlen = win_ref.shape[-1]
    t_rowl = lax.broadcasted_iota(I32, (rows, LANES), 0) % n_new
    lane = lax.broadcasted_iota(I32, (rows, LANES), 1)
    t_row8 = lax.broadcasted_iota(I32, (rows, SUBLANES), 0) % n_new
    t_col8 = lax.broadcasted_iota(I32, (rows, SUBLANES), 1)
    new_ok = _f01((t_col8 <= t_row8) & (t_col8 < n_new))

    def softmax_parts(parts):
        m = None
        masked = []
        for s, a in parts:
            s = jnp.where(a > 0.5, s, NEG_BIG)
            masked.append(s)
            mx = jnp.max(s, axis=1, keepdims=True)
            m = mx if m is None else jnp.maximum(m, mx)
        ps = [jnp.exp(s - m) * a for s, (_, a) in zip(masked, parts)]
        l = ps[0].sum(axis=1, keepdims=True)
        for p in ps[1:]:
            l = l + p.sum(axis=1, keepdims=True)
        inv = 1.0 / jnp.where(l > 0.0, l, 1.0)
        return [p * inv for p in ps]

    heads = range(NSA_KV_HEADS)
    qpos = past + t_rowl
    cidx = lax.broadcasted_iota(I32, (rows, ncp), 1)
    qpos_c = past + lax.broadcasted_iota(I32, (rows, ncp), 0) % n_new
    cmp_ok = _f01((cidx * NSA_CMP_STRIDE + (NSA_CMP_LEN - 1) <= qpos_c) & (cidx < nc))
    jw = lax.broadcasted_iota(I32, (rows, wlen), 1)
    tw = lax.broadcasted_iota(I32, (rows, wlen), 0) % n_new
    dist = wlen + tw - jw
    win_okay = _f01((dist >= 0) & (dist <= NSA_WINDOW) & (past - wlen + jw >= 0))

    qs = [q_ref[0, h].astype(BF16) for h in heads]
    s_cmp = [_dot_nt(qs[h], kc_ref[0, h]) * ATT_SCALE for h in heads]
    s_pages = [[_dot(qs[h], pages[p][0, 0, 0, h].astype(BF16)) * ATT_SCALE for p in range(n_pages)] for h in heads]
    s_new = [_dot_nt(qs[h], kn_ref[0, h].astype(BF16)) * ATT_SCALE for h in heads]
    s_win = [_dot(qs[h], win_ref[0, 0, 0, h].astype(BF16)) * ATT_SCALE for h in heads]
    s_wnew = [_dot_nt(qs[h], kwn_ref[0, h].astype(BF16)) * ATT_SCALE for h in heads]

    pbs = [softmax_parts([(s_cmp[h], cmp_ok)])[0].astype(BF16) for h in heads]
    o_cmp = [_dot(pbs[h], kc_ref[0, NSA_KV_HEADS + h]) for h in heads]
    ps_all = [_dot(pbs[h], ov_ref[...]) for h in heads]
    p_win = [softmax_parts([(s_win[h], win_okay), (s_wnew[h], new_ok)]) for h in heads]
    o_win = [_dot_nt(p_win[h][0].astype(BF16), win_ref[0, 0, 1, h].astype(BF16))
             + _dot(p_win[h][1].astype(BF16), vwn_ref[0, h].astype(BF16)) for h in heads]

    sels = []
    for h in heads:
        p_slc = ps_all[h]
        for g in range(1, rows // n_new):
            p_slc = p_slc + pltpu.roll(ps_all[h], g * n_new, 0)
        cur = qpos // NSA_SLC_BLOCK
        forced = (lane == 0) | (lane == cur) | (lane == cur - 1)
        elig = lane <= cur
        score = jnp.where(forced, jnp.inf, p_slc)
        score = jnp.where(elig, score, -jnp.inf)
        cnt = jnp.zeros((rows, LANES), F32)
        for j2 in range(ns):
            col = score[:, j2:j2 + 1]
            cnt = cnt + _f01((col > score) | ((col == score) & (j2 < lane)))
        sels.append(_f01(elig & (cnt < NSA_TOPN)).astype(BF16))
    key_ok = [_dot(sels[h], ex_ref[...]) for h in heads]

    p_slc_all = []
    for h in heads:
        parts = [(s_pages[h][p], key_ok[h][:, p * LANES:(p + 1) * LANES]) for p in range(n_pages)]
        parts.append((s_new[h], new_ok))
        p_slc_all.append([p.astype(BF16) for p in softmax_parts(parts)])
    for h in heads:
        probs = p_slc_all[h]
        o_slc = _dot(probs[-1], vn_ref[0, h].astype(BF16))
        for p in range(n_pages):
            o_slc = o_slc + _dot_nt(probs[p], pages[p][0, 0, 1, h].astype(BF16))
        gates = _sigmoid(gt_ref[0, h])
        o_ref[0, h] = gates[:, 0:1] * o_cmp[h] + gates[:, 1:2] * o_slc + gates[:, 2:3] * o_win[h]


def nsa_decode(grp, q_dec, new_rows, gates_dec, kc_rows, win_state, cache, layer, page_table, past):
    b_sz, n_new = grp["B"], grp["L"]
    n_pages = page_table.shape[1]
    total = past + n_new
    ns = -(-total // NSA_SLC_BLOCK)
    nc = (total - NSA_CMP_LEN) // NSA_CMP_STRIDE + 1
    ncp = kc_rows.shape[2]
    assert ns <= LANES and past % NSA_SLC_BLOCK == 0 and n_new <= NSA_SLC_BLOCK
    assert (nc - 1) * NSA_CMP_STRIDE + NSA_CMP_LEN <= past, "compressed blocks must not reach the new rows"
    overlap = jnp.asarray(_overlap_matrix(ncp, LANES), BF16)
    expand = jnp.asarray((np.arange(past)[None, :] // NSA_SLC_BLOCK == np.arange(LANES)[:, None]), BF16)
    per_seq = lambda shape: pl.BlockSpec((1,) + shape, lambda b, pt: (b,) + (0,) * len(shape))
    fixed = lambda shape: pl.BlockSpec(shape, lambda b, pt: (0,) * len(shape))
    k_new, v_new, kw_new, vw_new = new_rows
    specs = [per_seq(q_dec.shape[1:])] + [per_seq(k_new.shape[1:])] * 4
    specs += [per_seq(gates_dec.shape[1:]), per_seq(kc_rows.shape[1:]),
              pl.BlockSpec((1, 1) + win_state.shape[2:], lambda b, pt: (layer, b, 0, 0, 0, 0)),
              fixed(overlap.shape), fixed(expand.shape)]
    specs += _page_specs((1, 1, 2, NSA_KV_HEADS, HEAD_DIM, LANES), n_pages, layer, (1, 0, 0, 0))
    return pl.pallas_call(
        functools.partial(_nsa_dec_kernel, n_pages=n_pages, n_new=n_new, past=past, ns=ns, nc=nc),
        grid_spec=pltpu.PrefetchScalarGridSpec(num_scalar_prefetch=1, grid=(b_sz,), in_specs=specs,
                                               out_specs=per_seq(q_dec.shape[1:])),
        out_shape=jax.ShapeDtypeStruct(q_dec.shape, F32),
        compiler_params=_cparams(("arbitrary",)),
        name="nsa_decode",
    )(page_table, q_dec, k_new, v_new, kw_new, vw_new, gates_dec, kc_rows, win_state, overlap, expand,
      *([cache] * n_pages))


def _tile(n, prefs):
    for t in prefs:
        if n % t == 0:
            return t
    return n


def _prep_layer_weights(w_in, w_branch, w_out, cmp_w1, cmp_w2):
    wt = jnp.swapaxes(w_in, 1, 2)
    depth = wt.shape[0]
    zeros = lambda r: jnp.zeros((depth, r, wt.shape[2]), wt.dtype)
    w_att = jnp.concatenate([wt[:, 0:1024], wt[:, 2568:3872], wt[:, 2560:2568], zeros(T_ROWS - 2336)], axis=1)
    w_row = jnp.concatenate([wt[:, 3872:6944], wt[:, 1536:2560], wt[:, 1024:1536]], axis=1)
    w_dt = jnp.concatenate([wt[:, 2560:2568], zeros(LANES - SSM_HEADS)], axis=1)
    return (w_att.astype(BF16), w_row.astype(BF16), w_dt.astype(BF16), w_branch.astype(BF16), w_out.astype(BF16),
            cmp_w1.astype(BF16), jnp.swapaxes(cmp_w2, 2, 3).astype(BF16))


def _heads_first(a, b_sz, n_new, dims, pad_to=None):
    a = a.reshape((b_sz, n_new) + dims + (HEAD_DIM,))
    nd = len(dims)
    a = jnp.transpose(a, (0,) + tuple(range(2, 2 + nd)) + (1, 2 + nd))
    if pad_to is not None and pad_to > n_new:
        a = jnp.pad(a, [(0, 0)] * (1 + nd) + [(0, pad_to - n_new), (0, 0)])
    return a


def _layer_prompt(grp, x, h, mod, mod_next, lw, ffn_w, g_next, last):
    (w_att, w_row, w_dt, w_branch, w_out, cmp_w, ssm_w, g_ffn) = lw
    b_sz, seq, n = grp["B"], grp["L"], grp["N"]
    proj_t = nt_matmul(w_att, h, _tile(T_ROWS, (640, 512)), _tile(n, (1024, 512)))
    row_proj = nt_matmul(h, w_row, grp["tm"], _tile(R_COLS, (1536, 512)))
    dt_cols = nt_matmul(h, w_dt, grp["tm"], LANES)

    moba_out = moba_prompt(grp, proj_t)

    q_len = SSM_CHUNK
    n_chunks = seq // q_len
    dt_rows = proj_t[T_DT:T_DT + SSM_HEADS].reshape(SSM_HEADS, b_sz * n_chunks, q_len).transpose(1, 0, 2)
    ssm_out, new_ssm = ssm_mixer(
        b_sz, q_len, n_chunks, q_len, row_proj, R_XBC // SSM_CONV_DIM, row_proj, R_Z // SSM_INNER, dt_cols, dt_rows,
        jnp.zeros((b_sz, SUBLANES, SSM_CONV_DIM), F32), jnp.zeros((b_sz, SSM_HEADS, SSM_HEAD_DIM, SSM_STATE), F32),
        ssm_w, BF16)

    kc_rows, vc_t = compress_prompt(grp, proj_t, cmp_w)
    nsa_out = nsa_prompt(grp, proj_t, kc_rows, vc_t)

    x1, h2 = merge_branches(grp, x, (moba_out, ssm_out, nsa_out), row_proj, w_branch, w_out, g_ffn, mod)
    x2, hn = _ffn(grp, x1, h2, ffn_w, mod, g_next, mod_next, last)

    def rows_of(lo, hi, dims):
        return proj_t[lo:hi].reshape(dims + (HEAD_DIM, b_sz, seq)).transpose(3, 4, 0, 1, 2)

    new_moba = rows_of(T_MK, T_NQ, (2, MOBA_KV_HEADS))
    new_nsa = rows_of(T_NKV, T_NKV + 4 * NSA_KV_HEADS * HEAD_DIM, (4, NSA_KV_HEADS))
    win_len = grp["win_len"]
    assert seq >= win_len
    new_win = rows_of(T_NKV + 4 * NSA_KV_HEADS * HEAD_DIM, T_NGATE, (2, NSA_KV_HEADS))[:, seq - win_len:]
    new_conv = row_proj[:, R_XBC:R_XBC + SSM_CONV_DIM].reshape(b_sz, seq, SSM_CONV_DIM)[:, seq - (SSM_CONV - 1):]
    return x2, hn, (new_moba, new_nsa, new_win, new_ssm, new_conv)


def _ffn(grp, x1, h2, ffn_w, mod, g_next, mod_next, last):
    if len(ffn_w) == 4:
        w_router_t, wg, wu, wd = ffn_w
        y = moe_experts(grp, h2, moe_router(grp, h2, w_router_t), wg, wu, wd)
        return ffn_finish(grp, x1, y, mod, g_next, mod_next, last)
    wg, wu, wd = ffn_w
    combine = jnp.ones((x1.shape[0], LANES), F32)
    return ffn_block(grp, x1, h2, combine, wg, wu, wd, mod, g_next, mod_next, last)


def _layer_sample(grp, x, h, mod, mod_next, lw, ffn_w, g_next, last, caches):
    (w_att, w_row, w_dt, w_branch, w_out, cmp_w, ssm_w, g_ffn) = lw
    moba_cache, nsa_cache, win_state, win_prev, conv_state, ssm_state, layer, page_table, past = caches
    b_sz, n_new, n = grp["B"], grp["L"], grp["N"]
    att = nt_matmul(h, w_att, grp["tm"], _tile(T_ROWS, (640, 512)))
    row_proj = nt_matmul(h, w_row, grp["tm"], _tile(R_COLS, (1536, 512)))
    dt_cols = nt_matmul(h, w_dt, grp["tm"], LANES)

    g_m = MOBA_HEADS // MOBA_KV_HEADS
    q_dec = _heads_first(att[:, T_MQ:T_MK], b_sz, n_new, (MOBA_KV_HEADS, g_m)).reshape(b_sz, MOBA_KV_HEADS, g_m * n_new, HEAD_DIM)
    k_new = _heads_first(att[:, T_MK:T_MV], b_sz, n_new, (MOBA_KV_HEADS,), SUBLANES)
    v_new = _heads_first(att[:, T_MV:T_NQ], b_sz, n_new, (MOBA_KV_HEADS,), SUBLANES)
    o = moba_decode(grp, q_dec, k_new, v_new, moba_cache, layer, page_table)
    moba_out = o.reshape(b_sz, MOBA_KV_HEADS, g_m, n_new, HEAD_DIM).transpose(0, 3, 1, 2, 4).reshape(n, -1)

    q_len = SUBLANES
    pad_t = lambda a: jnp.pad(a.reshape(b_sz, n_new, -1), ((0, 0), (0, q_len - n_new), (0, 0))).reshape(b_sz * q_len, -1)
    xbc = row_proj[:, R_XBC:R_XBC + SSM_CONV_DIM]
    dt_rows = jnp.pad(att[:, T_DT:T_DT + SSM_HEADS].reshape(b_sz, n_new, SSM_HEADS).transpose(0, 2, 1),
                      ((0, 0), (0, 0), (0, q_len - n_new)))
    conv8 = jnp.pad(conv_state, ((0, 0), (SUBLANES - (SSM_CONV - 1), 0), (0, 0)))
    ssm_pad, new_ssm = ssm_mixer(b_sz, q_len, 1, n_new, pad_t(xbc), 0, pad_t(row_proj[:, R_Z:R_Z + SSM_INNER]), 0,
                                 pad_t(dt_cols), dt_rows, conv8, ssm_state, ssm_w, F32)
    ssm_out = ssm_pad.reshape(b_sz, q_len, SSM_INNER)[:, :n_new].reshape(n, SSM_INNER)

    g_n = NSA_HEADS // NSA_KV_HEADS
    nq_dec = _heads_first(att[:, T_NQ:T_NKV], b_sz, n_new, (NSA_KV_HEADS, g_n)).reshape(b_sz, NSA_KV_HEADS, g_n * n_new, HEAD_DIM)
    sets = _heads_first(att[:, T_NKV:T_NGATE], b_sz, n_new, (6, NSA_KV_HEADS), SUBLANES)
    gates = att[:, T_NGATE:T_NGATE + 3 * NSA_HEADS].reshape(b_sz, n_new, 3, NSA_KV_HEADS, g_n)
    gates = gates.transpose(0, 3, 4, 1, 2).reshape(b_sz, NSA_KV_HEADS, g_n * n_new, 3)
    gates = jnp.pad(gates, ((0, 0), (0, 0), (0, 0), (0, LANES - 3)))
    kc_rows, _ = compress_paged(grp, nsa_cache, layer, page_table, cmp_w)
    o = nsa_decode(grp, nq_dec, (sets[:, 2], sets[:, 3], sets[:, 4], sets[:, 5]), gates, kc_rows, win_state,
                   nsa_cache, layer, page_table, past)
    nsa_out = o.reshape(b_sz, NSA_KV_HEADS, g_n, n_new, HEAD_DIM).transpose(0, 3, 1, 2, 4).reshape(n, -1)

    x1, h2 = merge_branches(grp, x, (moba_out, ssm_out, nsa_out), row_proj, w_branch, w_out, g_ffn, mod)
    x2, hn = _ffn(grp, x1, h2, ffn_w, mod, g_next, mod_next, last)

    new_moba = att[:, T_MK:T_NQ].reshape(b_sz, n_new, 2, MOBA_KV_HEADS, HEAD_DIM)
    new_nsa = att[:, T_NKV:T_NKV + 4 * NSA_KV_HEADS * HEAD_DIM].reshape(b_sz, n_new, 4, NSA_KV_HEADS, HEAD_DIM)
    win_rows = att[:, T_NKV + 4 * NSA_KV_HEADS * HEAD_DIM:T_NGATE].reshape(b_sz, n_new, 2, NSA_KV_HEADS, HEAD_DIM)
    win_len = grp["win_len"]
    new_win = jnp.concatenate([win_prev, win_rows], axis=1)[:, -win_len:]
    new_conv = jnp.concatenate([conv_state, xbc.reshape(b_sz, n_new, -1)], axis=1)[:, -(SSM_CONV - 1):]
    return x2, hn, (new_moba, new_nsa, new_win, new_ssm, new_conv)


def kernel(x_prompt, x_sample, c_prompt, c_sample, cache_moba_kv, cache_nsa_kv, state_nsa_win_kv, state_ssm, state_conv, page_table, w_ada, b_ada, g_mix, w_in, conv_w, conv_b, dt_bias, a_log, d_skip, g_ssm, cmp_pe, cmp_w1, cmp_b1, cmp_w2, cmp_b2, w_branch, w_out, g_ffn, w_ffn_gate, w_ffn_up, w_ffn_down, w_router, w_exp_gate, w_exp_up, w_exp_down, g_final):
    bp, seq, d = x_prompt.shape
    bs, n_new, _ = x_sample.shape
    depth = w_in.shape[0]
    n_pages, page = page_table.shape[1], cache_moba_kv.shape[2]
    past = n_pages * page
    win_len = state_nsa_win_kv.shape[2]
    assert page == LANES and past % MOBA_BLOCK == 0 and n_new <= SUBLANES and win_len == min(NSA_WINDOW, past)

    n_p, n_s = bp * seq, bs * n_new
    grp_p = dict(B=bp, L=seq, N=n_p, tm=_tile(seq, (512, 256, 128)), per_token_mod=False, win_len=win_len)
    grp_s = dict(B=bs, L=n_new, N=n_s, tm=_tile(n_s, (512, 256, 128)), per_token_mod=True, win_len=win_len)

    rows = bp + bs
    rows_pad = -(-rows // SUBLANES) * SUBLANES
    c_all = jnp.pad(jnp.concatenate([c_prompt, c_sample], axis=0), ((0, rows_pad - rows), (0, 0)))
    mod_all = ada_modulation(c_all, w_ada, b_ada)
    mod_p = [jnp.repeat(mod_all[l, :bp], SUBLANES, axis=0) for l in range(depth)]
    mod_s = [jnp.repeat(mod_all[l, bp:rows], n_new, axis=0) for l in range(depth)]

    w_att, w_row, w_dt, wb, wo, w1, w2t = _prep_layer_weights(w_in, w_branch, w_out, cmp_w1, cmp_w2)
    groups = LANES // NSA_CMP_STRIDE
    pe_codes = cmp_pe.reshape(depth, 2, NSA_CMP_LEN // NSA_CMP_STRIDE, NSA_CMP_STRIDE, HEAD_DIM)
    pe_codes = jnp.swapaxes(jnp.tile(pe_codes, (1, 1, 1, groups, 1)), -1, -2)
    w_router_t = jnp.pad(jnp.swapaxes(w_router, 1, 2), ((0, 0), (0, LANES - N_EXPERTS), (0, 0))).astype(BF16)
    dense_w = (w_ffn_gate.astype(BF16), w_ffn_up.astype(BF16), w_ffn_down.astype(BF16))
    moe_w = (w_exp_gate.astype(BF16), w_exp_up.astype(BF16), w_exp_down.astype(BF16))

    moba_cache = jnp.transpose(cache_moba_kv, (0, 1, 3, 4, 5, 2))
    nsa_cache = jnp.transpose(cache_nsa_kv, (0, 1, 3, 4, 5, 2))
    win_state = jnp.transpose(state_nsa_win_kv, (0, 1, 3, 4, 5, 2))

    xp = x_prompt.reshape(n_p, d)
    xs = x_sample.reshape(n_s, d)
    hp = norm_modulate(grp_p, xp, g_mix[0], mod_p[0])
    hs = norm_modulate(grp_s, xs, g_mix[0], mod_s[0])
    st_p, st_s = [], []
    for l in range(depth):
        last = l == depth - 1
        if l % 2:
            ffn_w = (w_router_t[l // 2],) + tuple(w[l // 2] for w in moe_w)
        else:
            ffn_w = tuple(w[l // 2][None] for w in dense_w)
        cmp_w = (pe_codes[l], w1[l], cmp_b1[l], w2t[l], cmp_b2[l])
        ssm_w = (conv_w[l], conv_b[l], dt_bias[l], a_log[l], d_skip[l], g_ssm[l])
        lw = (w_att[l], w_row[l], w_dt[l], wb[l], wo[l], cmp_w, ssm_w, g_ffn[l])
        g_next = g_final if last else g_mix[l + 1]
        xp, hp, new_p = _layer_prompt(grp_p, xp, hp, mod_p[l], None if last else mod_p[l + 1], lw, ffn_w, g_next, last)
        caches = (moba_cache, nsa_cache, win_state, state_nsa_win_kv[l], state_conv[l], state_ssm[l], l, page_table, past)
        xs, hs, new_s = _layer_sample(grp_s, xs, hs, mod_s[l], None if last else mod_s[l + 1], lw, ffn_w, g_next, last,
                                      caches)
        st_p.append(new_p)
        st_s.append(new_s)

    stack = lambda sts, k: jnp.stack([s[k] for s in sts])
    y_prompt = hp.reshape(bp, seq, d)
    y_sample = hs.reshape(bs, n_new, d)
    return (y_prompt, y_sample, stack(st_p, 0), stack(st_s, 0), stack(st_p, 1), stack(st_s, 1), stack(st_p, 2),
            stack(st_s, 2), stack(st_p, 3), stack(st_s, 3), stack(st_p, 4), stack(st_s, 4))
```

```python
import functools

import numpy as np
import jax
import jax.numpy as jnp
from jax import lax
from jax.experimental import pallas as pl
from jax.experimental.pallas import tpu as pltpu

F32 = jnp.float32
BF16 = jnp.bfloat16
I32 = jnp.int32

D_MODEL = 1024
HEAD_DIM = 64
MOBA_HEADS, MOBA_KV_HEADS, MOBA_BLOCK, MOBA_TOPK = 8, 4, 256, 3
SSM_HEADS, SSM_HEAD_DIM, SSM_INNER, SSM_GROUPS, SSM_STATE, SSM_CONV, SSM_CHUNK = 8, 64, 512, 2, 128, 4, 128
SSM_CONV_DIM = SSM_INNER + 2 * SSM_GROUPS * SSM_STATE
NSA_HEADS, NSA_KV_HEADS = 8, 2
NSA_CMP_LEN, NSA_CMP_STRIDE, NSA_CMP_HIDDEN = 32, 16, 256
NSA_SLC_BLOCK, NSA_TOPN, NSA_WINDOW = 64, 16, 512
N_BRANCHES, BRANCH_WIDTH = 3, 512
N_EXPERTS = 8
RMS_EPS = 1e-6
NEG_BIG = -1e30
ATT_SCALE = HEAD_DIM ** -0.5

LANES = 128
SUBLANES = 8
VMEM_LIMIT = 56 * 1024 * 1024

T_MQ, T_MK, T_MV, T_NQ, T_NKV, T_NGATE, T_DT = 0, 512, 768, 1024, 1536, 2304, 2328
T_ROWS = 2560
R_XBC, R_Z = 0, 1024
R_COLS = 1536


def _cparams(sem):
    return pltpu.CompilerParams(dimension_semantics=sem, vmem_limit_bytes=VMEM_LIMIT)


def _silu(x):
    return x * (1.0 / (1.0 + jnp.exp(-x)))


def _sigmoid(x):
    return 1.0 / (1.0 + jnp.exp(-x))


def _softplus(x):
    return jnp.maximum(x, 0.0) + jnp.log(1.0 + jnp.exp(-jnp.abs(x)))


def _dot(a, b):
    return jnp.dot(a, b, preferred_element_type=F32)


def _dot_nt(a, b):
    return lax.dot_general(a, b, (((1,), (1,)), ((), ())), preferred_element_type=F32)


def _dot_tn(a, b):
    return lax.dot_general(a, b, (((0,), (0,)), ((), ())), preferred_element_type=F32)


def _dot_exact(a, b):
    return jnp.dot(a, b, preferred_element_type=F32, precision=lax.Precision.HIGHEST)


def _f01(mask):
    return jnp.where(mask, 1.0, 0.0)


def _mod_rows(ref, rows):
    m = ref[...]
    return m if m.shape[0] == rows else m[0:1]


def _rms(x):
    return x * lax.rsqrt(jnp.mean(x * x, axis=-1, keepdims=True) + RMS_EPS)


def _softmax_init(width):
    return (jnp.full((1, width), NEG_BIG, F32), jnp.zeros((1, width), F32), jnp.zeros((HEAD_DIM, width), F32))


def _softmax_steps(states, scores, values):
    partial = []
    for (m, l, acc), s in zip(states, scores):
        m_new = jnp.maximum(m, jnp.max(s, axis=0, keepdims=True))
        alpha = jnp.exp(m - m_new)
        p = jnp.exp(s - m_new)
        partial.append((m_new, alpha * l + jnp.sum(p, axis=0, keepdims=True), alpha * acc, p.astype(BF16)))
    return tuple((m, l, acc + _dot(v_t, p)) for (m, l, acc, p), v_t in zip(partial, values))


def _softmax_merge(states):
    m = states[0][0]
    for st in states[1:]:
        m = jnp.maximum(m, st[0])
    l = jnp.zeros_like(states[0][1])
    acc = jnp.zeros_like(states[0][2])
    for m_s, l_s, acc_s in states:
        w = jnp.exp(m_s - m)
        l = l + w * l_s
        acc = acc + w * acc_s
    return acc / l


def _mod_spec(grp, chunk, tm):
    if grp["per_token_mod"]:
        return pl.BlockSpec((tm, D_MODEL), lambda i, *_: (i, chunk))
    tiles_per_seq = grp["L"] // tm
    return pl.BlockSpec((SUBLANES, D_MODEL), lambda i, *_: (i // tiles_per_seq, chunk))


def _ada_kernel(c_ref, w_ref, b_ref, o_ref):
    a = _silu(c_ref[...]).astype(BF16)
    o_ref[0] = _dot(a, w_ref[0].astype(BF16)) + b_ref[0]


def ada_modulation(c_all, w_ada, b_ada):
    depth, d, n6 = w_ada.shape
    rows = c_all.shape[0]
    tn = 1536
    return pl.pallas_call(
        _ada_kernel,
        grid=(depth, n6 // tn),
        in_specs=[pl.BlockSpec((rows, d), lambda l, j: (0, 0)),
                  pl.BlockSpec((1, d, tn), lambda l, j: (l, 0, j)),
                  pl.BlockSpec((1, 1, tn), lambda l, j: (l, 0, j))],
        out_specs=pl.BlockSpec((1, rows, tn), lambda l, j: (l, 0, j)),
        out_shape=jax.ShapeDtypeStruct((depth, rows, n6), F32),
        compiler_params=_cparams(("arbitrary", "arbitrary")),
        name="ada_modulation",
    )(c_all, w_ada, b_ada.reshape(depth, 1, n6))


def _norm_kernel(x_ref, g_ref, sc_ref, sh_ref, o_ref):
    x = x_ref[...]
    rows = x.shape[0]
    y = _rms(x) * g_ref[...]
    o_ref[...] = (y * (1.0 + _mod_rows(sc_ref, rows)) + _mod_rows(sh_ref, rows)).astype(o_ref.dtype)


def norm_modulate(grp, x, g, mod):
    n, d = x.shape
    tm = grp["tm"]
    return pl.pallas_call(
        _norm_kernel,
        grid=(n // tm,),
        in_specs=[pl.BlockSpec((tm, d), lambda i: (i, 0)),
                  pl.BlockSpec((1, d), lambda i: (0, 0)),
                  _mod_spec(grp, 1, tm), _mod_spec(grp, 0, tm)],
        out_specs=pl.BlockSpec((tm, d), lambda i: (i, 0)),
        out_shape=jax.ShapeDtypeStruct((n, d), BF16),
        compiler_params=_cparams(("arbitrary",)),
        name="norm_modulate",
    )(x, g.reshape(1, d), mod, mod)


def _nt_kernel(a_ref, b_ref, o_ref):
    o_ref[...] = _dot_nt(a_ref[...], b_ref[...])


def nt_matmul(a, b, tm, tn, b_outer=False):
    m, k = a.shape
    n = b.shape[0]
    assert m % tm == 0 and n % tn == 0, (a.shape, b.shape, tm, tn)
    grid = (n // tn, m // tm) if b_outer else (m // tm, n // tn)
    ij = (lambda j, i: (i, j)) if b_outer else (lambda i, j: (i, j))
    return pl.pallas_call(
        _nt_kernel,
        grid=grid,
        in_specs=[pl.BlockSpec((tm, k), lambda *g: (ij(*g)[0], 0)),
                  pl.BlockSpec((tn, k), lambda *g: (ij(*g)[1], 0))],
        out_specs=pl.BlockSpec((tm, tn), lambda *g: ij(*g)),
        out_shape=jax.ShapeDtypeStruct((m, n), F32),
        compiler_params=_cparams(("arbitrary", "arbitrary")),
        name="nt_matmul",
    )(a, b)


def _merge_kernel(x_ref, b0_ref, b1_ref, b2_ref, h_ref, wmg_ref, wb_ref, wo_ref, g_ref, gate_ref, sc_ref, sh_ref,
                  x1_ref, h2_ref):
    rows = x_ref.shape[0]
    merged = jnp.zeros((rows, D_MODEL), F32)
    h = h_ref[...]
    for n, b_ref in enumerate((b0_ref, b1_ref, b2_ref)):
        up = _dot(b_ref[...].astype(BF16), wb_ref[n])
        mgate = _dot_nt(h, wmg_ref[n * D_MODEL:(n + 1) * D_MODEL, :])
        merged = merged + _sigmoid(mgate) * up
    y = _dot(merged.astype(BF16), wo_ref[...])
    x1 = x_ref[...] + _mod_rows(gate_ref, rows) * y
    x1_ref[...] = x1
    h2 = _rms(x1) * g_ref[...]
    h2_ref[...] = (h2 * (1.0 + _mod_rows(sc_ref, rows)) + _mod_rows(sh_ref, rows)).astype(BF16)


def merge_branches(grp, x, branches, h, w_mgate, wb, wo, g_ffn, mod):
    n, d = x.shape
    tm = grp["tm"]
    bw = BRANCH_WIDTH
    row = lambda i: (i, 0)
    fixed2 = lambda i: (0, 0)
    once = pl.Buffered(1)
    return pl.pallas_call(
        _merge_kernel,
        grid=(n // tm,),
        in_specs=[pl.BlockSpec((tm, d), row),
                  pl.BlockSpec((tm, bw), row), pl.BlockSpec((tm, bw), row), pl.BlockSpec((tm, bw), row),
                  pl.BlockSpec((tm, d), row),
                  pl.BlockSpec((N_BRANCHES * d, d), fixed2, pipeline_mode=once),
                  pl.BlockSpec((N_BRANCHES, bw, d), lambda i: (0, 0, 0), pipeline_mode=once),
                  pl.BlockSpec((d, d), fixed2, pipeline_mode=once),
                  pl.BlockSpec((1, d), fixed2),
                  _mod_spec(grp, 2, tm), _mod_spec(grp, 4, tm), _mod_spec(grp, 3, tm)],
        out_specs=[pl.BlockSpec((tm, d), row), pl.BlockSpec((tm, d), row)],
        out_shape=[jax.ShapeDtypeStruct((n, d), F32), jax.ShapeDtypeStruct((n, d), BF16)],
        compiler_params=_cparams(("arbitrary",)),
        name="merge_branches",
    )(x, *branches, h, w_mgate, wb, wo, g_ffn.reshape(1, d), mod, mod, mod)


def _router_kernel(h_ref, w_ref, o_ref, ot_ref, slot_ref, slott_ref):
    logits = _dot_nt(h_ref[...], w_ref[...])
    lane = lax.broadcasted_iota(I32, logits.shape, 1)
    logits = jnp.where(lane < N_EXPERTS, logits, -jnp.inf)
    m1 = jnp.max(logits, axis=-1, keepdims=True)
    i1 = jnp.min(jnp.where(logits == m1, lane, LANES), axis=-1, keepdims=True)
    rest = jnp.where(lane == i1, -jnp.inf, logits)
    m2 = jnp.max(rest, axis=-1, keepdims=True)
    i2 = jnp.min(jnp.where(rest == m2, lane, LANES), axis=-1, keepdims=True)
    e2 = jnp.exp(m2 - m1)
    den = 1.0 + e2
    combine = jnp.where(lane == i1, 1.0 / den, 0.0) + jnp.where(lane == i2, e2 / den, 0.0)
    o_ref[...] = combine
    ot_ref[...] = combine.T
    tm = combine.shape[0]
    routed = combine > 0.0
    earlier = lax.broadcasted_iota(I32, (tm, tm), 1) < lax.broadcasted_iota(I32, (tm, tm), 0)
    before = _dot(_f01(earlier).astype(BF16), _f01(routed).astype(BF16))
    slot = jnp.where(routed, before, -1.0)
    slot_ref[...] = slot
    slott_ref[...] = slot.T


def moe_router(grp, h2, w_router_t):
    n, d = h2.shape
    tm = grp["tm"]
    by_tok = pl.BlockSpec((tm, LANES), lambda i: (i, 0))
    by_exp = pl.BlockSpec((LANES, tm), lambda i: (0, i))
    tok_shape, exp_shape = jax.ShapeDtypeStruct((n, LANES), F32), jax.ShapeDtypeStruct((LANES, n), F32)
    return pl.pallas_call(
        _router_kernel,
        grid=(n // tm,),
        in_specs=[pl.BlockSpec((tm, d), lambda i: (i, 0)), pl.BlockSpec((LANES, d), lambda i: (0, 0))],
        out_specs=[by_tok, by_exp, by_tok, by_exp],
        out_shape=[tok_shape, exp_shape, tok_shape, exp_shape],
        compiler_params=_cparams(("arbitrary",)),
        name="moe_router",
    )(h2, w_router_t)


MOE_CAP = 192
MOE_CAP_OVERFLOW = 64


def _moe_kernel(cnt_ref, h_ref, cmbt_ref, slot_ref, slott_ref, wg_ref, wu_ref, wd_ref, y_ref):
    s_idx, e, f, j = (pl.program_id(k) for k in range(4))
    tm = h_ref.shape[0]
    i = s_idx * pl.num_programs(3) + j
    w_row = cmbt_ref[pl.ds(e, 1), :]
    slot_row = slott_ref[pl.ds(e, 1), :]
    slots = slot_ref[...]
    lane = lax.broadcasted_iota(I32, slots.shape, 1)
    slot_col = jnp.sum(jnp.where(lane == e, slots, 0.0), axis=-1, keepdims=True)
    h = h_ref[...]
    rows = pl.ds(pl.multiple_of(j * tm, tm), tm)

    @pl.when((e == 0) & (f == 0))
    def _():
        y_ref[rows, :] = jnp.zeros((tm, D_MODEL), F32)

    def run_slots(base, cap):
        base = base.astype(F32)
        slot_r = lax.broadcasted_iota(I32, (cap, tm), 0).astype(F32)
        slot_c = lax.broadcasted_iota(I32, (tm, cap), 1).astype(F32)
        p = _f01(slot_row - base == slot_r)
        pt = _f01(slot_col - base == slot_c).astype(BF16)
        xc = _dot(p.astype(BF16), h).astype(BF16)
        a = _silu(_dot(xc, wg_ref[0])) * _dot(xc, wu_ref[0])
        out = _dot(a.astype(BF16), wd_ref[0])
        out = out * jnp.sum(p * w_row, axis=-1, keepdims=True)
        hi = out.astype(BF16)
        lo = (out - hi.astype(F32)).astype(BF16)
        y_ref[rows, :] += _dot(pt, hi) + _dot(pt, lo)

    count = cnt_ref[i, e]

    @pl.when(count > 0)
    def _():
        run_slots(jnp.int32(0), MOE_CAP)

    def overflow(k, carry):
        run_slots(MOE_CAP + k * MOE_CAP_OVERFLOW, MOE_CAP_OVERFLOW)
        return carry

    lax.fori_loop(0, (jnp.maximum(count - MOE_CAP, 0) + MOE_CAP_OVERFLOW - 1) // MOE_CAP_OVERFLOW, overflow, 0)


def moe_experts(grp, h2, routing, wg, wu, wd):
    combine, combine_t, slot, slot_t = routing
    n, d = h2.shape
    n_exp, _, ff = wg.shape
    tm = grp["tm"]
    sup = _tile(n, (4 * tm, 2 * tm))
    n_j = sup // tm
    tf = _tile(ff, (1408,))
    counts = jnp.sum((combine[:, :n_exp] > 0.0).reshape(n // tm, tm, n_exp), axis=1).astype(I32)
    tok = lambda s, e, f, j, cnt: (s * n_j + j, 0)
    by_exp = pl.BlockSpec((LANES, tm), lambda s, e, f, j, cnt: (0, s * n_j + j))
    return pl.pallas_call(
        _moe_kernel,
        grid_spec=pltpu.PrefetchScalarGridSpec(
            num_scalar_prefetch=1, grid=(n // sup, n_exp, ff // tf, n_j),
            in_specs=[pl.BlockSpec((tm, d), tok), by_exp, pl.BlockSpec((tm, LANES), tok), by_exp,
                      pl.BlockSpec((1, d, tf), lambda s, e, f, j, cnt: (e, 0, f)),
                      pl.BlockSpec((1, d, tf), lambda s, e, f, j, cnt: (e, 0, f)),
                      pl.BlockSpec((1, tf, d), lambda s, e, f, j, cnt: (e, f, 0))],
            out_specs=pl.BlockSpec((sup, d), lambda s, e, f, j, cnt: (s, 0))),
        out_shape=jax.ShapeDtypeStruct((n, d), F32),
        compiler_params=_cparams(("arbitrary", "arbitrary", "arbitrary", "arbitrary")),
        name="moe_experts",
    )(counts, h2, combine_t, slot, slot_t, wg, wu, wd)


def _finish_kernel(x_ref, y_ref, gate_ref, g_ref, sc_ref, sh_ref, x2_ref, hn_ref):
    rows = x_ref.shape[0]
    x2 = x_ref[...] + _mod_rows(gate_ref, rows) * y_ref[...]
    x2_ref[...] = x2
    hn = _rms(x2) * g_ref[...]
    hn_ref[...] = (hn * (1.0 + _mod_rows(sc_ref, rows)) + _mod_rows(sh_ref, rows)).astype(hn_ref.dtype)


def _next_norm_specs(grp, tm, mod_next, last):
    if last:
        zero_mod = jnp.zeros((SUBLANES, D_MODEL), F32)
        spec = pl.BlockSpec((SUBLANES, D_MODEL), lambda i, *_: (0, 0))
        return [spec, spec], (zero_mod, zero_mod)
    return [_mod_spec(grp, 1, tm), _mod_spec(grp, 0, tm)], (mod_next, mod_next)


def ffn_finish(grp, x1, y, mod, g_next, mod_next, last):
    n, d = x1.shape
    tm = grp["tm"]
    row = lambda i: (i, 0)
    next_specs, next_args = _next_norm_specs(grp, tm, mod_next, last)
    return pl.pallas_call(
        _finish_kernel,
        grid=(n // tm,),
        in_specs=[pl.BlockSpec((tm, d), row), pl.BlockSpec((tm, d), row), _mod_spec(grp, 5, tm),
                  pl.BlockSpec((1, d), lambda i: (0, 0))] + next_specs,
        out_specs=[pl.BlockSpec((tm, d), row), pl.BlockSpec((tm, d), row)],
        out_shape=[jax.ShapeDtypeStruct((n, d), F32), jax.ShapeDtypeStruct((n, d), F32 if last else BF16)],
        compiler_params=_cparams(("arbitrary",)),
        name="ffn_finish",
    )(x1, y, mod, g_next.reshape(1, d), *next_args)


def _ffn_kernel(x_ref, h_ref, cmb_ref, wg_ref, wu_ref, wd_ref, gate_ref, g_ref, sc_ref, sh_ref,
                x2_ref, hn_ref, acc_ref):
    e, f = pl.program_id(1), pl.program_id(2)
    rows = x_ref.shape[0]

    @pl.when((e == 0) & (f == 0))
    def _():
        acc_ref[...] = jnp.zeros_like(acc_ref)

    h = h_ref[...]
    a = _silu(_dot(h, wg_ref[0])) * _dot(h, wu_ref[0])
    part = _dot(a.astype(BF16), wd_ref[0])
    cmb = cmb_ref[...]
    lane = lax.broadcasted_iota(I32, cmb.shape, 1)
    w = jnp.sum(jnp.where(lane == e, cmb, 0.0), axis=-1, keepdims=True)
    acc_ref[...] += w * part

    @pl.when((e == pl.num_programs(1) - 1) & (f == pl.num_programs(2) - 1))
    def _():
        x2 = x_ref[...] + _mod_rows(gate_ref, rows) * acc_ref[...]
        x2_ref[...] = x2
        hn = _rms(x2) * g_ref[...]
        hn_ref[...] = (hn * (1.0 + _mod_rows(sc_ref, rows)) + _mod_rows(sh_ref, rows)).astype(hn_ref.dtype)


def ffn_block(grp, x1, h2, combine, wg, wu, wd, mod, g_next, mod_next, last):
    n, d = x1.shape
    n_exp, _, ff = wg.shape
    tm = grp["tm"]
    tf = 1408 if ff % 1408 == 0 else ff
    row = lambda i, e, f: (i, 0)
    if last:
        zero_mod = jnp.zeros((SUBLANES, d), F32)
        nspec = pl.BlockSpec((SUBLANES, d), lambda i, e, f: (0, 0))
        next_specs, next_args = [nspec, nspec], (zero_mod, zero_mod)
    else:
        next_specs, next_args = [_mod_spec(grp, 1, tm), _mod_spec(grp, 0, tm)], (mod_next, mod_next)
    return pl.pallas_call(
        _ffn_kernel,
        grid=(n // tm, n_exp, ff // tf),
        in_specs=[pl.BlockSpec((tm, d), row), pl.BlockSpec((tm, d), row), pl.BlockSpec((tm, LANES), row),
                  pl.BlockSpec((1, d, tf), lambda i, e, f: (e, 0, f)),
                  pl.BlockSpec((1, d, tf), lambda i, e, f: (e, 0, f)),
                  pl.BlockSpec((1, tf, d), lambda i, e, f: (e, f, 0)),
                  _mod_spec(grp, 5, tm),
                  pl.BlockSpec((1, d), lambda i, e, f: (0, 0))] + next_specs,
        out_specs=[pl.BlockSpec((tm, d), row), pl.BlockSpec((tm, d), row)],
        out_shape=[jax.ShapeDtypeStruct((n, d), F32), jax.ShapeDtypeStruct((n, d), F32 if last else BF16)],
        scratch_shapes=[pltpu.VMEM((tm, d), F32)],
        compiler_params=_cparams(("arbitrary", "arbitrary", "arbitrary")),
        name="ffn_block",
    )(x1, h2, combine, wg, wu, wd, mod, g_next.reshape(1, d), *next_args)


def _moba_kernel(q_ref, k_ref, v_ref, o_ref, kb_ref, vb_ref, km_ref, sel_ref, *, nb):
    qi = pl.program_id(2)
    blk = MOBA_BLOCK
    nbp = sel_ref.shape[1]

    @pl.when(qi == 0)
    def _():
        km = jnp.zeros((HEAD_DIM, LANES), F32)
        lane = lax.broadcasted_iota(I32, (HEAD_DIM, LANES), 1)
        for j in range(nb):
            kt = k_ref[:, j * blk:(j + 1) * blk]
            kb_ref[j] = kt.T.astype(BF16)
            vb_ref[j] = v_ref[:, j * blk:(j + 1) * blk].astype(BF16)
            mean = jnp.sum(kt, axis=1, keepdims=True) * (1.0 / blk)
            km = jnp.where(lane == j, mean, km)
        km_ref[...] = km.astype(BF16)

    g_sz = MOBA_HEADS // MOBA_KV_HEADS
    sub = lax.broadcasted_iota(I32, (nbp, blk), 0)
    qts = []
    for g in range(g_sz):
        q = q_ref[g * HEAD_DIM:(g + 1) * HEAD_DIM, :]
        gate = _dot_tn(km_ref[...], q.astype(BF16))[:nbp]
        cnt = jnp.zeros((nbp, blk), F32)
        for j2 in range(nb):
            row = gate[j2:j2 + 1, :]
            beats = _f01((row > gate) | ((row == gate) & (j2 < sub)))
            cnt = cnt + beats * _f01(j2 < qi)
        sel_ref[g] = _f01((sub < qi) & (cnt < MOBA_TOPK))
        qts.append((q * ATT_SCALE).astype(BF16))

    init = _softmax_init(blk)
    streams = [(g, par) for g in range(g_sz) for par in range(2)]

    def past_pair(i, carry):
        scores, values = [], []
        for g, par in streams:
            j = 2 * i + par
            jc = jnp.minimum(j, qi - 1)
            chosen = sel_ref[g, pl.ds(jc, 1), :] * _f01(j < qi)
            scores.append(_dot(kb_ref[jc], qts[g]) + (chosen - 1.0) * (-NEG_BIG))
            values.append(vb_ref[jc])
        return _softmax_steps(carry, scores, values)

    states = lax.fori_loop(0, (qi + 1) // 2, past_pair, (init,) * len(streams))
    krow = lax.broadcasted_iota(I32, (blk, blk), 0)
    qcol = lax.broadcasted_iota(I32, (blk, blk), 1)
    causal_bias = jnp.where(krow <= qcol, 0.0, NEG_BIG)
    own = _softmax_steps((init,) * g_sz, [_dot(kb_ref[qi], qts[g]) + causal_bias for g in range(g_sz)],
                         [vb_ref[qi]] * g_sz)
    outs = [_softmax_merge([own[g], states[2 * g], states[2 * g + 1]]).T for g in range(g_sz)]
    o_ref[...] = jnp.concatenate(outs, axis=1).astype(o_ref.dtype)


def moba_prompt(grp, proj_t):
    b_sz, seq = grp["B"], grp["L"]
    assert seq % MOBA_BLOCK == 0
    nb = seq // MOBA_BLOCK
    nbp = -(-nb // SUBLANES) * SUBLANES
    g = MOBA_HEADS // MOBA_KV_HEADS
    return pl.pallas_call(
        functools.partial(_moba_kernel, nb=nb),
        grid=(b_sz, MOBA_KV_HEADS, nb),
        in_specs=[pl.BlockSpec((g * HEAD_DIM, MOBA_BLOCK), lambda b, h, i: (T_MQ // (g * HEAD_DIM) + h, b * nb + i)),
                  pl.BlockSpec((HEAD_DIM, seq), lambda b, h, i: (T_MK // HEAD_DIM + h, b)),
                  pl.BlockSpec((HEAD_DIM, seq), lambda b, h, i: (T_MV // HEAD_DIM + h, b))],
        out_specs=pl.BlockSpec((MOBA_BLOCK, g * HEAD_DIM), lambda b, h, i: (b * nb + i, h)),
        out_shape=jax.ShapeDtypeStruct((b_sz * seq, MOBA_HEADS * HEAD_DIM), BF16),
        scratch_shapes=[pltpu.VMEM((nb, MOBA_BLOCK, HEAD_DIM), BF16), pltpu.VMEM((nb, HEAD_DIM, MOBA_BLOCK), BF16),
                        pltpu.VMEM((HEAD_DIM, LANES), BF16), pltpu.VMEM((g, nbp, MOBA_BLOCK), F32)],
        compiler_params=_cparams(("arbitrary", "arbitrary", "arbitrary")),
        name="moba_prompt",
    )(proj_t, proj_t, proj_t)


def _compress_body(get_tile, n_tiles, pe_ref, w1_ref, b1_ref, w2t_ref, b2r_ref, b2c_ref,
                   orow_ref, ot_ref, xt_ref, xb_ref):
    ncp = xt_ref.shape[0]
    half = NSA_CMP_STRIDE
    gpt = LANES // half
    half_w = half * HEAD_DIM
    o_idx = lax.broadcasted_iota(I32, (LANES, LANES), 0)
    i_idx = lax.broadcasted_iota(I32, (LANES, LANES), 1)
    perm = _f01(i_idx == (o_idx % gpt) * half + o_idx // gpt).astype(BF16)
    for sh in range(2 * NSA_KV_HEADS):
        st = sh // NSA_KV_HEADS
        for t in range(n_tiles):
            kt = get_tile(sh, t)
            for x_ref, code in ((xt_ref, pe_ref[st, 0]), (xb_ref, pe_ref[st, 1])):
                rp = _dot_nt(perm, (kt + code).astype(BF16))
                for lp in range(half // 2):
                    pair = [rp[l * gpt:(l + 1) * gpt, :] for l in (2 * lp, 2 * lp + 1)]
                    x_ref[t * gpt:(t + 1) * gpt, lp * LANES:(lp + 1) * LANES] = jnp.concatenate(pair, axis=1)
        first = _dot(xt_ref[...].astype(BF16), w1_ref[st, :half_w, :])
        second = _dot(xb_ref[...].astype(BF16), w1_ref[st, half_w:, :])
        pre = first + pltpu.roll(second, ncp - 1, 0) + b1_ref[st]
        hid = _silu(pre).astype(BF16)
        orow_ref[0, sh] = (_dot_nt(hid, w2t_ref[st]) + b2r_ref[st]).astype(orow_ref.dtype)
        ot_ref[0, sh] = (_dot_nt(w2t_ref[st], hid) + b2c_ref[st]).astype(ot_ref.dtype)


def _compress_prompt_kernel(src_ref, *rest, n_tiles):
    get = lambda sh, t: src_ref[sh * HEAD_DIM:(sh + 1) * HEAD_DIM, t * LANES:(t + 1) * LANES]
    _compress_body(get, n_tiles, *rest)


def _compress_paged_kernel(pt_ref, *rest, n_tiles):
    pages, rest = rest[:n_tiles], rest[n_tiles:]
    get = lambda sh, t: pages[t][0, 0, sh // NSA_KV_HEADS, sh % NSA_KV_HEADS]
    _compress_body(get, n_tiles, *rest)


def _compress_call(kernel, b_sz, n_tiles, src_specs, src_args, cmp_w, prefetch=()):
    pe, w1, b1, w2t, b2 = cmp_w
    ncp = n_tiles * LANES // NSA_CMP_STRIDE
    full = lambda shape: pl.BlockSpec(shape, lambda b, *_: (0,) * len(shape))
    in_specs = src_specs + [full(pe.shape), full(w1.shape), full((2, 1, NSA_CMP_HIDDEN)), full(w2t.shape),
                            full((2, 1, HEAD_DIM)), full((2, HEAD_DIM, 1))]
    nsh = 2 * NSA_KV_HEADS
    out_specs = [pl.BlockSpec((1, nsh, ncp, HEAD_DIM), lambda b, *_: (b, 0, 0, 0)),
                 pl.BlockSpec((1, nsh, HEAD_DIM, ncp), lambda b, *_: (b, 0, 0, 0))]
    out_shape = [jax.ShapeDtypeStruct((b_sz, nsh, ncp, HEAD_DIM), BF16),
                 jax.ShapeDtypeStruct((b_sz, nsh, HEAD_DIM, ncp), BF16)]
    scratch = [pltpu.VMEM((ncp, NSA_CMP_STRIDE * HEAD_DIM), F32)] * 2
    args = src_args + [pe, w1, b1.reshape(2, 1, -1), w2t, b2.reshape(2, 1, -1), b2.reshape(2, -1, 1)]
    return pl.pallas_call(
        functools.partial(kernel, n_tiles=n_tiles),
        grid_spec=pltpu.PrefetchScalarGridSpec(num_scalar_prefetch=len(prefetch), grid=(b_sz,), in_specs=in_specs,
                                               out_specs=out_specs, scratch_shapes=scratch),
        out_shape=out_shape,
        compiler_params=_cparams(("arbitrary",)),
        name="nsa_compress",
    )(*prefetch, *args)


def compress_prompt(grp, proj_t, cmp_w):
    seq = grp["L"]
    rows = 2 * NSA_KV_HEADS * HEAD_DIM
    spec = pl.BlockSpec((rows, seq), lambda b: (T_NKV // rows, b))
    return _compress_call(_compress_prompt_kernel, grp["B"], seq // LANES, [spec], [proj_t], cmp_w)


def compress_paged(grp, cache, layer, page_table, cmp_w):
    n_pages = page_table.shape[1]
    specs = _page_specs((1, 1, 2, NSA_KV_HEADS, HEAD_DIM, LANES), n_pages, layer, (0, 0, 0, 0))
    return _compress_call(_compress_paged_kernel, grp["B"], n_pages, specs, [cache] * n_pages, cmp_w,
                          prefetch=(page_table,))


def _nsa_kernel(q_ref, ks_ref, vs_ref, kw_ref, vw_ref, gt_ref, kc_ref, vct_ref, ov_ref, o_ref,
                ksr_ref, vsb_ref, kwr_ref, vwb_ref, sel_ref, *, seq, ns, nc):
    kvh, qc = pl.program_id(1), pl.program_id(2)
    n_tiles = seq // LANES
    g_sz = NSA_HEADS // NSA_KV_HEADS
    width = g_sz * LANES
    ncp = kc_ref.shape[2]
    nsp = sel_ref.shape[0]

    @pl.when(qc == 0)
    def _():
        for t in range(n_tiles):
            sl = slice(t * LANES, (t + 1) * LANES)
            ksr_ref[t] = ks_ref[:, sl].T.astype(BF16)
            vsb_ref[t] = vs_ref[:, sl].astype(BF16)
            kwr_ref[t] = kw_ref[:, sl].T.astype(BF16)
            vwb_ref[t] = vw_ref[:, sl].astype(BF16)

    q4 = q_ref[...]
    qts = jnp.concatenate([q4[g * HEAD_DIM:(g + 1) * HEAD_DIM, :] for g in range(g_sz)], axis=1)
    qts = (qts * ATT_SCALE).astype(BF16)
    lane_w = lax.broadcasted_iota(I32, (1, width), 1)
    qpos_w = qc * LANES + (lane_w & (LANES - 1))
    qpos = qpos_w[:, :LANES]

    sub = lax.broadcasted_iota(I32, (LANES, LANES), 0)
    init = _softmax_init(width)

    def tile_scores(k_ref, t, allowed):
        bias = jnp.concatenate([(allowed - 1.0) * (-NEG_BIG)] * g_sz, axis=1)
        return _dot(k_ref[t], qts) + bias

    s = _dot(kc_ref[0, 0], qts)
    win_scores, win_values = [], []
    for k in range(NSA_WINDOW // LANES + 1):
        t = qc - k
        tc = jnp.maximum(t, 0)
        dist = qpos - (tc * LANES + sub)
        allowed = _f01((dist >= 0) & (dist <= NSA_WINDOW)) * _f01(t >= 0)
        win_scores.append(tile_scores(kwr_ref, tc, allowed))
        win_values.append(vwb_ref[tc])

    cidx = lax.broadcasted_iota(I32, (ncp, width), 0)
    valid = _f01((cidx * NSA_CMP_STRIDE + (NSA_CMP_LEN - 1) <= qpos_w) & (cidx < nc))
    s = jnp.where(valid > 0.5, s, NEG_BIG)
    p = jnp.exp(s - jnp.max(s, axis=0, keepdims=True)) * valid
    l = jnp.sum(p, axis=0, keepdims=True)
    pb = (p / jnp.where(l > 0.0, l, 1.0)).astype(BF16)
    o_cmp = _dot(vct_ref[0, 0], pb)
    ps = _dot(ov_ref[...], pb)
    o_win = _softmax_merge(list(_softmax_steps((init,) * len(win_scores), win_scores, win_values)))
    p_slc = ps[:, 0:LANES]
    for g in range(1, g_sz):
        p_slc = p_slc + ps[:, g * LANES:(g + 1) * LANES]

    j = lax.broadcasted_iota(I32, (nsp, LANES), 0)
    cur = qpos // NSA_SLC_BLOCK
    forced = (j == 0) | (j == cur) | (j == cur - 1)
    elig = j <= cur
    score = jnp.where(forced, jnp.inf, p_slc)
    score = jnp.where(elig, score, -jnp.inf)
    cnt = jnp.zeros((nsp, LANES), F32)
    for j2 in range(ns):
        row = score[j2:j2 + 1, :]
        cnt = cnt + _f01((row > score) | ((row == score) & (j2 < j)))
    sel_ref[...] = _f01(elig & (cnt < NSA_TOPN))

    def slc_allowed(t, in_range):
        blocks_per_tile = LANES // NSA_SLC_BLOCK
        r0 = sel_ref[pl.ds(blocks_per_tile * t, 1), :]
        r1 = sel_ref[pl.ds(blocks_per_tile * t + 1, 1), :]
        chosen = jnp.where(sub < NSA_SLC_BLOCK, r0, r1)
        return chosen * _f01(t * LANES + sub <= qpos) * in_range

    n_streams = 4

    def slc_group(i, carry):
        scores, values = [], []
        for k in range(n_streams):
            t = i * n_streams + k
            tc = jnp.minimum(t, qc)
            scores.append(tile_scores(ksr_ref, tc, slc_allowed(tc, _f01(t <= qc))))
            values.append(vsb_ref[tc])
        return _softmax_steps(carry, scores, values)

    slc_states = lax.fori_loop(0, (qc + n_streams) // n_streams, slc_group, (init,) * n_streams)
    o_slc = _softmax_merge(list(slc_states))

    outs = []
    for g in range(g_sz):
        head = kvh * g_sz + g
        gates = [_sigmoid(gt_ref[pl.ds(br * NSA_HEADS + head, 1), :]) for br in range(3)]
        sl = slice(g * LANES, (g + 1) * LANES)
        o = gates[0] * o_cmp[:, sl] + gates[1] * o_slc[:, sl] + gates[2] * o_win[:, sl]
        outs.append(o.T)
    o_ref[...] = jnp.concatenate(outs, axis=1).astype(o_ref.dtype)


def _overlap_matrix(ncp, nsp):
    c0 = np.arange(ncp)[:, None] * NSA_CMP_STRIDE
    s0 = np.arange(nsp)[None, :] * NSA_SLC_BLOCK
    return ((c0 < s0 + NSA_SLC_BLOCK) & (c0 + NSA_CMP_LEN > s0)).astype(np.float32)


def nsa_prompt(grp, proj_t, kc_rows, vc_t):
    b_sz, seq = grp["B"], grp["L"]
    assert seq % LANES == 0 and seq >= NSA_CMP_LEN
    nqc = seq // LANES
    g_sz = NSA_HEADS // NSA_KV_HEADS
    ns = seq // NSA_SLC_BLOCK
    nsp = -(-ns // SUBLANES) * SUBLANES
    nc = (seq - NSA_CMP_LEN) // NSA_CMP_STRIDE + 1
    ncp = kc_rows.shape[2]
    overlap_t = jnp.asarray(_overlap_matrix(ncp, nsp).T, BF16)
    kv_spec = lambda st: pl.BlockSpec((HEAD_DIM, seq), lambda b, h, i: (T_NKV // HEAD_DIM + st * NSA_KV_HEADS + h, b))
    tile_rows = pltpu.VMEM((nqc, LANES, HEAD_DIM), BF16)
    tile_cols = pltpu.VMEM((nqc, HEAD_DIM, LANES), BF16)
    return pl.pallas_call(
        functools.partial(_nsa_kernel, seq=seq, ns=ns, nc=nc),
        grid=(b_sz, NSA_KV_HEADS, nqc),
        in_specs=[pl.BlockSpec((g_sz * HEAD_DIM, LANES), lambda b, h, i: (T_NQ // (g_sz * HEAD_DIM) + h, b * nqc + i)),
                  kv_spec(2), kv_spec(3), kv_spec(4), kv_spec(5),
                  pl.BlockSpec((HEAD_DIM, LANES), lambda b, h, i: (T_NGATE // HEAD_DIM, b * nqc + i)),
                  pl.BlockSpec((1, 1, ncp, HEAD_DIM), lambda b, h, i: (b, h, 0, 0)),
                  pl.BlockSpec((1, 1, HEAD_DIM, ncp), lambda b, h, i: (b, NSA_KV_HEADS + h, 0, 0)),
                  pl.BlockSpec((nsp, ncp), lambda b, h, i: (0, 0))],
        out_specs=pl.BlockSpec((LANES, g_sz * HEAD_DIM), lambda b, h, i: (b * nqc + i, h)),
        out_shape=jax.ShapeDtypeStruct((b_sz * seq, NSA_HEADS * HEAD_DIM), BF16),
        scratch_shapes=[tile_rows, tile_cols, tile_rows, tile_cols, pltpu.VMEM((nsp, LANES), F32)],
        compiler_params=_cparams(("arbitrary", "arbitrary", "arbitrary")),
        name="nsa_prompt",
    )(proj_t, proj_t, proj_t, proj_t, proj_t, proj_t, kc_rows, vc_t, overlap_t)


def _ssm_kernel(xbc_ref, z_ref, dtc_ref, dtr_ref, cs_ref, h0_ref, cw_ref, cb_ref, dbr_ref, alr_ref, dbc_ref, alc_ref,
                gs_ref, dsk_ref, y_ref, hf_ref, tail_ref, h_ref, ybuf_ref, *, n_valid):
    c = pl.program_id(1)
    q_len = xbc_ref.shape[0]

    @pl.when(c == 0)
    def _():
        tail_ref[...] = cs_ref[0]
        h_ref[...] = h0_ref[0]

    x = xbc_ref[...]
    tail = tail_ref[...]
    row8 = lax.broadcasted_iota(I32, tail.shape, 0)
    conv = cb_ref[...] + x * cw_ref[SSM_CONV - 1:SSM_CONV, :]
    for k in range(1, SSM_CONV):
        xs = pltpu.roll(x, k, 0)
        first = jnp.where(row8 < k, pltpu.roll(tail, k, 0), xs[:SUBLANES])
        xs = first if q_len == SUBLANES else jnp.concatenate([first, xs[SUBLANES:]], axis=0)
        conv = conv + xs * cw_ref[SSM_CONV - 1 - k:SSM_CONV - k, :]
    tail_ref[...] = x[q_len - SUBLANES:, :]
    act = _silu(conv)
    gn = SSM_GROUPS * SSM_STATE
    b_all = act[:, SSM_INNER:SSM_INNER + gn].astype(BF16)
    c_all = act[:, SSM_INNER + gn:].astype(BF16)

    rows_q = lax.broadcasted_iota(I32, (q_len, LANES), 0)
    dt_c = _softplus(dtc_ref[...] + dbr_ref[...]) * _f01(rows_q < n_valid)
    a_c = dt_c * (-jnp.exp(alr_ref[...]))
    cols_q = lax.broadcasted_iota(I32, (SSM_HEADS, q_len), 1)
    dt_r = _softplus(dtr_ref[0] + dbc_ref[...]) * _f01(cols_q < n_valid)
    a_r = dt_r * (-jnp.exp(alc_ref[...]))
    ti = lax.broadcasted_iota(I32, (q_len, q_len), 0)
    si = lax.broadcasted_iota(I32, (q_len, q_len), 1)
    causal = ti >= si
    acs_c = _dot_exact(_f01(causal), a_c)
    acs_r = _dot_exact(a_r, _f01(si >= ti))

    cb = []
    for gi in range(SSM_GROUPS):
        sl = slice(gi * SSM_STATE, (gi + 1) * SSM_STATE)
        cb.append(_dot_nt(c_all[:, sl], b_all[:, sl]))
    hpg = SSM_HEADS // SSM_GROUPS
    for h in range(SSM_HEADS):
        gi = h // hpg
        sl = slice(gi * SSM_STATE, (gi + 1) * SSM_STATE)
        col = acs_c[:, h:h + 1]
        decay = jnp.where(causal, jnp.exp(col - acs_r[h:h + 1, :]), 0.0)
        xh = act[:, h * SSM_HEAD_DIM:(h + 1) * SSM_HEAD_DIM]
        xdt = xh * dt_c[:, h:h + 1]
        y = _dot((cb[gi] * decay).astype(BF16), xdt.astype(BF16))
        h_prev = h_ref[h]
        y = y + _dot_nt(c_all[:, sl], h_prev.astype(BF16)) * jnp.exp(col)
        last = acs_c[q_len - 1:q_len, h:h + 1]
        upd = _dot_tn((xdt * jnp.exp(last - col)).astype(BF16), b_all[:, sl])
        h_ref[h] = jnp.exp(last) * h_prev + upd
        ybuf_ref[:, h * SSM_HEAD_DIM:(h + 1) * SSM_HEAD_DIM] = y + xh * dsk_ref[h]

    yz = ybuf_ref[...] * _silu(z_ref[...])
    y_ref[...] = (_rms(yz) * gs_ref[...]).astype(y_ref.dtype)

    @pl.when(c == pl.num_programs(1) - 1)
    def _():
        hf_ref[0] = h_ref[...]


def ssm_mixer(b_sz, q_len, n_chunks, n_valid, xbc, xbc_col, z, z_col, dt_cols, dt_rows, conv_state8, h0, ssm_w,
              out_dtype):
    conv_w, conv_b, dt_bias, a_log, d_skip, g_ssm = ssm_w
    tok = lambda col: (lambda b, c: (b * n_chunks + c, col))
    fixed = lambda shape: pl.BlockSpec(shape, lambda b, c: (0,) * len(shape))
    pad_lanes = lambda v: jnp.pad(v.reshape(1, -1), ((0, 0), (0, LANES - v.shape[0])))
    state_shape = (SSM_HEADS, SSM_HEAD_DIM, SSM_STATE)
    return pl.pallas_call(
        functools.partial(_ssm_kernel, n_valid=n_valid),
        grid=(b_sz, n_chunks),
        in_specs=[pl.BlockSpec((q_len, SSM_CONV_DIM), tok(xbc_col)),
                  pl.BlockSpec((q_len, SSM_INNER), tok(z_col)),
                  pl.BlockSpec((q_len, LANES), tok(0)),
                  pl.BlockSpec((1, SSM_HEADS, q_len), lambda b, c: (b * n_chunks + c, 0, 0)),
                  pl.BlockSpec((1, SUBLANES, SSM_CONV_DIM), lambda b, c: (b, 0, 0)),
                  pl.BlockSpec((1,) + state_shape, lambda b, c: (b, 0, 0, 0)),
                  fixed((SSM_CONV, SSM_CONV_DIM)), fixed((1, SSM_CONV_DIM)),
                  fixed((1, LANES)), fixed((1, LANES)), fixed((SSM_HEADS, 1)), fixed((SSM_HEADS, 1)),
                  fixed((1, SSM_INNER)),
                  pl.BlockSpec(memory_space=pltpu.SMEM)],
        out_specs=[pl.BlockSpec((q_len, SSM_INNER), tok(0)),
                   pl.BlockSpec((1,) + state_shape, lambda b, c: (b, 0, 0, 0))],
        out_shape=[jax.ShapeDtypeStruct((b_sz * n_chunks * q_len, SSM_INNER), out_dtype),
                   jax.ShapeDtypeStruct((b_sz,) + state_shape, F32)],
        scratch_shapes=[pltpu.VMEM((SUBLANES, SSM_CONV_DIM), F32), pltpu.VMEM(state_shape, F32),
                        pltpu.VMEM((q_len, SSM_INNER), F32)],
        compiler_params=_cparams(("arbitrary", "arbitrary")),
        name="ssm_mixer",
    )(xbc, z, dt_cols, dt_rows, conv_state8, h0, conv_w, conv_b.reshape(1, -1), pad_lanes(dt_bias), pad_lanes(a_log),
      dt_bias.reshape(-1, 1), a_log.reshape(-1, 1), g_ssm.reshape(1, -1), d_skip)


def _moba_dec_kernel(pt_ref, q_ref, kn_ref, vn_ref, *rest, n_pages, n_new):
    pages, o_ref = rest[:n_pages], rest[n_pages]
    rows = q_ref.shape[2]
    ppb = MOBA_BLOCK // LANES
    nb_past = n_pages // ppb
    lane = lax.broadcasted_iota(I32, (rows, LANES), 1)
    lane_k = lax.broadcasted_iota(I32, (HEAD_DIM, LANES), 1)
    t_row = lax.broadcasted_iota(I32, (rows, SUBLANES), 0) % n_new
    t_col = lax.broadcasted_iota(I32, (rows, SUBLANES), 1)
    new_ok = _f01((t_col <= t_row) & (t_col < n_new))
    for kvh in range(MOBA_KV_HEADS):
        q = q_ref[0, kvh].astype(BF16)
        scores = []
        km = jnp.zeros((HEAD_DIM, LANES), F32)
        for j in range(nb_past):
            ksum = jnp.zeros((HEAD_DIM, LANES), F32)
            for pp in range(ppb):
                kt = pages[j * ppb + pp][0, 0, 0, kvh]
                scores.append(_dot(q, kt.astype(BF16)) * ATT_SCALE)
                ksum = ksum + kt
            mean = jnp.sum(ksum, axis=1, keepdims=True) * (1.0 / MOBA_BLOCK)
            km = jnp.where(lane_k == j, mean, km)
        gate = jnp.where(lane < nb_past, _dot(q, km.astype(BF16)), -jnp.inf)
        cnt = jnp.zeros((rows, LANES), F32)
        for j2 in range(nb_past):
            col = gate[:, j2:j2 + 1]
            cnt = cnt + _f01((col > gate) | ((col == gate) & (j2 < lane)))
        sel = _f01((lane < nb_past) & (cnt < MOBA_TOPK))
        s_new = jnp.where(new_ok > 0.5, _dot_nt(q, kn_ref[0, kvh].astype(BF16)) * ATT_SCALE, NEG_BIG)
        m = jnp.max(s_new, axis=1, keepdims=True)
        chosen = []
        for p in range(n_pages):
            a = sel[:, p // ppb:p // ppb + 1]
            scores[p] = jnp.where(a > 0.5, scores[p], NEG_BIG)
            chosen.append(a)
            m = jnp.maximum(m, jnp.max(scores[p], axis=1, keepdims=True))
        p_new = jnp.exp(s_new - m) * new_ok
        l = jnp.sum(p_new, axis=1, keepdims=True)
        o = _dot(p_new.astype(BF16), vn_ref[0, kvh].astype(BF16))
        for p in range(n_pages):
            pr = jnp.exp(scores[p] - m) * chosen[p]
            l = l + jnp.sum(pr, axis=1, keepdims=True)
            o = o + _dot_nt(pr.astype(BF16), pages[p][0, 0, 1, kvh].astype(BF16))
        o_ref[0, kvh] = o / l


def _page_specs(block, n_pages, layer, block_idx):
    return [pl.BlockSpec(block, functools.partial(lambda b, pt, j: (layer, pt[b, j]) + block_idx, j=j))
            for j in range(n_pages)]


def moba_decode(grp, q_dec, k_new, v_new, cache, layer, page_table):
    b_sz = grp["B"]
    n_pages = page_table.shape[1]
    per_seq = lambda shape: pl.BlockSpec((1,) + shape, lambda b, pt: (b,) + (0,) * len(shape))
    q_shape, n_shape = q_dec.shape[1:], k_new.shape[1:]
    specs = [per_seq(q_shape), per_seq(n_shape), per_seq(n_shape)]
    specs += _page_specs((1, 1, 2, MOBA_KV_HEADS, HEAD_DIM, LANES), n_pages, layer, (0, 0, 0, 0))
    return pl.pallas_call(
        functools.partial(_moba_dec_kernel, n_pages=n_pages, n_new=grp["L"]),
        grid_spec=pltpu.PrefetchScalarGridSpec(num_scalar_prefetch=1, grid=(b_sz,), in_specs=specs,
                                               out_specs=per_seq(q_shape)),
        out_shape=jax.ShapeDtypeStruct(q_dec.shape, F32),
        compiler_params=_cparams(("arbitrary",)),
        name="moba_decode",
    )(page_table, q_dec, k_new, v_new, *([cache] * n_pages))


def _nsa_dec_kernel(pt_ref, q_ref, kn_ref, vn_ref, kwn_ref, vwn_ref, gt_ref, kc_ref, win_ref, ov_ref, ex_ref, *rest,
                    n_pages, n_new, past, ns, nc):
    pages, o_ref = rest[:n_pages], rest[n_pages]
    rows = q_ref.shape[2]
    ncp = kc_ref.shape[2]
    wlen = win_ref.shape[-1]
    t_rowl = lax.broadcasted_iota(I32, (rows, LANES), 0) % n_new
    lane = lax.broadcasted_iota(I32, (rows, LANES), 1)
    t_row8 = lax.broadcasted_iota(I32, (rows, SUBLANES), 0) % n_new
    t_col8 = lax.broadcasted_iota(I32, (rows, SUBLANES), 1)
    new_ok = _f01((t_col8 <= t_row8) & (t_col8 < n_new))

    def softmax_parts(parts):
        m = None
        masked = []
        for s, a in parts:
            s = jnp.where(a > 0.5, s, NEG_BIG)
            masked.append(s)
            mx = jnp.max(s, axis=1, keepdims=True)
            m = mx if m is None else jnp.maximum(m, mx)
        ps = [jnp.exp(s - m) * a for s, (_, a) in zip(masked, parts)]
        l = ps[0].sum(axis=1, keepdims=True)
        for p in ps[1:]:
            l = l + p.sum(axis=1, keepdims=True)
        inv = 1.0 / jnp.where(l > 0.0, l, 1.0)
        return [p * inv for p in ps]

    heads = range(NSA_KV_HEADS)
    qpos = past + t_rowl
    cidx = lax.broadcasted_iota(I32, (rows, ncp), 1)
    qpos_c = past + lax.broadcasted_iota(I32, (rows, ncp), 0) % n_new
    cmp_ok = _f01((cidx * NSA_CMP_STRIDE + (NSA_CMP_LEN - 1) <= qpos_c) & (cidx < nc))
    jw = lax.broadcasted_iota(I32, (rows, wlen), 1)
    tw = lax.broadcasted_iota(I32, (rows, wlen), 0) % n_new
    dist = wlen + tw - jw
    win_okay = _f01((dist >= 0) & (dist <= NSA_WINDOW) & (past - wlen + jw >= 0))

    qs = [q_ref[0, h].astype(BF16) for h in heads]
    s_cmp = [_dot_nt(qs[h], kc_ref[0, h]) * ATT_SCALE for h in heads]
    s_pages = [[_dot(qs[h], pages[p][0, 0, 0, h].astype(BF16)) * ATT_SCALE for p in range(n_pages)] for h in heads]
    s_new = [_dot_nt(qs[h], kn_ref[0, h].astype(BF16)) * ATT_SCALE for h in heads]
    s_win = [_dot(qs[h], win_ref[0, 0, 0, h].astype(BF16)) * ATT_SCALE for h in heads]
    s_wnew = [_dot_nt(qs[h], kwn_ref[0, h].astype(BF16)) * ATT_SCALE for h in heads]

    pbs = [softmax_parts([(s_cmp[h], cmp_ok)])[0].astype(BF16) for h in heads]
    o_cmp = [_dot(pbs[h], kc_ref[0, NSA_KV_HEADS + h]) for h in heads]
    ps_all = [_dot(pbs[h], ov_ref[...]) for h in heads]
    p_win = [softmax_parts([(s_win[h], win_okay), (s_wnew[h], new_ok)]) for h in heads]
    o_win = [_dot_nt(p_win[h][0].astype(BF16), win_ref[0, 0, 1, h].astype(BF16))
             + _dot(p_win[h][1].astype(BF16), vwn_ref[0, h].astype(BF16)) for h in heads]

    sels = []
    for h in heads:
        p_slc = ps_all[h]
        for g in range(1, rows // n_new):
            p_slc = p_slc + pltpu.roll(ps_all[h], g * n_new, 0)
        cur = qpos // NSA_SLC_BLOCK
        forced = (lane == 0) | (lane == cur) | (lane == cur - 1)
        elig = lane <= cur
        score = jnp.where(forced, jnp.inf, p_slc)
        score = jnp.where(elig, score, -jnp.inf)
        cnt = jnp.zeros((rows, LANES), F32)
        for j2 in range(ns):
            col = score[:, j2:j2 + 1]
            cnt = cnt + _f01((col > score) | ((col == score) & (j2 < lane)))
        sels.append(_f01(elig & (cnt < NSA_TOPN)).astype(BF16))
    key_ok = [_dot(sels[h], ex_ref[...]) for h in heads]

    p_slc_all = []
    for h in heads:
        parts = [(s_pages[h][p], key_ok[h][:, p * LANES:(p + 1) * LANES]) for p in range(n_pages)]
        parts.append((s_new[h], new_ok))
        p_slc_all.append([p.astype(BF16) for p in softmax_parts(parts)])
    for h in heads:
        probs = p_slc_all[h]
        o_slc = _dot(probs[-1], vn_ref[0, h].astype(BF16))
        for p in range(n_pages):
            o_slc = o_slc + _dot_nt(probs[p], pages[p][0, 0, 1, h].astype(BF16))
        gates = _sigmoid(gt_ref[0, h])
        o_ref[0, h] = gates[:, 0:1] * o_cmp[h] + gates[:, 1:2] * o_slc + gates[:, 2:3] * o_win[h]


def nsa_decode(grp, q_dec, new_rows, gates_dec, kc_rows, win_state, cache, layer, page_table, past):
    b_sz, n_new = grp["B"], grp["L"]
    n_pages = page_table.shape[1]
    total = past + n_new
    ns = -(-total // NSA_SLC_BLOCK)
    nc = (total - NSA_CMP_LEN) // NSA_CMP_STRIDE + 1
    ncp = kc_rows.shape[2]
    assert ns <= LANES and past % NSA_SLC_BLOCK == 0 and n_new <= NSA_SLC_BLOCK
    assert (nc - 1) * NSA_CMP_STRIDE + NSA_CMP_LEN <= past, "compressed blocks must not reach the new rows"
    overlap = jnp.asarray(_overlap_matrix(ncp, LANES), BF16)
    expand = jnp.asarray((np.arange(past)[None, :] // NSA_SLC_BLOCK == np.arange(LANES)[:, None]), BF16)
    per_seq = lambda shape: pl.BlockSpec((1,) + shape, lambda b, pt: (b,) + (0,) * len(shape))
    fixed = lambda shape: pl.BlockSpec(shape, lambda b, pt: (0,) * len(shape))
    k_new, v_new, kw_new, vw_new = new_rows
    specs = [per_seq(q_dec.shape[1:])] + [per_seq(k_new.shape[1:])] * 4
    specs += [per_seq(gates_dec.shape[1:]), per_seq(kc_rows.shape[1:]),
              pl.BlockSpec((1, 1) + win_state.shape[2:], lambda b, pt: (layer, b, 0, 0, 0, 0)),
              fixed(overlap.shape), fixed(expand.shape)]
    specs += _page_specs((1, 1, 2, NSA_KV_HEADS, HEAD_DIM, LANES), n_pages, layer, (1, 0, 0, 0))
    return pl.pallas_call(
        functools.partial(_nsa_dec_kernel, n_pages=n_pages, n_new=n_new, past=past, ns=ns, nc=nc),
        grid_spec=pltpu.PrefetchScalarGridSpec(num_scalar_prefetch=1, grid=(b_sz,), in_specs=specs,
                                               out_specs=per_seq(q_dec.shape[1:])),
        out_shape=jax.ShapeDtypeStruct(q_dec.shape, F32),
        compiler_params=_cparams(("arbitrary",)),
        name="nsa_decode",
    )(page_table, q_dec, k_new, v_new, kw_new, vw_new, gates_dec, kc_rows, win_state, overlap, expand,
      *([cache] * n_pages))


def _tile(n, prefs):
    for t in prefs:
        if n % t == 0:
            return t
    return n


def _prep_layer_weights(w_in, w_branch, w_out, cmp_w1, cmp_w2):
    wt = jnp.swapaxes(w_in, 1, 2)
    depth = wt.shape[0]
    zeros = lambda r: jnp.zeros((depth, r, wt.shape[2]), wt.dtype)
    w_att = jnp.concatenate([wt[:, 0:1024], wt[:, 2568:3872], wt[:, 2560:2568], zeros(T_ROWS - 2336)], axis=1)
    w_row = jnp.concatenate([wt[:, 1536:2560], wt[:, 1024:1536]], axis=1)
    w_dt = jnp.concatenate([wt[:, 2560:2568], zeros(LANES - SSM_HEADS)], axis=1)
    return (w_att.astype(BF16), w_row.astype(BF16), w_dt.astype(BF16), wt[:, 3872:6944].astype(BF16),
            w_branch.astype(BF16), w_out.astype(BF16), cmp_w1.astype(BF16), jnp.swapaxes(cmp_w2, 2, 3).astype(BF16))


def _heads_first(a, b_sz, n_new, dims, pad_to=None):
    a = a.reshape((b_sz, n_new) + dims + (HEAD_DIM,))
    nd = len(dims)
    a = jnp.transpose(a, (0,) + tuple(range(2, 2 + nd)) + (1, 2 + nd))
    if pad_to is not None and pad_to > n_new:
        a = jnp.pad(a, [(0, 0)] * (1 + nd) + [(0, pad_to - n_new), (0, 0)])
    return a


def _layer_prompt(grp, x, h, mod, mod_next, lw, ffn_w, g_next, last):
    (w_att, w_row, w_dt, w_mgate, w_branch, w_out, cmp_w, ssm_w, g_ffn) = lw
    b_sz, seq, n = grp["B"], grp["L"], grp["N"]
    proj_t = nt_matmul(w_att, h, _tile(T_ROWS, (640, 512)), _tile(n, (1024, 512)))
    row_proj = nt_matmul(h, w_row, grp["tm"], _tile(R_COLS, (1536, 512)))
    dt_cols = nt_matmul(h, w_dt, grp["tm"], LANES)

    moba_out = moba_prompt(grp, proj_t)

    q_len = SSM_CHUNK
    n_chunks = seq // q_len
    dt_rows = proj_t[T_DT:T_DT + SSM_HEADS].reshape(SSM_HEADS, b_sz * n_chunks, q_len).transpose(1, 0, 2)
    ssm_out, new_ssm = ssm_mixer(
        b_sz, q_len, n_chunks, q_len, row_proj, R_XBC // SSM_CONV_DIM, row_proj, R_Z // SSM_INNER, dt_cols, dt_rows,
        jnp.zeros((b_sz, SUBLANES, SSM_CONV_DIM), F32), jnp.zeros((b_sz, SSM_HEADS, SSM_HEAD_DIM, SSM_STATE), F32),
        ssm_w, BF16)

    kc_rows, vc_t = compress_prompt(grp, proj_t, cmp_w)
    nsa_out = nsa_prompt(grp, proj_t, kc_rows, vc_t)

    x1, h2 = merge_branches(grp, x, (moba_out, ssm_out, nsa_out), h, w_mgate, w_branch, w_out, g_ffn, mod)
    x2, hn = _ffn(grp, x1, h2, ffn_w, mod, g_next, mod_next, last)

    def rows_of(lo, hi, dims):
        return proj_t[lo:hi].reshape(dims + (HEAD_DIM, b_sz, seq)).transpose(3, 4, 0, 1, 2)

    new_moba = rows_of(T_MK, T_NQ, (2, MOBA_KV_HEADS))
    new_nsa = rows_of(T_NKV, T_NKV + 4 * NSA_KV_HEADS * HEAD_DIM, (4, NSA_KV_HEADS))
    win_len = grp["win_len"]
    assert seq >= win_len
    new_win = rows_of(T_NKV + 4 * NSA_KV_HEADS * HEAD_DIM, T_NGATE, (2, NSA_KV_HEADS))[:, seq - win_len:]
    new_conv = row_proj[:, R_XBC:R_XBC + SSM_CONV_DIM].reshape(b_sz, seq, SSM_CONV_DIM)[:, seq - (SSM_CONV - 1):]
    return x2, hn, (new_moba, new_nsa, new_win, new_ssm, new_conv)


def _ffn(grp, x1, h2, ffn_w, mod, g_next, mod_next, last):
    if len(ffn_w) == 4:
        w_router_t, wg, wu, wd = ffn_w
        y = moe_experts(grp, h2, moe_router(grp, h2, w_router_t), wg, wu, wd)
        return ffn_finish(grp, x1, y, mod, g_next, mod_next, last)
    wg, wu, wd = ffn_w
    combine = jnp.ones((x1.shape[0], LANES), F32)
    return ffn_block(grp, x1, h2, combine, wg, wu, wd, mod, g_next, mod_next, last)


def _layer_sample(grp, x, h, mod, mod_next, lw, ffn_w, g_next, last, caches):
    (w_att, w_row, w_dt, w_mgate, w_branch, w_out, cmp_w, ssm_w, g_ffn) = lw
    moba_cache, nsa_cache, win_state, win_prev, conv_state, ssm_state, layer, page_table, past = caches
    b_sz, n_new, n = grp["B"], grp["L"], grp["N"]
    att = nt_matmul(h, w_att, grp["tm"], _tile(T_ROWS, (640, 512)))
    row_proj = nt_matmul(h, w_row, grp["tm"], _tile(R_COLS, (1536, 512)))
    dt_cols = nt_matmul(h, w_dt, grp["tm"], LANES)

    g_m = MOBA_HEADS // MOBA_KV_HEADS
    q_dec = _heads_first(att[:, T_MQ:T_MK], b_sz, n_new, (MOBA_KV_HEADS, g_m)).reshape(b_sz, MOBA_KV_HEADS, g_m * n_new, HEAD_DIM)
    k_new = _heads_first(att[:, T_MK:T_MV], b_sz, n_new, (MOBA_KV_HEADS,), SUBLANES)
    v_new = _heads_first(att[:, T_MV:T_NQ], b_sz, n_new, (MOBA_KV_HEADS,), SUBLANES)
    o = moba_decode(grp, q_dec, k_new, v_new, moba_cache, layer, page_table)
    moba_out = o.reshape(b_sz, MOBA_KV_HEADS, g_m, n_new, HEAD_DIM).transpose(0, 3, 1, 2, 4).reshape(n, -1)

    q_len = SUBLANES
    pad_t = lambda a: jnp.pad(a.reshape(b_sz, n_new, -1), ((0, 0), (0, q_len - n_new), (0, 0))).reshape(b_sz * q_len, -1)
    xbc = row_proj[:, R_XBC:R_XBC + SSM_CONV_DIM]
    dt_rows = jnp.pad(att[:, T_DT:T_DT + SSM_HEADS].reshape(b_sz, n_new, SSM_HEADS).transpose(0, 2, 1),
                      ((0, 0), (0, 0), (0, q_len - n_new)))
    conv8 = jnp.pad(conv_state, ((0, 0), (SUBLANES - (SSM_CONV - 1), 0), (0, 0)))
    ssm_pad, new_ssm = ssm_mixer(b_sz, q_len, 1, n_new, pad_t(xbc), 0, pad_t(row_proj[:, R_Z:R_Z + SSM_INNER]), 0,
                                 pad_t(dt_cols), dt_rows, conv8, ssm_state, ssm_w, F32)
    ssm_out = ssm_pad.reshape(b_sz, q_len, SSM_INNER)[:, :n_new].reshape(n, SSM_INNER)

    g_n = NSA_HEADS // NSA_KV_HEADS
    nq_dec = _heads_first(att[:, T_NQ:T_NKV], b_sz, n_new, (NSA_KV_HEADS, g_n)).reshape(b_sz, NSA_KV_HEADS, g_n * n_new, HEAD_DIM)
    sets = _heads_first(att[:, T_NKV:T_NGATE], b_sz, n_new, (6, NSA_KV_HEADS), SUBLANES)
    gates = att[:, T_NGATE:T_NGATE + 3 * NSA_HEADS].reshape(b_sz, n_new, 3, NSA_KV_HEADS, g_n)
    gates = gates.transpose(0, 3, 4, 1, 2).reshape(b_sz, NSA_KV_HEADS, g_n * n_new, 3)
    gates = jnp.pad(gates, ((0, 0), (0, 0), (0, 0), (0, LANES - 3)))
    kc_rows, _ = compress_paged(grp, nsa_cache, layer, page_table, cmp_w)
    o = nsa_decode(grp, nq_dec, (sets[:, 2], sets[:, 3], sets[:, 4], sets[:, 5]), gates, kc_rows, win_state,
                   nsa_cache, layer, page_table, past)
    nsa_out = o.reshape(b_sz, NSA_KV_HEADS, g_n, n_new, HEAD_DIM).transpose(0, 3, 1, 2, 4).reshape(n, -1)

    x1, h2 = merge_branches(grp, x, (moba_out, ssm_out, nsa_out), h, w_mgate, w_branch, w_out, g_ffn, mod)
    x2, hn = _ffn(grp, x1, h2, ffn_w, mod, g_next, mod_next, last)

    new_moba = att[:, T_MK:T_NQ].reshape(b_sz, n_new, 2, MOBA_KV_HEADS, HEAD_DIM)
    new_nsa = att[:, T_NKV:T_NKV + 4 * NSA_KV_HEADS * HEAD_DIM].reshape(b_sz, n_new, 4, NSA_KV_HEADS, HEAD_DIM)
    win_rows = att[:, T_NKV + 4 * NSA_KV_HEADS * HEAD_DIM:T_NGATE].reshape(b_sz, n_new, 2, NSA_KV_HEADS, HEAD_DIM)
    new_conv = jnp.concatenate([conv_state, xbc.reshape(b_sz, n_new, -1)], axis=1)[:, -(SSM_CONV - 1):]
    return x2, hn, (new_moba, new_nsa, win_rows, new_ssm, new_conv)


def kernel(x_prompt, x_sample, c_prompt, c_sample, cache_moba_kv, cache_nsa_kv, state_nsa_win_kv, state_ssm, state_conv, page_table, w_ada, b_ada, g_mix, w_in, conv_w, conv_b, dt_bias, a_log, d_skip, g_ssm, cmp_pe, cmp_w1, cmp_b1, cmp_w2, cmp_b2, w_branch, w_out, g_ffn, w_ffn_gate, w_ffn_up, w_ffn_down, w_router, w_exp_gate, w_exp_up, w_exp_down, g_final):
    bp, seq, d = x_prompt.shape
    bs, n_new, _ = x_sample.shape
    depth = w_in.shape[0]
    n_pages, page = page_table.shape[1], cache_moba_kv.shape[2]
    past = n_pages * page
    win_len = state_nsa_win_kv.shape[2]
    assert page == LANES and past % MOBA_BLOCK == 0 and n_new <= SUBLANES and win_len == min(NSA_WINDOW, past)

    n_p, n_s = bp * seq, bs * n_new
    grp_p = dict(B=bp, L=seq, N=n_p, tm=_tile(seq, (512, 256, 128)), per_token_mod=False, win_len=win_len)
    grp_s = dict(B=bs, L=n_new, N=n_s, tm=_tile(n_s, (512, 256, 128)), per_token_mod=True, win_len=win_len)

    rows = bp + bs
    rows_pad = -(-rows // SUBLANES) * SUBLANES
    c_all = jnp.pad(jnp.concatenate([c_prompt, c_sample], axis=0), ((0, rows_pad - rows), (0, 0)))
    mod_all = ada_modulation(c_all, w_ada, b_ada)
    mod_p = [jnp.repeat(mod_all[l, :bp], SUBLANES, axis=0) for l in range(depth)]
    mod_s = [jnp.repeat(mod_all[l, bp:rows], n_new, axis=0) for l in range(depth)]

    w_att, w_row, w_dt, w_mg, wb, wo, w1, w2t = _prep_layer_weights(w_in, w_branch, w_out, cmp_w1, cmp_w2)
    groups = LANES // NSA_CMP_STRIDE
    pe_codes = cmp_pe.reshape(depth, 2, NSA_CMP_LEN // NSA_CMP_STRIDE, NSA_CMP_STRIDE, HEAD_DIM)
    pe_codes = jnp.swapaxes(jnp.tile(pe_codes, (1, 1, 1, groups, 1)), -1, -2)
    w_router_t = jnp.pad(jnp.swapaxes(w_router, 1, 2), ((0, 0), (0, LANES - N_EXPERTS), (0, 0))).astype(BF16)
    dense_w = (w_ffn_gate.astype(BF16), w_ffn_up.astype(BF16), w_ffn_down.astype(BF16))
    moe_w = (w_exp_gate.astype(BF16), w_exp_up.astype(BF16), w_exp_down.astype(BF16))

    moba_cache = jnp.transpose(cache_moba_kv, (0, 1, 3, 4, 5, 2))
    nsa_cache = jnp.transpose(cache_nsa_kv, (0, 1, 3, 4, 5, 2))
    win_state = jnp.transpose(state_nsa_win_kv, (0, 1, 3, 4, 5, 2))

    xp = x_prompt.reshape(n_p, d)
    xs = x_sample.reshape(n_s, d)
    hp = norm_modulate(grp_p, xp, g_mix[0], mod_p[0])
    hs = norm_modulate(grp_s, xs, g_mix[0], mod_s[0])
    st_p, st_s = [], []
    for l in range(depth):
        last = l == depth - 1
        if l % 2:
            ffn_w = (w_router_t[l // 2],) + tuple(w[l // 2] for w in moe_w)
        else:
            ffn_w = tuple(w[l // 2][None] for w in dense_w)
        cmp_w = (pe_codes[l], w1[l], cmp_b1[l], w2t[l], cmp_b2[l])
        ssm_w = (conv_w[l], conv_b[l], dt_bias[l], a_log[l], d_skip[l], g_ssm[l])
        lw = (w_att[l], w_row[l], w_dt[l], w_mg[l], wb[l], wo[l], cmp_w, ssm_w, g_ffn[l])
        g_next = g_final if last else g_mix[l + 1]
        xp, hp, new_p = _layer_prompt(grp_p, xp, hp, mod_p[l], None if last else mod_p[l + 1], lw, ffn_w, g_next, last)
        caches = (moba_cache, nsa_cache, win_state, state_nsa_win_kv[l], state_conv[l], state_ssm[l], l, page_table, past)
        xs, hs, new_s = _layer_sample(grp_s, xs, hs, mod_s[l], None if last else mod_s[l + 1], lw, ffn_w, g_next, last,
                                      caches)
        st_p.append(new_p)
        st_s.append(new_s)

    stack = lambda sts, k: jnp.stack([s[k] for s in sts])
    y_prompt = hp.reshape(bp, seq, d)
    y_sample = hs.reshape(bs, n_new, d)
    new_win_s = jnp.concatenate([state_nsa_win_kv, stack(st_s, 2)], axis=2)[:, :, -win_len:]
    return (y_prompt, y_sample, stack(st_p, 0), stack(st_s, 0), stack(st_p, 1), stack(st_s, 1), stack(st_p, 2),
            new_win_s, stack(st_p, 3), stack(st_s, 3), stack(st_p, 4), stack(st_s, 4))
```

```python
import functools

import numpy as np
import jax
import jax.numpy as jnp
from jax import lax
from jax.experimental import pallas as pl
from jax.experimental.pallas import tpu as pltpu

F32 = jnp.float32
BF16 = jnp.bfloat16
I32 = jnp.int32

D_MODEL = 1024
HEAD_DIM = 64
MOBA_HEADS, MOBA_KV_HEADS, MOBA_BLOCK, MOBA_TOPK = 8, 4, 256, 3
SSM_HEADS, SSM_HEAD_DIM, SSM_INNER, SSM_GROUPS, SSM_STATE, SSM_CONV, SSM_CHUNK = 8, 64, 512, 2, 128, 4, 128
SSM_CONV_DIM = SSM_INNER + 2 * SSM_GROUPS * SSM_STATE
NSA_HEADS, NSA_KV_HEADS = 8, 2
NSA_CMP_LEN, NSA_CMP_STRIDE, NSA_CMP_HIDDEN = 32, 16, 256
NSA_SLC_BLOCK, NSA_TOPN, NSA_WINDOW = 64, 16, 512
N_BRANCHES, BRANCH_WIDTH = 3, 512
N_EXPERTS = 8
RMS_EPS = 1e-6
NEG_BIG = -1e30
ATT_SCALE = HEAD_DIM ** -0.5

LANES = 128
SUBLANES = 8
VMEM_LIMIT = 56 * 1024 * 1024

T_MQ, T_MK, T_MV, T_NQ, T_NKV, T_NGATE, T_DT = 0, 512, 768, 1024, 1536, 2304, 2328
T_ROWS = 2560
R_XBC, R_Z = 0, 1024
R_COLS = 1536


def _cparams(sem):
    return pltpu.CompilerParams(dimension_semantics=sem, vmem_limit_bytes=VMEM_LIMIT)


def _silu(x):
    return x * (1.0 / (1.0 + jnp.exp(-x)))


def _sigmoid(x):
    return 1.0 / (1.0 + jnp.exp(-x))


def _softplus(x):
    return jnp.maximum(x, 0.0) + jnp.log(1.0 + jnp.exp(-jnp.abs(x)))


def _dot(a, b):
    return jnp.dot(a, b, preferred_element_type=F32)


def _dot_nt(a, b):
    return lax.dot_general(a, b, (((1,), (1,)), ((), ())), preferred_element_type=F32)


def _dot_tn(a, b):
    return lax.dot_general(a, b, (((0,), (0,)), ((), ())), preferred_element_type=F32)


def _dot_exact(a, b):
    return jnp.dot(a, b, preferred_element_type=F32, precision=lax.Precision.HIGHEST)


def _f01(mask):
    return jnp.where(mask, 1.0, 0.0)


def _mod_rows(ref, rows):
    m = ref[...]
    return m if m.shape[0] == rows else m[0:1]


def _rms(x):
    return x * lax.rsqrt(jnp.mean(x * x, axis=-1, keepdims=True) + RMS_EPS)


def _softmax_init(width):
    return (jnp.full((1, width), NEG_BIG, F32), jnp.zeros((1, width), F32), jnp.zeros((HEAD_DIM, width), F32))


def _softmax_steps(states, scores, values):
    partial = []
    for (m, l, acc), s in zip(states, scores):
        m_new = jnp.maximum(m, jnp.max(s, axis=0, keepdims=True))
        alpha = jnp.exp(m - m_new)
        p = jnp.exp(s - m_new)
        partial.append((m_new, alpha * l + jnp.sum(p, axis=0, keepdims=True), alpha * acc, p.astype(BF16)))
    return tuple((m, l, acc + _dot(v_t, p)) for (m, l, acc, p), v_t in zip(partial, values))


def _softmax_merge(states):
    m = states[0][0]
    for st in states[1:]:
        m = jnp.maximum(m, st[0])
    l = jnp.zeros_like(states[0][1])
    acc = jnp.zeros_like(states[0][2])
    for m_s, l_s, acc_s in states:
        w = jnp.exp(m_s - m)
        l = l + w * l_s
        acc = acc + w * acc_s
    return acc / l


def _mod_spec(grp, chunk, tm):
    if grp["per_token_mod"]:
        return pl.BlockSpec((tm, D_MODEL), lambda i, *_: (i, chunk))
    tiles_per_seq = grp["L"] // tm
    return pl.BlockSpec((SUBLANES, D_MODEL), lambda i, *_: (i // tiles_per_seq, chunk))


def _ada_kernel(c_ref, w_ref, b_ref, o_ref):
    a = _silu(c_ref[...]).astype(BF16)
    o_ref[0] = _dot(a, w_ref[0].astype(BF16)) + b_ref[0]


def ada_modulation(c_all, w_ada, b_ada):
    depth, d, n6 = w_ada.shape
    rows = c_all.shape[0]
    tn = 1536
    return pl.pallas_call(
        _ada_kernel,
        grid=(depth, n6 // tn),
        in_specs=[pl.BlockSpec((rows, d), lambda l, j: (0, 0)),
                  pl.BlockSpec((1, d, tn), lambda l, j: (l, 0, j)),
                  pl.BlockSpec((1, 1, tn), lambda l, j: (l, 0, j))],
        out_specs=pl.BlockSpec((1, rows, tn), lambda l, j: (l, 0, j)),
        out_shape=jax.ShapeDtypeStruct((depth, rows, n6), F32),
        compiler_params=_cparams(("arbitrary", "arbitrary")),
        name="ada_modulation",
    )(c_all, w_ada, b_ada.reshape(depth, 1, n6))


def _norm_kernel(x_ref, g_ref, sc_ref, sh_ref, o_ref):
    x = x_ref[...]
    rows = x.shape[0]
    y = _rms(x) * g_ref[...]
    o_ref[...] = (y * (1.0 + _mod_rows(sc_ref, rows)) + _mod_rows(sh_ref, rows)).astype(o_ref.dtype)


def norm_modulate(grp, x, g, mod):
    n, d = x.shape
    tm = grp["tm"]
    return pl.pallas_call(
        _norm_kernel,
        grid=(n // tm,),
        in_specs=[pl.BlockSpec((tm, d), lambda i: (i, 0)),
                  pl.BlockSpec((1, d), lambda i: (0, 0)),
                  _mod_spec(grp, 1, tm), _mod_spec(grp, 0, tm)],
        out_specs=pl.BlockSpec((tm, d), lambda i: (i, 0)),
        out_shape=jax.ShapeDtypeStruct((n, d), BF16),
        compiler_params=_cparams(("arbitrary",)),
        name="norm_modulate",
    )(x, g.reshape(1, d), mod, mod)


def _nt_kernel(a_ref, b_ref, o_ref):
    o_ref[...] = _dot_nt(a_ref[...], b_ref[...])


def nt_matmul(a, b, tm, tn, b_outer=False):
    m, k = a.shape
    n = b.shape[0]
    assert m % tm == 0 and n % tn == 0, (a.shape, b.shape, tm, tn)
    grid = (n // tn, m // tm) if b_outer else (m // tm, n // tn)
    ij = (lambda j, i: (i, j)) if b_outer else (lambda i, j: (i, j))
    return pl.pallas_call(
        _nt_kernel,
        grid=grid,
        in_specs=[pl.BlockSpec((tm, k), lambda *g: (ij(*g)[0], 0)),
                  pl.BlockSpec((tn, k), lambda *g: (ij(*g)[1], 0))],
        out_specs=pl.BlockSpec((tm, tn), lambda *g: ij(*g)),
        out_shape=jax.ShapeDtypeStruct((m, n), F32),
        compiler_params=_cparams(("arbitrary", "arbitrary")),
        name="nt_matmul",
    )(a, b)


def _merge_kernel(x_ref, b0_ref, b1_ref, b2_ref, h_ref, wmg_ref, wb_ref, wo_ref, g_ref, gate_ref, sc_ref, sh_ref,
                  x1_ref, h2_ref):
    rows = x_ref.shape[0]
    merged = jnp.zeros((rows, D_MODEL), F32)
    h = h_ref[...]
    for n, b_ref in enumerate((b0_ref, b1_ref, b2_ref)):
        up = _dot(b_ref[...].astype(BF16), wb_ref[n])
        mgate = _dot_nt(h, wmg_ref[n * D_MODEL:(n + 1) * D_MODEL, :])
        merged = merged + _sigmoid(mgate) * up
    y = _dot(merged.astype(BF16), wo_ref[...])
    x1 = x_ref[...] + _mod_rows(gate_ref, rows) * y
    x1_ref[...] = x1
    h2 = _rms(x1) * g_ref[...]
    h2_ref[...] = (h2 * (1.0 + _mod_rows(sc_ref, rows)) + _mod_rows(sh_ref, rows)).astype(BF16)


def merge_branches(grp, x, branches, h, w_mgate, wb, wo, g_ffn, mod):
    n, d = x.shape
    tm = grp["tm"]
    bw = BRANCH_WIDTH
    row = lambda i: (i, 0)
    fixed2 = lambda i: (0, 0)
    once = pl.Buffered(1)
    return pl.pallas_call(
        _merge_kernel,
        grid=(n // tm,),
        in_specs=[pl.BlockSpec((tm, d), row),
                  pl.BlockSpec((tm, bw), row), pl.BlockSpec((tm, bw), row), pl.BlockSpec((tm, bw), row),
                  pl.BlockSpec((tm, d), row),
                  pl.BlockSpec((N_BRANCHES * d, d), fixed2, pipeline_mode=once),
                  pl.BlockSpec((N_BRANCHES, bw, d), lambda i: (0, 0, 0), pipeline_mode=once),
                  pl.BlockSpec((d, d), fixed2, pipeline_mode=once),
                  pl.BlockSpec((1, d), fixed2),
                  _mod_spec(grp, 2, tm), _mod_spec(grp, 4, tm), _mod_spec(grp, 3, tm)],
        out_specs=[pl.BlockSpec((tm, d), row), pl.BlockSpec((tm, d), row)],
        out_shape=[jax.ShapeDtypeStruct((n, d), F32), jax.ShapeDtypeStruct((n, d), BF16)],
        compiler_params=_cparams(("arbitrary",)),
        name="merge_branches",
    )(x, *branches, h, w_mgate, wb, wo, g_ffn.reshape(1, d), mod, mod, mod)


def _router_kernel(h_ref, w_ref, o_ref, ot_ref, slot_ref, slott_ref):
    logits = _dot_nt(h_ref[...], w_ref[...])
    lane = lax.broadcasted_iota(I32, logits.shape, 1)
    logits = jnp.where(lane < N_EXPERTS, logits, -jnp.inf)
    m1 = jnp.max(logits, axis=-1, keepdims=True)
    i1 = jnp.min(jnp.where(logits == m1, lane, LANES), axis=-1, keepdims=True)
    rest = jnp.where(lane == i1, -jnp.inf, logits)
    m2 = jnp.max(rest, axis=-1, keepdims=True)
    i2 = jnp.min(jnp.where(rest == m2, lane, LANES), axis=-1, keepdims=True)
    e2 = jnp.exp(m2 - m1)
    den = 1.0 + e2
    combine = jnp.where(lane == i1, 1.0 / den, 0.0) + jnp.where(lane == i2, e2 / den, 0.0)
    o_ref[...] = combine
    ot_ref[...] = combine.T
    tm = combine.shape[0]
    routed = combine > 0.0
    earlier = lax.broadcasted_iota(I32, (tm, tm), 1) < lax.broadcasted_iota(I32, (tm, tm), 0)
    before = _dot(_f01(earlier).astype(BF16), _f01(routed).astype(BF16))
    slot = jnp.where(routed, before, -1.0)
    slot_ref[...] = slot
    slott_ref[...] = slot.T


def moe_router(grp, h2, w_router_t):
    n, d = h2.shape
    tm = grp["tm"]
    by_tok = pl.BlockSpec((tm, LANES), lambda i: (i, 0))
    by_exp = pl.BlockSpec((LANES, tm), lambda i: (0, i))
    tok_shape, exp_shape = jax.ShapeDtypeStruct((n, LANES), F32), jax.ShapeDtypeStruct((LANES, n), F32)
    return pl.pallas_call(
        _router_kernel,
        grid=(n // tm,),
        in_specs=[pl.BlockSpec((tm, d), lambda i: (i, 0)), pl.BlockSpec((LANES, d), lambda i: (0, 0))],
        out_specs=[by_tok, by_exp, by_tok, by_exp],
        out_shape=[tok_shape, exp_shape, tok_shape, exp_shape],
        compiler_params=_cparams(("arbitrary",)),
        name="moe_router",
    )(h2, w_router_t)


MOE_UNIT = 64
MOE_MAX_UNITS = 4


def _moe_kernel(cnt_ref, h_ref, cmbt_ref, slot_ref, slott_ref, wg_ref, wu_ref, wd_ref, y_ref):
    s_idx, e, f, j = (pl.program_id(k) for k in range(4))
    tm = h_ref.shape[0]
    i = s_idx * pl.num_programs(3) + j
    w_row = cmbt_ref[pl.ds(e, 1), :]
    slot_row = slott_ref[pl.ds(e, 1), :]
    slots = slot_ref[...]
    lane = lax.broadcasted_iota(I32, slots.shape, 1)
    slot_col = jnp.sum(jnp.where(lane == e, slots, 0.0), axis=-1, keepdims=True)
    h = h_ref[...]
    rows = pl.ds(pl.multiple_of(j * tm, tm), tm)

    @pl.when((e == 0) & (f == 0))
    def _():
        y_ref[rows, :] = jnp.zeros((tm, D_MODEL), F32)

    def run_slots(base, cap):
        base = base.astype(F32)
        slot_r = lax.broadcasted_iota(I32, (cap, tm), 0).astype(F32)
        slot_c = lax.broadcasted_iota(I32, (tm, cap), 1).astype(F32)
        p = _f01(slot_row - base == slot_r)
        pt = _f01(slot_col - base == slot_c).astype(BF16)
        xc = _dot(p.astype(BF16), h).astype(BF16)
        a = _silu(_dot(xc, wg_ref[0])) * _dot(xc, wu_ref[0])
        out = _dot(a.astype(BF16), wd_ref[0])
        out = out * jnp.sum(p * w_row, axis=-1, keepdims=True)
        hi = out.astype(BF16)
        lo = (out - hi.astype(F32)).astype(BF16)
        y_ref[rows, :] += _dot(pt, hi) + _dot(pt, lo)

    n_units = (cnt_ref[i, e] + MOE_UNIT - 1) // MOE_UNIT
    for k in range(1, MOE_MAX_UNITS + 1):
        @pl.when(jnp.minimum(n_units, MOE_MAX_UNITS) == k)
        def _(k=k):
            run_slots(jnp.int32(0), k * MOE_UNIT)

    def overflow(k, carry):
        run_slots((MOE_MAX_UNITS + k) * MOE_UNIT, MOE_UNIT)
        return carry

    lax.fori_loop(0, jnp.maximum(n_units - MOE_MAX_UNITS, 0), overflow, 0)


def moe_experts(grp, h2, routing, wg, wu, wd):
    combine, combine_t, slot, slot_t = routing
    n, d = h2.shape
    n_exp, _, ff = wg.shape
    tm = grp["tm"]
    sup = _tile(n, (4 * tm, 2 * tm))
    n_j = sup // tm
    tf = _tile(ff, (1408,))
    counts = jnp.sum((combine[:, :n_exp] > 0.0).reshape(n // tm, tm, n_exp), axis=1).astype(I32)
    tok = lambda s, e, f, j, cnt: (s * n_j + j, 0)
    by_exp = pl.BlockSpec((LANES, tm), lambda s, e, f, j, cnt: (0, s * n_j + j))
    return pl.pallas_call(
        _moe_kernel,
        grid_spec=pltpu.PrefetchScalarGridSpec(
            num_scalar_prefetch=1, grid=(n // sup, n_exp, ff // tf, n_j),
            in_specs=[pl.BlockSpec((tm, d), tok), by_exp, pl.BlockSpec((tm, LANES), tok), by_exp,
                      pl.BlockSpec((1, d, tf), lambda s, e, f, j, cnt: (e, 0, f)),
                      pl.BlockSpec((1, d, tf), lambda s, e, f, j, cnt: (e, 0, f)),
                      pl.BlockSpec((1, tf, d), lambda s, e, f, j, cnt: (e, f, 0))],
            out_specs=pl.BlockSpec((sup, d), lambda s, e, f, j, cnt: (s, 0))),
        out_shape=jax.ShapeDtypeStruct((n, d), F32),
        compiler_params=_cparams(("arbitrary", "arbitrary", "arbitrary", "arbitrary")),
        name="moe_experts",
    )(counts, h2, combine_t, slot, slot_t, wg, wu, wd)


def _finish_kernel(x_ref, y_ref, gate_ref, g_ref, sc_ref, sh_ref, x2_ref, hn_ref):
    rows = x_ref.shape[0]
    x2 = x_ref[...] + _mod_rows(gate_ref, rows) * y_ref[...]
    x2_ref[...] = x2
    hn = _rms(x2) * g_ref[...]
    hn_ref[...] = (hn * (1.0 + _mod_rows(sc_ref, rows)) + _mod_rows(sh_ref, rows)).astype(hn_ref.dtype)


def _next_norm_specs(grp, tm, mod_next, last):
    if last:
        zero_mod = jnp.zeros((SUBLANES, D_MODEL), F32)
        spec = pl.BlockSpec((SUBLANES, D_MODEL), lambda i, *_: (0, 0))
        return [spec, spec], (zero_mod, zero_mod)
    return [_mod_spec(grp, 1, tm), _mod_spec(grp, 0, tm)], (mod_next, mod_next)


def ffn_finish(grp, x1, y, mod, g_next, mod_next, last):
    n, d = x1.shape
    tm = grp["tm"]
    row = lambda i: (i, 0)
    next_specs, next_args = _next_norm_specs(grp, tm, mod_next, last)
    return pl.pallas_call(
        _finish_kernel,
        grid=(n // tm,),
        in_specs=[pl.BlockSpec((tm, d), row), pl.BlockSpec((tm, d), row), _mod_spec(grp, 5, tm),
                  pl.BlockSpec((1, d), lambda i: (0, 0))] + next_specs,
        out_specs=[pl.BlockSpec((tm, d), row), pl.BlockSpec((tm, d), row)],
        out_shape=[jax.ShapeDtypeStruct((n, d), F32), jax.ShapeDtypeStruct((n, d), F32 if last else BF16)],
        compiler_params=_cparams(("arbitrary",)),
        name="ffn_finish",
    )(x1, y, mod, g_next.reshape(1, d), *next_args)


def _ffn_kernel(x_ref, h_ref, cmb_ref, wg_ref, wu_ref, wd_ref, gate_ref, g_ref, sc_ref, sh_ref,
                x2_ref, hn_ref, acc_ref):
    e, f = pl.program_id(1), pl.program_id(2)
    rows = x_ref.shape[0]

    @pl.when((e == 0) & (f == 0))
    def _():
        acc_ref[...] = jnp.zeros_like(acc_ref)

    h = h_ref[...]
    a = _silu(_dot(h, wg_ref[0])) * _dot(h, wu_ref[0])
    part = _dot(a.astype(BF16), wd_ref[0])
    cmb = cmb_ref[...]
    lane = lax.broadcasted_iota(I32, cmb.shape, 1)
    w = jnp.sum(jnp.where(lane == e, cmb, 0.0), axis=-1, keepdims=True)
    acc_ref[...] += w * part

    @pl.when((e == pl.num_programs(1) - 1) & (f == pl.num_programs(2) - 1))
    def _():
        x2 = x_ref[...] + _mod_rows(gate_ref, rows) * acc_ref[...]
        x2_ref[...] = x2
        hn = _rms(x2) * g_ref[...]
        hn_ref[...] = (hn * (1.0 + _mod_rows(sc_ref, rows)) + _mod_rows(sh_ref, rows)).astype(hn_ref.dtype)


def ffn_block(grp, x1, h2, combine, wg, wu, wd, mod, g_next, mod_next, last):
    n, d = x1.shape
    n_exp, _, ff = wg.shape
    tm = grp["tm"]
    tf = 1408 if ff % 1408 == 0 else ff
    row = lambda i, e, f: (i, 0)
    if last:
        zero_mod = jnp.zeros((SUBLANES, d), F32)
        nspec = pl.BlockSpec((SUBLANES, d), lambda i, e, f: (0, 0))
        next_specs, next_args = [nspec, nspec], (zero_mod, zero_mod)
    else:
        next_specs, next_args = [_mod_spec(grp, 1, tm), _mod_spec(grp, 0, tm)], (mod_next, mod_next)
    return pl.pallas_call(
        _ffn_kernel,
        grid=(n // tm, n_exp, ff // tf),
        in_specs=[pl.BlockSpec((tm, d), row), pl.BlockSpec((tm, d), row), pl.BlockSpec((tm, LANES), row),
                  pl.BlockSpec((1, d, tf), lambda i, e, f: (e, 0, f)),
                  pl.BlockSpec((1, d, tf), lambda i, e, f: (e, 0, f)),
                  pl.BlockSpec((1, tf, d), lambda i, e, f: (e, f, 0)),
                  _mod_spec(grp, 5, tm),
                  pl.BlockSpec((1, d), lambda i, e, f: (0, 0))] + next_specs,
        out_specs=[pl.BlockSpec((tm, d), row), pl.BlockSpec((tm, d), row)],
        out_shape=[jax.ShapeDtypeStruct((n, d), F32), jax.ShapeDtypeStruct((n, d), F32 if last else BF16)],
        scratch_shapes=[pltpu.VMEM((tm, d), F32)],
        compiler_params=_cparams(("arbitrary", "arbitrary", "arbitrary")),
        name="ffn_block",
    )(x1, h2, combine, wg, wu, wd, mod, g_next.reshape(1, d), *next_args)


def _moba_kernel(q_ref, k_ref, v_ref, o_ref, kb_ref, vb_ref, km_ref, sel_ref, *, nb):
    qi = pl.program_id(2)
    blk = MOBA_BLOCK
    nbp = sel_ref.shape[1]

    @pl.when(qi == 0)
    def _():
        km = jnp.zeros((HEAD_DIM, LANES), F32)
        lane = lax.broadcasted_iota(I32, (HEAD_DIM, LANES), 1)
        for j in range(nb):
            kt = k_ref[:, j * blk:(j + 1) * blk]
            kb_ref[j] = kt.T.astype(BF16)
            vb_ref[j] = v_ref[:, j * blk:(j + 1) * blk].astype(BF16)
            mean = jnp.sum(kt, axis=1, keepdims=True) * (1.0 / blk)
            km = jnp.where(lane == j, mean, km)
        km_ref[...] = km.astype(BF16)

    g_sz = MOBA_HEADS // MOBA_KV_HEADS
    sub = lax.broadcasted_iota(I32, (nbp, blk), 0)
    qts = []
    for g in range(g_sz):
        q = q_ref[g * HEAD_DIM:(g + 1) * HEAD_DIM, :]
        gate = _dot_tn(km_ref[...], q.astype(BF16))[:nbp]
        cnt = jnp.zeros((nbp, blk), F32)
        for j2 in range(nb):
            row = gate[j2:j2 + 1, :]
            beats = _f01((row > gate) | ((row == gate) & (j2 < sub)))
            cnt = cnt + beats * _f01(j2 < qi)
        sel_ref[g] = _f01((sub < qi) & (cnt < MOBA_TOPK))
        qts.append((q * ATT_SCALE).astype(BF16))

    init = _softmax_init(blk)
    streams = [(g, par) for g in range(g_sz) for par in range(2)]

    def past_pair(i, carry):
        scores, values = [], []
        for g, par in streams:
            j = 2 * i + par
            jc = jnp.minimum(j, qi - 1)
            chosen = sel_ref[g, pl.ds(jc, 1), :] * _f01(j < qi)
            scores.append(_dot(kb_ref[jc], qts[g]) + (chosen - 1.0) * (-NEG_BIG))
            values.append(vb_ref[jc])
        return _softmax_steps(carry, scores, values)

    states = lax.fori_loop(0, (qi + 1) // 2, past_pair, (init,) * len(streams))
    krow = lax.broadcasted_iota(I32, (blk, blk), 0)
    qcol = lax.broadcasted_iota(I32, (blk, blk), 1)
    causal_bias = jnp.where(krow <= qcol, 0.0, NEG_BIG)
    own = _softmax_steps((init,) * g_sz, [_dot(kb_ref[qi], qts[g]) + causal_bias for g in range(g_sz)],
                         [vb_ref[qi]] * g_sz)
    outs = [_softmax_merge([own[g], states[2 * g], states[2 * g + 1]]).T for g in range(g_sz)]
    o_ref[...] = jnp.concatenate(outs, axis=1).astype(o_ref.dtype)


def moba_prompt(grp, proj_t):
    b_sz, seq = grp["B"], grp["L"]
    assert seq % MOBA_BLOCK == 0
    nb = seq // MOBA_BLOCK
    nbp = -(-nb // SUBLANES) * SUBLANES
    g = MOBA_HEADS // MOBA_KV_HEADS
    return pl.pallas_call(
        functools.partial(_moba_kernel, nb=nb),
        grid=(b_sz, MOBA_KV_HEADS, nb),
        in_specs=[pl.BlockSpec((g * HEAD_DIM, MOBA_BLOCK), lambda b, h, i: (T_MQ // (g * HEAD_DIM) + h, b * nb + i)),
                  pl.BlockSpec((HEAD_DIM, seq), lambda b, h, i: (T_MK // HEAD_DIM + h, b)),
                  pl.BlockSpec((HEAD_DIM, seq), lambda b, h, i: (T_MV // HEAD_DIM + h, b))],
        out_specs=pl.BlockSpec((MOBA_BLOCK, g * HEAD_DIM), lambda b, h, i: (b * nb + i, h)),
        out_shape=jax.ShapeDtypeStruct((b_sz * seq, MOBA_HEADS * HEAD_DIM), BF16),
        scratch_shapes=[pltpu.VMEM((nb, MOBA_BLOCK, HEAD_DIM), BF16), pltpu.VMEM((nb, HEAD_DIM, MOBA_BLOCK), BF16),
                        pltpu.VMEM((HEAD_DIM, LANES), BF16), pltpu.VMEM((g, nbp, MOBA_BLOCK), F32)],
        compiler_params=_cparams(("arbitrary", "arbitrary", "arbitrary")),
        name="moba_prompt",
    )(proj_t, proj_t, proj_t)


def _compress_body(get_tile, n_tiles, pe_ref, w1_ref, b1_ref, w2t_ref, b2r_ref, b2c_ref,
                   orow_ref, ot_ref, xt_ref, xb_ref):
    ncp = xt_ref.shape[0]
    half = NSA_CMP_STRIDE
    gpt = LANES // half
    half_w = half * HEAD_DIM
    o_idx = lax.broadcasted_iota(I32, (LANES, LANES), 0)
    i_idx = lax.broadcasted_iota(I32, (LANES, LANES), 1)
    perm = _f01(i_idx == (o_idx % gpt) * half + o_idx // gpt).astype(BF16)
    for sh in range(2 * NSA_KV_HEADS):
        st = sh // NSA_KV_HEADS
        for t in range(n_tiles):
            kt = get_tile(sh, t)
            for x_ref, code in ((xt_ref, pe_ref[st, 0]), (xb_ref, pe_ref[st, 1])):
                rp = _dot_nt(perm, (kt + code).astype(BF16))
                for lp in range(half // 2):
                    pair = [rp[l * gpt:(l + 1) * gpt, :] for l in (2 * lp, 2 * lp + 1)]
                    x_ref[t * gpt:(t + 1) * gpt, lp * LANES:(lp + 1) * LANES] = jnp.concatenate(pair, axis=1)
        first = _dot(xt_ref[...].astype(BF16), w1_ref[st, :half_w, :])
        second = _dot(xb_ref[...].astype(BF16), w1_ref[st, half_w:, :])
        pre = first + pltpu.roll(second, ncp - 1, 0) + b1_ref[st]
        hid = _silu(pre).astype(BF16)
        orow_ref[0, sh] = (_dot_nt(hid, w2t_ref[st]) + b2r_ref[st]).astype(orow_ref.dtype)
        ot_ref[0, sh] = (_dot_nt(w2t_ref[st], hid) + b2c_ref[st]).astype(ot_ref.dtype)


def _compress_prompt_kernel(src_ref, *rest, n_tiles):
    get = lambda sh, t: src_ref[sh * HEAD_DIM:(sh + 1) * HEAD_DIM, t * LANES:(t + 1) * LANES]
    _compress_body(get, n_tiles, *rest)


def _compress_paged_kernel(pt_ref, *rest, n_tiles):
    pages, rest = rest[:n_tiles], rest[n_tiles:]
    get = lambda sh, t: pages[t][0, 0, sh // NSA_KV_HEADS, sh % NSA_KV_HEADS]
    _compress_body(get, n_tiles, *rest)


def _compress_call(kernel, b_sz, n_tiles, src_specs, src_args, cmp_w, prefetch=()):
    pe, w1, b1, w2t, b2 = cmp_w
    ncp = n_tiles * LANES // NSA_CMP_STRIDE
    full = lambda shape: pl.BlockSpec(shape, lambda b, *_: (0,) * len(shape))
    in_specs = src_specs + [full(pe.shape), full(w1.shape), full((2, 1, NSA_CMP_HIDDEN)), full(w2t.shape),
                            full((2, 1, HEAD_DIM)), full((2, HEAD_DIM, 1))]
    nsh = 2 * NSA_KV_HEADS
    out_specs = [pl.BlockSpec((1, nsh, ncp, HEAD_DIM), lambda b, *_: (b, 0, 0, 0)),
                 pl.BlockSpec((1, nsh, HEAD_DIM, ncp), lambda b, *_: (b, 0, 0, 0))]
    out_shape = [jax.ShapeDtypeStruct((b_sz, nsh, ncp, HEAD_DIM), BF16),
                 jax.ShapeDtypeStruct((b_sz, nsh, HEAD_DIM, ncp), BF16)]
    scratch = [pltpu.VMEM((ncp, NSA_CMP_STRIDE * HEAD_DIM), F32)] * 2
    args = src_args + [pe, w1, b1.reshape(2, 1, -1), w2t, b2.reshape(2, 1, -1), b2.reshape(2, -1, 1)]
    return pl.pallas_call(
        functools.partial(kernel, n_tiles=n_tiles),
        grid_spec=pltpu.PrefetchScalarGridSpec(num_scalar_prefetch=len(prefetch), grid=(b_sz,), in_specs=in_specs,
                                               out_specs=out_specs, scratch_shapes=scratch),
        out_shape=out_shape,
        compiler_params=_cparams(("arbitrary",)),
        name="nsa_compress",
    )(*prefetch, *args)


def compress_prompt(grp, proj_t, cmp_w):
    seq = grp["L"]
    rows = 2 * NSA_KV_HEADS * HEAD_DIM
    spec = pl.BlockSpec((rows, seq), lambda b: (T_NKV // rows, b))
    return _compress_call(_compress_prompt_kernel, grp["B"], seq // LANES, [spec], [proj_t], cmp_w)


def compress_paged(grp, cache, layer, page_table, cmp_w):
    n_pages = page_table.shape[1]
    specs = _page_specs((1, 1, 2, NSA_KV_HEADS, HEAD_DIM, LANES), n_pages, layer, (0, 0, 0, 0))
    return _compress_call(_compress_paged_kernel, grp["B"], n_pages, specs, [cache] * n_pages, cmp_w,
                          prefetch=(page_table,))


def _nsa_kernel(q_ref, ks_ref, vs_ref, kw_ref, vw_ref, gt_ref, kc_ref, vct_ref, ov_ref, o_ref,
                ksr_ref, vsb_ref, kwr_ref, vwb_ref, sel_ref, *, seq, ns, nc):
    kvh, qc = pl.program_id(1), pl.program_id(2)
    n_tiles = seq // LANES
    g_sz = NSA_HEADS // NSA_KV_HEADS
    width = g_sz * LANES
    ncp = kc_ref.shape[2]
    nsp = sel_ref.shape[0]

    @pl.when(qc == 0)
    def _():
        for t in range(n_tiles):
            sl = slice(t * LANES, (t + 1) * LANES)
            ksr_ref[t] = ks_ref[:, sl].T.astype(BF16)
            vsb_ref[t] = vs_ref[:, sl].astype(BF16)
            kwr_ref[t] = kw_ref[:, sl].T.astype(BF16)
            vwb_ref[t] = vw_ref[:, sl].astype(BF16)

    q4 = q_ref[...]
    qts = jnp.concatenate([q4[g * HEAD_DIM:(g + 1) * HEAD_DIM, :] for g in range(g_sz)], axis=1)
    qts = (qts * ATT_SCALE).astype(BF16)
    lane_w = lax.broadcasted_iota(I32, (1, width), 1)
    qpos_w = qc * LANES + (lane_w & (LANES - 1))
    qpos = qpos_w[:, :LANES]

    sub = lax.broadcasted_iota(I32, (LANES, LANES), 0)
    init = _softmax_init(width)

    def tile_scores(k_ref, t, allowed):
        bias = jnp.concatenate([(allowed - 1.0) * (-NEG_BIG)] * g_sz, axis=1)
        return _dot(k_ref[t], qts) + bias

    s = _dot(kc_ref[0, 0], qts)
    win_scores, win_values = [], []
    for k in range(NSA_WINDOW // LANES + 1):
        t = qc - k
        tc = jnp.maximum(t, 0)
        dist = qpos - (tc * LANES + sub)
        allowed = _f01((dist >= 0) & (dist <= NSA_WINDOW)) * _f01(t >= 0)
        win_scores.append(tile_scores(kwr_ref, tc, allowed))
        win_values.append(vwb_ref[tc])

    cidx = lax.broadcasted_iota(I32, (ncp, width), 0)
    valid = _f01((cidx * NSA_CMP_STRIDE + (NSA_CMP_LEN - 1) <= qpos_w) & (cidx < nc))
    s = jnp.where(valid > 0.5, s, NEG_BIG)
    p = jnp.exp(s - jnp.max(s, axis=0, keepdims=True)) * valid
    l = jnp.sum(p, axis=0, keepdims=True)
    pb = (p / jnp.where(l > 0.0, l, 1.0)).astype(BF16)
    o_cmp = _dot(vct_ref[0, 0], pb)
    ps = _dot(ov_ref[...], pb)
    o_win = _softmax_merge(list(_softmax_steps((init,) * len(win_scores), win_scores, win_values)))
    p_slc = ps[:, 0:LANES]
    for g in range(1, g_sz):
        p_slc = p_slc + ps[:, g * LANES:(g + 1) * LANES]

    j = lax.broadcasted_iota(I32, (nsp, LANES), 0)
    cur = qpos // NSA_SLC_BLOCK
    forced = (j == 0) | (j == cur) | (j == cur - 1)
    elig = j <= cur
    score = jnp.where(forced, jnp.inf, p_slc)
    score = jnp.where(elig, score, -jnp.inf)
    cnt = jnp.zeros((nsp, LANES), F32)
    for j2 in range(ns):
        row = score[j2:j2 + 1, :]
        cnt = cnt + _f01((row > score) | ((row == score) & (j2 < j)))
    sel_ref[...] = _f01(elig & (cnt < NSA_TOPN))

    def slc_allowed(t, in_range):
        blocks_per_tile = LANES // NSA_SLC_BLOCK
        r0 = sel_ref[pl.ds(blocks_per_tile * t, 1), :]
        r1 = sel_ref[pl.ds(blocks_per_tile * t + 1, 1), :]
        chosen = jnp.where(sub < NSA_SLC_BLOCK, r0, r1)
        return chosen * _f01(t * LANES + sub <= qpos) * in_range

    n_streams = 4

    def slc_group(i, carry):
        scores, values = [], []
        for k in range(n_streams):
            t = i * n_streams + k
            tc = jnp.minimum(t, qc)
            scores.append(tile_scores(ksr_ref, tc, slc_allowed(tc, _f01(t <= qc))))
            values.append(vsb_ref[tc])
        return _softmax_steps(carry, scores, values)

    slc_states = lax.fori_loop(0, (qc + n_streams) // n_streams, slc_group, (init,) * n_streams)
    o_slc = _softmax_merge(list(slc_states))

    outs = []
    for g in range(g_sz):
        head = kvh * g_sz + g
        gates = [_sigmoid(gt_ref[pl.ds(br * NSA_HEADS + head, 1), :]) for br in range(3)]
        sl = slice(g * LANES, (g + 1) * LANES)
        o = gates[0] * o_cmp[:, sl] + gates[1] * o_slc[:, sl] + gates[2] * o_win[:, sl]
        outs.append(o.T)
    o_ref[...] = jnp.concatenate(outs, axis=1).astype(o_ref.dtype)


def _overlap_matrix(ncp, nsp):
    c0 = np.arange(ncp)[:, None] * NSA_CMP_STRIDE
    s0 = np.arange(nsp)[None, :] * NSA_SLC_BLOCK
    return ((c0 < s0 + NSA_SLC_BLOCK) & (c0 + NSA_CMP_LEN > s0)).astype(np.float32)


def nsa_prompt(grp, proj_t, kc_rows, vc_t):
    b_sz, seq = grp["B"], grp["L"]
    assert seq % LANES == 0 and seq >= NSA_CMP_LEN
    nqc = seq // LANES
    g_sz = NSA_HEADS // NSA_KV_HEADS
    ns = seq // NSA_SLC_BLOCK
    nsp = -(-ns // SUBLANES) * SUBLANES
    nc = (seq - NSA_CMP_LEN) // NSA_CMP_STRIDE + 1
    ncp = kc_rows.shape[2]
    overlap_t = jnp.asarray(_overlap_matrix(ncp, nsp).T, BF16)
    kv_spec = lambda st: pl.BlockSpec((HEAD_DIM, seq), lambda b, h, i: (T_NKV // HEAD_DIM + st * NSA_KV_HEADS + h, b))
    tile_rows = pltpu.VMEM((nqc, LANES, HEAD_DIM), BF16)
    tile_cols = pltpu.VMEM((nqc, HEAD_DIM, LANES), BF16)
    return pl.pallas_call(
        functools.partial(_nsa_kernel, seq=seq, ns=ns, nc=nc),
        grid=(b_sz, NSA_KV_HEADS, nqc),
        in_specs=[pl.BlockSpec((g_sz * HEAD_DIM, LANES), lambda b, h, i: (T_NQ // (g_sz * HEAD_DIM) + h, b * nqc + i)),
                  kv_spec(2), kv_spec(3), kv_spec(4), kv_spec(5),
                  pl.BlockSpec((HEAD_DIM, LANES), lambda b, h, i: (T_NGATE // HEAD_DIM, b * nqc + i)),
                  pl.BlockSpec((1, 1, ncp, HEAD_DIM), lambda b, h, i: (b, h, 0, 0)),
                  pl.BlockSpec((1, 1, HEAD_DIM, ncp), lambda b, h, i: (b, NSA_KV_HEADS + h, 0, 0)),
                  pl.BlockSpec((nsp, ncp), lambda b, h, i: (0, 0))],
        out_specs=pl.BlockSpec((LANES, g_sz * HEAD_DIM), lambda b, h, i: (b * nqc + i, h)),
        out_shape=jax.ShapeDtypeStruct((b_sz * seq, NSA_HEADS * HEAD_DIM), BF16),
        scratch_shapes=[tile_rows, tile_cols, tile_rows, tile_cols, pltpu.VMEM((nsp, LANES), F32)],
        compiler_params=_cparams(("arbitrary", "arbitrary", "arbitrary")),
        name="nsa_prompt",
    )(proj_t, proj_t, proj_t, proj_t, proj_t, proj_t, kc_rows, vc_t, overlap_t)


def _ssm_kernel(xbc_ref, z_ref, dtc_ref, dtr_ref, cs_ref, h0_ref, cw_ref, cb_ref, dbr_ref, alr_ref, dbc_ref, alc_ref,
                gs_ref, dsk_ref, y_ref, hf_ref, tail_ref, h_ref, ybuf_ref, *, n_valid):
    c = pl.program_id(1)
    q_len = xbc_ref.shape[0]

    @pl.when(c == 0)
    def _():
        tail_ref[...] = cs_ref[0]
        h_ref[...] = h0_ref[0]

    x = xbc_ref[...]
    tail = tail_ref[...]
    row8 = lax.broadcasted_iota(I32, tail.shape, 0)
    conv = cb_ref[...] + x * cw_ref[SSM_CONV - 1:SSM_CONV, :]
    for k in range(1, SSM_CONV):
        xs = pltpu.roll(x, k, 0)
        first = jnp.where(row8 < k, pltpu.roll(tail, k, 0), xs[:SUBLANES])
        xs = first if q_len == SUBLANES else jnp.concatenate([first, xs[SUBLANES:]], axis=0)
        conv = conv + xs * cw_ref[SSM_CONV - 1 - k:SSM_CONV - k, :]
    tail_ref[...] = x[q_len - SUBLANES:, :]
    act = _silu(conv)
    gn = SSM_GROUPS * SSM_STATE
    b_all = act[:, SSM_INNER:SSM_INNER + gn].astype(BF16)
    c_all = act[:, SSM_INNER + gn:].astype(BF16)

    rows_q = lax.broadcasted_iota(I32, (q_len, LANES), 0)
    dt_c = _softplus(dtc_ref[...] + dbr_ref[...]) * _f01(rows_q < n_valid)
    a_c = dt_c * (-jnp.exp(alr_ref[...]))
    cols_q = lax.broadcasted_iota(I32, (SSM_HEADS, q_len), 1)
    dt_r = _softplus(dtr_ref[0] + dbc_ref[...]) * _f01(cols_q < n_valid)
    a_r = dt_r * (-jnp.exp(alc_ref[...]))
    ti = lax.broadcasted_iota(I32, (q_len, q_len), 0)
    si = lax.broadcasted_iota(I32, (q_len, q_len), 1)
    causal = ti >= si
    acs_c = _dot_exact(_f01(causal), a_c)
    acs_r = _dot_exact(a_r, _f01(si >= ti))

    cb = []
    for gi in range(SSM_GROUPS):
        sl = slice(gi * SSM_STATE, (gi + 1) * SSM_STATE)
        cb.append(_dot_nt(c_all[:, sl], b_all[:, sl]))
    hpg = SSM_HEADS // SSM_GROUPS
    for h in range(SSM_HEADS):
        gi = h // hpg
        sl = slice(gi * SSM_STATE, (gi + 1) * SSM_STATE)
        col = acs_c[:, h:h + 1]
        decay = jnp.where(causal, jnp.exp(col - acs_r[h:h + 1, :]), 0.0)
        xh = act[:, h * SSM_HEAD_DIM:(h + 1) * SSM_HEAD_DIM]
        xdt = xh * dt_c[:, h:h + 1]
        y = _dot((cb[gi] * decay).astype(BF16), xdt.astype(BF16))
        h_prev = h_ref[h]
        y = y + _dot_nt(c_all[:, sl], h_prev.astype(BF16)) * jnp.exp(col)
        last = acs_c[q_len - 1:q_len, h:h + 1]
        upd = _dot_tn((xdt * jnp.exp(last - col)).astype(BF16), b_all[:, sl])
        h_ref[h] = jnp.exp(last) * h_prev + upd
        ybuf_ref[:, h * SSM_HEAD_DIM:(h + 1) * SSM_HEAD_DIM] = y + xh * dsk_ref[h]

    yz = ybuf_ref[...] * _silu(z_ref[...])
    y_ref[...] = (_rms(yz) * gs_ref[...]).astype(y_ref.dtype)

    @pl.when(c == pl.num_programs(1) - 1)
    def _():
        hf_ref[0] = h_ref[...]


def ssm_mixer(b_sz, q_len, n_chunks, n_valid, xbc, xbc_col, z, z_col, dt_cols, dt_rows, conv_state8, h0, ssm_w,
              out_dtype):
    conv_w, conv_b, dt_bias, a_log, d_skip, g_ssm = ssm_w
    tok = lambda col: (lambda b, c: (b * n_chunks + c, col))
    fixed = lambda shape: pl.BlockSpec(shape, lambda b, c: (0,) * len(shape))
    pad_lanes = lambda v: jnp.pad(v.reshape(1, -1), ((0, 0), (0, LANES - v.shape[0])))
    state_shape = (SSM_HEADS, SSM_HEAD_DIM, SSM_STATE)
    return pl.pallas_call(
        functools.partial(_ssm_kernel, n_valid=n_valid),
        grid=(b_sz, n_chunks),
        in_specs=[pl.BlockSpec((q_len, SSM_CONV_DIM), tok(xbc_col)),
                  pl.BlockSpec((q_len, SSM_INNER), tok(z_col)),
                  pl.BlockSpec((q_len, LANES), tok(0)),
                  pl.BlockSpec((1, SSM_HEADS, q_len), lambda b, c: (b * n_chunks + c, 0, 0)),
                  pl.BlockSpec((1, SUBLANES, SSM_CONV_DIM), lambda b, c: (b, 0, 0)),
                  pl.BlockSpec((1,) + state_shape, lambda b, c: (b, 0, 0, 0)),
                  fixed((SSM_CONV, SSM_CONV_DIM)), fixed((1, SSM_CONV_DIM)),
                  fixed((1, LANES)), fixed((1, LANES)), fixed((SSM_HEADS, 1)), fixed((SSM_HEADS, 1)),
                  fixed((1, SSM_INNER)),
                  pl.BlockSpec(memory_space=pltpu.SMEM)],
        out_specs=[pl.BlockSpec((q_len, SSM_INNER), tok(0)),
                   pl.BlockSpec((1,) + state_shape, lambda b, c: (b, 0, 0, 0))],
        out_shape=[jax.ShapeDtypeStruct((b_sz * n_chunks * q_len, SSM_INNER), out_dtype),
                   jax.ShapeDtypeStruct((b_sz,) + state_shape, F32)],
        scratch_shapes=[pltpu.VMEM((SUBLANES, SSM_CONV_DIM), F32), pltpu.VMEM(state_shape, F32),
                        pltpu.VMEM((q_len, SSM_INNER), F32)],
        compiler_params=_cparams(("arbitrary", "arbitrary")),
        name="ssm_mixer",
    )(xbc, z, dt_cols, dt_rows, conv_state8, h0, conv_w, conv_b.reshape(1, -1), pad_lanes(dt_bias), pad_lanes(a_log),
      dt_bias.reshape(-1, 1), a_log.reshape(-1, 1), g_ssm.reshape(1, -1), d_skip)


def _moba_dec_kernel(pt_ref, q_ref, kn_ref, vn_ref, *rest, n_pages, n_new):
    pages, o_ref = rest[:n_pages], rest[n_pages]
    rows = q_ref.shape[2]
    ppb = MOBA_BLOCK // LANES
    nb_past = n_pages // ppb
    lane = lax.broadcasted_iota(I32, (rows, LANES), 1)
    lane_k = lax.broadcasted_iota(I32, (HEAD_DIM, LANES), 1)
    t_row = lax.broadcasted_iota(I32, (rows, SUBLANES), 0) % n_new
    t_col = lax.broadcasted_iota(I32, (rows, SUBLANES), 1)
    new_ok = _f01((t_col <= t_row) & (t_col < n_new))
    for kvh in range(MOBA_KV_HEADS):
        q = q_ref[0, kvh].astype(BF16)
        scores = []
        km = jnp.zeros((HEAD_DIM, LANES), F32)
        for j in range(nb_past):
            ksum = jnp.zeros((HEAD_DIM, LANES), F32)
            for pp in range(ppb):
                kt = pages[j * ppb + pp][0, 0, 0, kvh]
                scores.append(_dot(q, kt.astype(BF16)) * ATT_SCALE)
                ksum = ksum + kt
            mean = jnp.sum(ksum, axis=1, keepdims=True) * (1.0 / MOBA_BLOCK)
            km = jnp.where(lane_k == j, mean, km)
        gate = jnp.where(lane < nb_past, _dot(q, km.astype(BF16)), -jnp.inf)
        cnt = jnp.zeros((rows, LANES), F32)
        for j2 in range(nb_past):
            col = gate[:, j2:j2 + 1]
            cnt = cnt + _f01((col > gate) | ((col == gate) & (j2 < lane)))
        sel = _f01((lane < nb_past) & (cnt < MOBA_TOPK))
        s_new = jnp.where(new_ok > 0.5, _dot_nt(q, kn_ref[0, kvh].astype(BF16)) * ATT_SCALE, NEG_BIG)
        m = jnp.max(s_new, axis=1, keepdims=True)
        chosen = []
        for p in range(n_pages):
            a = sel[:, p // ppb:p // ppb + 1]
            scores[p] = jnp.where(a > 0.5, scores[p], NEG_BIG)
            chosen.append(a)
            m = jnp.maximum(m, jnp.max(scores[p], axis=1, keepdims=True))
        p_new = jnp.exp(s_new - m) * new_ok
        l = jnp.sum(p_new, axis=1, keepdims=True)
        o = _dot(p_new.astype(BF16), vn_ref[0, kvh].astype(BF16))
        for p in range(n_pages):
            pr = jnp.exp(scores[p] - m) * chosen[p]
            l = l + jnp.sum(pr, axis=1, keepdims=True)
            o = o + _dot_nt(pr.astype(BF16), pages[p][0, 0, 1, kvh].astype(BF16))
        o_ref[0, kvh] = o / l


def _page_specs(block, n_pages, layer, block_idx):
    return [pl.BlockSpec(block, functools.partial(lambda b, pt, j: (layer, pt[b, j]) + block_idx, j=j))
            for j in range(n_pages)]


def moba_decode(grp, q_dec, k_new, v_new, cache, layer, page_table):
    b_sz = grp["B"]
    n_pages = page_table.shape[1]
    per_seq = lambda shape: pl.BlockSpec((1,) + shape, lambda b, pt: (b,) + (0,) * len(shape))
    q_shape, n_shape = q_dec.shape[1:], k_new.shape[1:]
    specs = [per_seq(q_shape), per_seq(n_shape), per_seq(n_shape)]
    specs += _page_specs((1, 1, 2, MOBA_KV_HEADS, HEAD_DIM, LANES), n_pages, layer, (0, 0, 0, 0))
    return pl.pallas_call(
        functools.partial(_moba_dec_kernel, n_pages=n_pages, n_new=grp["L"]),
        grid_spec=pltpu.PrefetchScalarGridSpec(num_scalar_prefetch=1, grid=(b_sz,), in_specs=specs,
                                               out_specs=per_seq(q_shape)),
        out_shape=jax.ShapeDtypeStruct(q_dec.shape, F32),
        compiler_params=_cparams(("arbitrary",)),
        name="moba_decode",
    )(page_table, q_dec, k_new, v_new, *([cache] * n_pages))


def _nsa_dec_kernel(pt_ref, q_ref, kn_ref, vn_ref, kwn_ref, vwn_ref, gt_ref, kc_ref, win_ref, ov_ref, ex_ref, *rest,
                    n_pages, n_new, past, ns, nc):
    pages, o_ref = rest[:n_pages], rest[n_pages]
    rows = q_ref.shape[2]
    ncp = kc_ref.shape[2]
    wlen = win_ref.shape[-1]
    t_rowl = lax.broadcasted_iota(I32, (rows, LANES), 0) % n_new
    lane = lax.broadcasted_iota(I32, (rows, LANES), 1)
    t_row8 = lax.broadcasted_iota(I32, (rows, SUBLANES), 0) % n_new
    t_col8 = lax.broadcasted_iota(I32, (rows, SUBLANES), 1)
    new_ok = _f01((t_col8 <= t_row8) & (t_col8 < n_new))

    def softmax_parts(parts):
        m = None
        masked = []
        for s, a in parts:
            s = jnp.where(a > 0.5, s, NEG_BIG)
            masked.append(s)
            mx = jnp.max(s, axis=1, keepdims=True)
            m = mx if m is None else jnp.maximum(m, mx)
        ps = [jnp.exp(s - m) * a for s, (_, a) in zip(masked, parts)]
        l = ps[0].sum(axis=1, keepdims=True)
        for p in ps[1:]:
            l = l + p.sum(axis=1, keepdims=True)
        inv = 1.0 / jnp.where(l > 0.0, l, 1.0)
        return [p * inv for p in ps]

    heads = range(NSA_KV_HEADS)
    qpos = past + t_rowl
    cidx = lax.broadcasted_iota(I32, (rows, ncp), 1)
    qpos_c = past + lax.broadcasted_iota(I32, (rows, ncp), 0) % n_new
    cmp_ok = _f01((cidx * NSA_CMP_STRIDE + (NSA_CMP_LEN - 1) <= qpos_c) & (cidx < nc))
    jw = lax.broadcasted_iota(I32, (rows, wlen), 1)
    tw = lax.broadcasted_iota(I32, (rows, wlen), 0) % n_new
    dist = wlen + tw - jw
    win_okay = _f01((dist >= 0) & (dist <= NSA_WINDOW) & (past - wlen + jw >= 0))

    qs = [q_ref[0, h].astype(BF16) for h in heads]
    s_cmp = [_dot_nt(qs[h], kc_ref[0, h]) * ATT_SCALE for h in heads]
    s_pages = [[_dot(qs[h], pages[p][0, 0, 0, h].astype(BF16)) * ATT_SCALE for p in range(n_pages)] for h in heads]
    s_new = [_dot_nt(qs[h], kn_ref[0, h].astype(BF16)) * ATT_SCALE for h in heads]
    s_win = [_dot(qs[h], win_ref[0, 0, 0, h].astype(BF16)) * ATT_SCALE for h in heads]
    s_wnew = [_dot_nt(qs[h], kwn_ref[0, h].astype(BF16)) * ATT_SCALE for h in heads]

    pbs = [softmax_parts([(s_cmp[h], cmp_ok)])[0].astype(BF16) for h in heads]
    o_cmp = [_dot(pbs[h], kc_ref[0, NSA_KV_HEADS + h]) for h in heads]
    ps_all = [_dot(pbs[h], ov_ref[...]) for h in heads]
    p_win = [softmax_parts([(s_win[h], win_okay), (s_wnew[h], new_ok)]) for h in heads]
    o_win = [_dot_nt(p_win[h][0].astype(BF16), win_ref[0, 0, 1, h].astype(BF16))
             + _dot(p_win[h][1].astype(BF16), vwn_ref[0, h].astype(BF16)) for h in heads]

    sels = []
    for h in heads:
        p_slc = ps_all[h]
        for g in range(1, rows // n_new):
            p_slc = p_slc + pltpu.roll(ps_all[h], g * n_new, 0)
        cur = qpos // NSA_SLC_BLOCK
        forced = (lane == 0) | (lane == cur) | (lane == cur - 1)
        elig = lane <= cur
        score = jnp.where(forced, jnp.inf, p_slc)
        score = jnp.where(elig, score, -jnp.inf)
        cnt = jnp.zeros((rows, LANES), F32)
        for j2 in range(ns):
            col = score[:, j2:j2 + 1]
            cnt = cnt + _f01((col > score) | ((col == score) & (j2 < lane)))
        sels.append(_f01(elig & (cnt < NSA_TOPN)).astype(BF16))
    key_ok = [_dot(sels[h], ex_ref[...]) for h in heads]

    p_slc_all = []
    for h in heads:
        parts = [(s_pages[h][p], key_ok[h][:, p * LANES:(p + 1) * LANES]) for p in range(n_pages)]
        parts.append((s_new[h], new_ok))
        p_slc_all.append([p.astype(BF16) for p in softmax_parts(parts)])
    for h in heads:
        probs = p_slc_all[h]
        o_slc = _dot(probs[-1], vn_ref[0, h].astype(BF16))
        for p in range(n_pages):
            o_slc = o_slc + _dot_nt(probs[p], pages[p][0, 0, 1, h].astype(BF16))
        gates = _sigmoid(gt_ref[0, h])
        o_ref[0, h] = gates[:, 0:1] * o_cmp[h] + gates[:, 1:2] * o_slc + gates[:, 2:3] * o_win[h]


def nsa_decode(grp, q_dec, new_rows, gates_dec, kc_rows, win_state, cache, layer, page_table, past):
    b_sz, n_new = grp["B"], grp["L"]
    n_pages = page_table.shape[1]
    total = past + n_new
    ns = -(-total // NSA_SLC_BLOCK)
    nc = (total - NSA_CMP_LEN) // NSA_CMP_STRIDE + 1
    ncp = kc_rows.shape[2]
    assert ns <= LANES and past % NSA_SLC_BLOCK == 0 and n_new <= NSA_SLC_BLOCK
    assert (nc - 1) * NSA_CMP_STRIDE + NSA_CMP_LEN <= past, "compressed blocks must not reach the new rows"
    overlap = jnp.asarray(_overlap_matrix(ncp, LANES), BF16)
    expand = jnp.asarray((np.arange(past)[None, :] // NSA_SLC_BLOCK == np.arange(LANES)[:, None]), BF16)
    per_seq = lambda shape: pl.BlockSpec((1,) + shape, lambda b, pt: (b,) + (0,) * len(shape))
    fixed = lambda shape: pl.BlockSpec(shape, lambda b, pt: (0,) * len(shape))
    k_new, v_new, kw_new, vw_new = new_rows
    specs = [per_seq(q_dec.shape[1:])] + [per_seq(k_new.shape[1:])] * 4
    specs += [per_seq(gates_dec.shape[1:]), per_seq(kc_rows.shape[1:]),
              pl.BlockSpec((1, 1) + win_state.shape[2:], lambda b, pt: (layer, b, 0, 0, 0, 0)),
              fixed(overlap.shape), fixed(expand.shape)]
    specs += _page_specs((1, 1, 2, NSA_KV_HEADS, HEAD_DIM, LANES), n_pages, layer, (1, 0, 0, 0))
    return pl.pallas_call(
        functools.partial(_nsa_dec_kernel, n_pages=n_pages, n_new=n_new, past=past, ns=ns, nc=nc),
        grid_spec=pltpu.PrefetchScalarGridSpec(num_scalar_prefetch=1, grid=(b_sz,), in_specs=specs,
                                               out_specs=per_seq(q_dec.shape[1:])),
        out_shape=jax.ShapeDtypeStruct(q_dec.shape, F32),
        compiler_params=_cparams(("arbitrary",)),
        name="nsa_decode",
    )(page_table, q_dec, k_new, v_new, kw_new, vw_new, gates_dec, kc_rows, win_state, overlap, expand,
      *([cache] * n_pages))


def _tile(n, prefs):
    for t in prefs:
        if n % t == 0:
            return t
    return n


def _prep_layer_weights(w_in, w_branch, w_out, cmp_w1, cmp_w2):
    wt = jnp.swapaxes(w_in, 1, 2)
    depth = wt.shape[0]
    zeros = lambda r: jnp.zeros((depth, r, wt.shape[2]), wt.dtype)
    w_att = jnp.concatenate([wt[:, 0:1024], wt[:, 2568:3872], wt[:, 2560:2568], zeros(T_ROWS - 2336)], axis=1)
    w_row = jnp.concatenate([wt[:, 1536:2560], wt[:, 1024:1536]], axis=1)
    w_dt = jnp.concatenate([wt[:, 2560:2568], zeros(LANES - SSM_HEADS)], axis=1)
    return (w_att.astype(BF16), w_row.astype(BF16), w_dt.astype(BF16), wt[:, 3872:6944].astype(BF16),
            w_branch.astype(BF16), w_out.astype(BF16), cmp_w1.astype(BF16), jnp.swapaxes(cmp_w2, 2, 3).astype(BF16))


def _heads_first(a, b_sz, n_new, dims, pad_to=None):
    a = a.reshape((b_sz, n_new) + dims + (HEAD_DIM,))
    nd = len(dims)
    a = jnp.transpose(a, (0,) + tuple(range(2, 2 + nd)) + (1, 2 + nd))
    if pad_to is not None and pad_to > n_new:
        a = jnp.pad(a, [(0, 0)] * (1 + nd) + [(0, pad_to - n_new), (0, 0)])
    return a


def _layer_prompt(grp, x, h, mod, mod_next, lw, ffn_w, g_next, last):
    (w_att, w_row, w_dt, w_mgate, w_branch, w_out, cmp_w, ssm_w, g_ffn) = lw
    b_sz, seq, n = grp["B"], grp["L"], grp["N"]
    proj_t = nt_matmul(w_att, h, _tile(T_ROWS, (640, 512)), _tile(n, (1024, 512)))
    row_proj = nt_matmul(h, w_row, grp["tm"], _tile(R_COLS, (1536, 512)))
    dt_cols = nt_matmul(h, w_dt, grp["tm"], LANES)

    moba_out = moba_prompt(grp, proj_t)

    q_len = SSM_CHUNK
    n_chunks = seq // q_len
    dt_rows = proj_t[T_DT:T_DT + SSM_HEADS].reshape(SSM_HEADS, b_sz * n_chunks, q_len).transpose(1, 0, 2)
    ssm_out, new_ssm = ssm_mixer(
        b_sz, q_len, n_chunks, q_len, row_proj, R_XBC // SSM_CONV_DIM, row_proj, R_Z // SSM_INNER, dt_cols, dt_rows,
        jnp.zeros((b_sz, SUBLANES, SSM_CONV_DIM), F32), jnp.zeros((b_sz, SSM_HEADS, SSM_HEAD_DIM, SSM_STATE), F32),
        ssm_w, BF16)

    kc_rows, vc_t = compress_prompt(grp, proj_t, cmp_w)
    nsa_out = nsa_prompt(grp, proj_t, kc_rows, vc_t)

    x1, h2 = merge_branches(grp, x, (moba_out, ssm_out, nsa_out), h, w_mgate, w_branch, w_out, g_ffn, mod)
    x2, hn = _ffn(grp, x1, h2, ffn_w, mod, g_next, mod_next, last)

    def rows_of(lo, hi, dims):
        return proj_t[lo:hi].reshape(dims + (HEAD_DIM, b_sz, seq)).transpose(3, 4, 0, 1, 2)

    new_moba = rows_of(T_MK, T_NQ, (2, MOBA_KV_HEADS))
    new_nsa = rows_of(T_NKV, T_NKV + 4 * NSA_KV_HEADS * HEAD_DIM, (4, NSA_KV_HEADS))
    win_len = grp["win_len"]
    assert seq >= win_len
    new_win = rows_of(T_NKV + 4 * NSA_KV_HEADS * HEAD_DIM, T_NGATE, (2, NSA_KV_HEADS))[:, seq - win_len:]
    new_conv = row_proj[:, R_XBC:R_XBC + SSM_CONV_DIM].reshape(b_sz, seq, SSM_CONV_DIM)[:, seq - (SSM_CONV - 1):]
    return x2, hn, (new_moba, new_nsa, new_win, new_ssm, new_conv)


def _ffn(grp, x1, h2, ffn_w, mod, g_next, mod_next, last):
    if len(ffn_w) == 4:
        w_router_t, wg, wu, wd = ffn_w
        y = moe_experts(grp, h2, moe_router(grp, h2, w_router_t), wg, wu, wd)
        return ffn_finish(grp, x1, y, mod, g_next, mod_next, last)
    wg, wu, wd = ffn_w
    combine = jnp.ones((x1.shape[0], LANES), F32)
    return ffn_block(grp, x1, h2, combine, wg, wu, wd, mod, g_next, mod_next, last)


def _layer_sample(grp, x, h, mod, mod_next, lw, ffn_w, g_next, last, caches):
    (w_att, w_row, w_dt, w_mgate, w_branch, w_out, cmp_w, ssm_w, g_ffn) = lw
    moba_cache, nsa_cache, win_state, win_prev, conv_state, ssm_state, layer, page_table, past = caches
    b_sz, n_new, n = grp["B"], grp["L"], grp["N"]
    att = nt_matmul(h, w_att, grp["tm"], _tile(T_ROWS, (640, 512)))
    row_proj = nt_matmul(h, w_row, grp["tm"], _tile(R_COLS, (1536, 512)))
    dt_cols = nt_matmul(h, w_dt, grp["tm"], LANES)

    g_m = MOBA_HEADS // MOBA_KV_HEADS
    q_dec = _heads_first(att[:, T_MQ:T_MK], b_sz, n_new, (MOBA_KV_HEADS, g_m)).reshape(b_sz, MOBA_KV_HEADS, g_m * n_new, HEAD_DIM)
    k_new = _heads_first(att[:, T_MK:T_MV], b_sz, n_new, (MOBA_KV_HEADS,), SUBLANES)
    v_new = _heads_first(att[:, T_MV:T_NQ], b_sz, n_new, (MOBA_KV_HEADS,), SUBLANES)
    o = moba_decode(grp, q_dec, k_new, v_new, moba_cache, layer, page_table)
    moba_out = o.reshape(b_sz, MOBA_KV_HEADS, g_m, n_new, HEAD_DIM).transpose(0, 3, 1, 2, 4).reshape(n, -1)

    q_len = SUBLANES
    pad_t = lambda a: jnp.pad(a.reshape(b_sz, n_new, -1), ((0, 0), (0, q_len - n_new), (0, 0))).reshape(b_sz * q_len, -1)
    xbc = row_proj[:, R_XBC:R_XBC + SSM_CONV_DIM]
    dt_rows = jnp.pad(att[:, T_DT:T_DT + SSM_HEADS].reshape(b_sz, n_new, SSM_HEADS).transpose(0, 2, 1),
                      ((0, 0), (0, 0), (0, q_len - n_new)))
    conv8 = jnp.pad(conv_state, ((0, 0), (SUBLANES - (SSM_CONV - 1), 0), (0, 0)))
    ssm_pad, new_ssm = ssm_mixer(b_sz, q_len, 1, n_new, pad_t(xbc), 0, pad_t(row_proj[:, R_Z:R_Z + SSM_INNER]), 0,
                                 pad_t(dt_cols), dt_rows, conv8, ssm_state, ssm_w, F32)
    ssm_out = ssm_pad.reshape(b_sz, q_len, SSM_INNER)[:, :n_new].reshape(n, SSM_INNER)

    g_n = NSA_HEADS // NSA_KV_HEADS
    nq_dec = _heads_first(att[:, T_NQ:T_NKV], b_sz, n_new, (NSA_KV_HEADS, g_n)).reshape(b_sz, NSA_KV_HEADS, g_n * n_new, HEAD_DIM)
    sets = _heads_first(att[:, T_NKV:T_NGATE], b_sz, n_new, (6, NSA_KV_HEADS), SUBLANES)
    gates = att[:, T_NGATE:T_NGATE + 3 * NSA_HEADS].reshape(b_sz, n_new, 3, NSA_KV_HEADS, g_n)
    gates = gates.transpose(0, 3, 4, 1, 2).reshape(b_sz, NSA_KV_HEADS, g_n * n_new, 3)
    gates = jnp.pad(gates, ((0, 0), (0, 0), (0, 0), (0, LANES - 3)))
    kc_rows, _ = compress_paged(grp, nsa_cache, layer, page_table, cmp_w)
    o = nsa_decode(grp, nq_dec, (sets[:, 2], sets[:, 3], sets[:, 4], sets[:, 5]), gates, kc_rows, win_state,
                   nsa_cache, layer, page_table, past)
    nsa_out = o.reshape(b_sz, NSA_KV_HEADS, g_n, n_new, HEAD_DIM).transpose(0, 3, 1, 2, 4).reshape(n, -1)

    x1, h2 = merge_branches(grp, x, (moba_out, ssm_out, nsa_out), h, w_mgate, w_branch, w_out, g_ffn, mod)
    x2, hn = _ffn(grp, x1, h2, ffn_w, mod, g_next, mod_next, last)

    new_moba = att[:, T_MK:T_NQ].reshape(b_sz, n_new, 2, MOBA_KV_HEADS, HEAD_DIM)
    new_nsa = att[:, T_NKV:T_NKV + 4 * NSA_KV_HEADS * HEAD_DIM].reshape(b_sz, n_new, 4, NSA_KV_HEADS, HEAD_DIM)
    win_rows = att[:, T_NKV + 4 * NSA_KV_HEADS * HEAD_DIM:T_NGATE].reshape(b_sz, n_new, 2, NSA_KV_HEADS, HEAD_DIM)
    new_conv = jnp.concatenate([conv_state, xbc.reshape(b_sz, n_new, -1)], axis=1)[:, -(SSM_CONV - 1):]
    return x2, hn, (new_moba, new_nsa, win_rows, new_ssm, new_conv)


def kernel(x_prompt, x_sample, c_prompt, c_sample, cache_moba_kv, cache_nsa_kv, state_nsa_win_kv, state_ssm, state_conv, page_table, w_ada, b_ada, g_mix, w_in, conv_w, conv_b, dt_bias, a_log, d_skip, g_ssm, cmp_pe, cmp_w1, cmp_b1, cmp_w2, cmp_b2, w_branch, w_out, g_ffn, w_ffn_gate, w_ffn_up, w_ffn_down, w_router, w_exp_gate, w_exp_up, w_exp_down, g_final):
    bp, seq, d = x_prompt.shape
    bs, n_new, _ = x_sample.shape
    depth = w_in.shape[0]
    n_pages, page = page_table.shape[1], cache_moba_kv.shape[2]
    past = n_pages * page
    win_len = state_nsa_win_kv.shape[2]
    assert page == LANES and past % MOBA_BLOCK == 0 and n_new <= SUBLANES and win_len == min(NSA_WINDOW, past)

    n_p, n_s = bp * seq, bs * n_new
    grp_p = dict(B=bp, L=seq, N=n_p, tm=_tile(seq, (512, 256, 128)), per_token_mod=False, win_len=win_len)
    grp_s = dict(B=bs, L=n_new, N=n_s, tm=_tile(n_s, (512, 256, 128)), per_token_mod=True, win_len=win_len)

    rows = bp + bs
    rows_pad = -(-rows // SUBLANES) * SUBLANES
    c_all = jnp.pad(jnp.concatenate([c_prompt, c_sample], axis=0), ((0, rows_pad - rows), (0, 0)))
    mod_all = ada_modulation(c_all, w_ada, b_ada)
    mod_p = [jnp.repeat(mod_all[l, :bp], SUBLANES, axis=0) for l in range(depth)]
    mod_s = [jnp.repeat(mod_all[l, bp:rows], n_new, axis=0) for l in range(depth)]

    w_att, w_row, w_dt, w_mg, wb, wo, w1, w2t = _prep_layer_weights(w_in, w_branch, w_out, cmp_w1, cmp_w2)
    groups = LANES // NSA_CMP_STRIDE
    pe_codes = cmp_pe.reshape(depth, 2, NSA_CMP_LEN // NSA_CMP_STRIDE, NSA_CMP_STRIDE, HEAD_DIM)
    pe_codes = jnp.swapaxes(jnp.tile(pe_codes, (1, 1, 1, groups, 1)), -1, -2)
    w_router_t = jnp.pad(jnp.swapaxes(w_router, 1, 2), ((0, 0), (0, LANES - N_EXPERTS), (0, 0))).astype(BF16)
    dense_w = (w_ffn_gate.astype(BF16), w_ffn_up.astype(BF16), w_ffn_down.astype(BF16))
    moe_w = (w_exp_gate.astype(BF16), w_exp_up.astype(BF16), w_exp_down.astype(BF16))

    moba_cache = jnp.transpose(cache_moba_kv, (0, 1, 3, 4, 5, 2))
    nsa_cache = jnp.transpose(cache_nsa_kv, (0, 1, 3, 4, 5, 2))
    win_state = jnp.transpose(state_nsa_win_kv, (0, 1, 3, 4, 5, 2))

    xp = x_prompt.reshape(n_p, d)
    xs = x_sample.reshape(n_s, d)
    hp = norm_modulate(grp_p, xp, g_mix[0], mod_p[0])
    hs = norm_modulate(grp_s, xs, g_mix[0], mod_s[0])
    st_p, st_s = [], []
    for l in range(depth):
        last = l == depth - 1
        if l % 2:
            ffn_w = (w_router_t[l // 2],) + tuple(w[l // 2] for w in moe_w)
        else:
            ffn_w = tuple(w[l // 2][None] for w in dense_w)
        cmp_w = (pe_codes[l], w1[l], cmp_b1[l], w2t[l], cmp_b2[l])
        ssm_w = (conv_w[l], conv_b[l], dt_bias[l], a_log[l], d_skip[l], g_ssm[l])
        lw = (w_att[l], w_row[l], w_dt[l], w_mg[l], wb[l], wo[l], cmp_w, ssm_w, g_ffn[l])
        g_next = g_final if last else g_mix[l + 1]
        xp, hp, new_p = _layer_prompt(grp_p, xp, hp, mod_p[l], None if last else mod_p[l + 1], lw, ffn_w, g_next, last)
        caches = (moba_cache, nsa_cache, win_state, state_nsa_win_kv[l], state_conv[l], state_ssm[l], l, page_table, past)
        xs, hs, new_s = _layer_sample(grp_s, xs, hs, mod_s[l], None if last else mod_s[l + 1], lw, ffn_w, g_next, last,
                                      caches)
        st_p.append(new_p)
        st_s.append(new_s)

    stack = lambda sts, k: jnp.stack([s[k] for s in sts])
    y_prompt = hp.reshape(bp, seq, d)
    y_sample = hs.reshape(bs, n_new, d)
    new_win_s = jnp.concatenate([state_nsa_win_kv, stack(st_s, 2)], axis=2)[:, :, -win_len:]
    return (y_prompt, y_sample, stack(st_p, 0), stack(st_s, 0), stack(st_p, 1), stack(st_s, 1), stack(st_p, 2),
            new_win_s, stack(st_p, 3), stack(st_s, 3), stack(st_p, 4), stack(st_s, 4))
```

```python
import functools

import numpy as np
import jax
import jax.numpy as jnp
from jax import lax
from jax.experimental import pallas as pl
from jax.experimental.pallas import tpu as pltpu

F32 = jnp.float32
BF16 = jnp.bfloat16
I32 = jnp.int32

D_MODEL = 1024
HEAD_DIM = 64
MOBA_HEADS, MOBA_KV_HEADS, MOBA_BLOCK, MOBA_TOPK = 8, 4, 256, 3
SSM_HEADS, SSM_HEAD_DIM, SSM_INNER, SSM_GROUPS, SSM_STATE, SSM_CONV, SSM_CHUNK = 8, 64, 512, 2, 128, 4, 128
SSM_CONV_DIM = SSM_INNER + 2 * SSM_GROUPS * SSM_STATE
NSA_HEADS, NSA_KV_HEADS = 8, 2
NSA_CMP_LEN, NSA_CMP_STRIDE, NSA_CMP_HIDDEN = 32, 16, 256
NSA_SLC_BLOCK, NSA_TOPN, NSA_WINDOW = 64, 16, 512
N_BRANCHES, BRANCH_WIDTH = 3, 512
N_EXPERTS = 8
RMS_EPS = 1e-6
NEG_BIG = -1e30
ATT_SCALE = HEAD_DIM ** -0.5

LANES = 128
SUBLANES = 8
VMEM_LIMIT = 56 * 1024 * 1024

T_MQ, T_MK, T_MV, T_NQ, T_NKV, T_NGATE, T_DT = 0, 512, 768, 1024, 1536, 2304, 2328
T_ROWS = 2560
R_XBC, R_Z = 0, 1024
R_COLS = 1536


def _cparams(sem):
    return pltpu.CompilerParams(dimension_semantics=sem, vmem_limit_bytes=VMEM_LIMIT)


def _silu(x):
    return x * (1.0 / (1.0 + jnp.exp(-x)))


def _sigmoid(x):
    return 1.0 / (1.0 + jnp.exp(-x))


def _softplus(x):
    return jnp.maximum(x, 0.0) + jnp.log(1.0 + jnp.exp(-jnp.abs(x)))


def _dot(a, b):
    return jnp.dot(a, b, preferred_element_type=F32)


def _dot_nt(a, b):
    return lax.dot_general(a, b, (((1,), (1,)), ((), ())), preferred_element_type=F32)


def _dot_tn(a, b):
    return lax.dot_general(a, b, (((0,), (0,)), ((), ())), preferred_element_type=F32)


def _dot_exact(a, b):
    return jnp.dot(a, b, preferred_element_type=F32, precision=lax.Precision.HIGHEST)


def _f01(mask):
    return jnp.where(mask, 1.0, 0.0)


def _mod_rows(ref, rows):
    m = ref[...]
    return m if m.shape[0] == rows else m[0:1]


def _rms(x):
    return x * lax.rsqrt(jnp.mean(x * x, axis=-1, keepdims=True) + RMS_EPS)


def _softmax_init(width):
    return (jnp.full((1, width), NEG_BIG, F32), jnp.zeros((1, width), F32), jnp.zeros((HEAD_DIM, width), F32))


def _softmax_steps(states, scores, values):
    partial = []
    for (m, l, acc), s in zip(states, scores):
        m_new = jnp.maximum(m, jnp.max(s, axis=0, keepdims=True))
        alpha = jnp.exp(m - m_new)
        p = jnp.exp(s - m_new)
        partial.append((m_new, alpha * l + jnp.sum(p, axis=0, keepdims=True), alpha * acc, p.astype(BF16)))
    return tuple((m, l, acc + _dot(v_t, p)) for (m, l, acc, p), v_t in zip(partial, values))


def _softmax_merge(states):
    m = states[0][0]
    for st in states[1:]:
        m = jnp.maximum(m, st[0])
    l = jnp.zeros_like(states[0][1])
    acc = jnp.zeros_like(states[0][2])
    for m_s, l_s, acc_s in states:
        w = jnp.exp(m_s - m)
        l = l + w * l_s
        acc = acc + w * acc_s
    return acc / l


def _mod_spec(grp, chunk, tm):
    if grp["per_token_mod"]:
        return pl.BlockSpec((tm, D_MODEL), lambda i, *_: (i, chunk))
    tiles_per_seq = grp["L"] // tm
    return pl.BlockSpec((SUBLANES, D_MODEL), lambda i, *_: (i // tiles_per_seq, chunk))


def _ada_kernel(c_ref, w_ref, b_ref, o_ref):
    a = _silu(c_ref[...]).astype(BF16)
    o_ref[0] = _dot(a, w_ref[0].astype(BF16)) + b_ref[0]


def ada_modulation(c_all, w_ada, b_ada):
    depth, d, n6 = w_ada.shape
    rows = c_all.shape[0]
    tn = 1536
    return pl.pallas_call(
        _ada_kernel,
        grid=(depth, n6 // tn),
        in_specs=[pl.BlockSpec((rows, d), lambda l, j: (0, 0)),
                  pl.BlockSpec((1, d, tn), lambda l, j: (l, 0, j)),
                  pl.BlockSpec((1, 1, tn), lambda l, j: (l, 0, j))],
        out_specs=pl.BlockSpec((1, rows, tn), lambda l, j: (l, 0, j)),
        out_shape=jax.ShapeDtypeStruct((depth, rows, n6), F32),
        compiler_params=_cparams(("arbitrary", "arbitrary")),
        name="ada_modulation",
    )(c_all, w_ada, b_ada.reshape(depth, 1, n6))


def _norm_kernel(x_ref, g_ref, sc_ref, sh_ref, o_ref):
    x = x_ref[...]
    rows = x.shape[0]
    y = _rms(x) * g_ref[...]
    o_ref[...] = (y * (1.0 + _mod_rows(sc_ref, rows)) + _mod_rows(sh_ref, rows)).astype(o_ref.dtype)


def norm_modulate(grp, x, g, mod):
    n, d = x.shape
    tm = grp["tm"]
    return pl.pallas_call(
        _norm_kernel,
        grid=(n // tm,),
        in_specs=[pl.BlockSpec((tm, d), lambda i: (i, 0)),
                  pl.BlockSpec((1, d), lambda i: (0, 0)),
                  _mod_spec(grp, 1, tm), _mod_spec(grp, 0, tm)],
        out_specs=pl.BlockSpec((tm, d), lambda i: (i, 0)),
        out_shape=jax.ShapeDtypeStruct((n, d), BF16),
        compiler_params=_cparams(("arbitrary",)),
        name="norm_modulate",
    )(x, g.reshape(1, d), mod, mod)


def _nt_kernel(a_ref, b_ref, o_ref):
    o_ref[...] = _dot_nt(a_ref[...], b_ref[...])


def nt_matmul(a, b, tm, tn, b_outer=False):
    m, k = a.shape
    n = b.shape[0]
    assert m % tm == 0 and n % tn == 0, (a.shape, b.shape, tm, tn)
    grid = (n // tn, m // tm) if b_outer else (m // tm, n // tn)
    ij = (lambda j, i: (i, j)) if b_outer else (lambda i, j: (i, j))
    return pl.pallas_call(
        _nt_kernel,
        grid=grid,
        in_specs=[pl.BlockSpec((tm, k), lambda *g: (ij(*g)[0], 0)),
                  pl.BlockSpec((tn, k), lambda *g: (ij(*g)[1], 0))],
        out_specs=pl.BlockSpec((tm, tn), lambda *g: ij(*g)),
        out_shape=jax.ShapeDtypeStruct((m, n), F32),
        compiler_params=_cparams(("arbitrary", "arbitrary")),
        name="nt_matmul",
    )(a, b)


def _merge_kernel(x_ref, b0_ref, b1_ref, b2_ref, h_ref, wmg_ref, wb_ref, wo_ref, g_ref, gate_ref, sc_ref, sh_ref,
                  x1_ref, h2_ref):
    rows = x_ref.shape[0]
    merged = jnp.zeros((rows, D_MODEL), F32)
    h = h_ref[...]
    for n, b_ref in enumerate((b0_ref, b1_ref, b2_ref)):
        up = _dot(b_ref[...].astype(BF16), wb_ref[n])
        mgate = _dot_nt(h, wmg_ref[n * D_MODEL:(n + 1) * D_MODEL, :])
        merged = merged + _sigmoid(mgate) * up
    y = _dot(merged.astype(BF16), wo_ref[...])
    x1 = x_ref[...] + _mod_rows(gate_ref, rows) * y
    x1_ref[...] = x1
    h2 = _rms(x1) * g_ref[...]
    h2_ref[...] = (h2 * (1.0 + _mod_rows(sc_ref, rows)) + _mod_rows(sh_ref, rows)).astype(BF16)


def merge_branches(grp, x, branches, h, w_mgate, wb, wo, g_ffn, mod):
    n, d = x.shape
    tm = grp["tm"]
    bw = BRANCH_WIDTH
    row = lambda i: (i, 0)
    fixed2 = lambda i: (0, 0)
    once = pl.Buffered(1)
    return pl.pallas_call(
        _merge_kernel,
        grid=(n // tm,),
        in_specs=[pl.BlockSpec((tm, d), row),
                  pl.BlockSpec((tm, bw), row), pl.BlockSpec((tm, bw), row), pl.BlockSpec((tm, bw), row),
                  pl.BlockSpec((tm, d), row),
                  pl.BlockSpec((N_BRANCHES * d, d), fixed2, pipeline_mode=once),
                  pl.BlockSpec((N_BRANCHES, bw, d), lambda i: (0, 0, 0), pipeline_mode=once),
                  pl.BlockSpec((d, d), fixed2, pipeline_mode=once),
                  pl.BlockSpec((1, d), fixed2),
                  _mod_spec(grp, 2, tm), _mod_spec(grp, 4, tm), _mod_spec(grp, 3, tm)],
        out_specs=[pl.BlockSpec((tm, d), row), pl.BlockSpec((tm, d), row)],
        out_shape=[jax.ShapeDtypeStruct((n, d), F32), jax.ShapeDtypeStruct((n, d), BF16)],
        compiler_params=_cparams(("arbitrary",)),
        name="merge_branches",
    )(x, *branches, h, w_mgate, wb, wo, g_ffn.reshape(1, d), mod, mod, mod)


def _router_kernel(h_ref, w_ref, o_ref, ot_ref, slot_ref, slott_ref):
    logits = _dot_nt(h_ref[...], w_ref[...])
    lane = lax.broadcasted_iota(I32, logits.shape, 1)
    logits = jnp.where(lane < N_EXPERTS, logits, -jnp.inf)
    m1 = jnp.max(logits, axis=-1, keepdims=True)
    i1 = jnp.min(jnp.where(logits == m1, lane, LANES), axis=-1, keepdims=True)
    rest = jnp.where(lane == i1, -jnp.inf, logits)
    m2 = jnp.max(rest, axis=-1, keepdims=True)
    i2 = jnp.min(jnp.where(rest == m2, lane, LANES), axis=-1, keepdims=True)
    e2 = jnp.exp(m2 - m1)
    den = 1.0 + e2
    combine = jnp.where(lane == i1, 1.0 / den, 0.0) + jnp.where(lane == i2, e2 / den, 0.0)
    o_ref[...] = combine
    ot_ref[...] = combine.T
    tm = combine.shape[0]
    routed = combine > 0.0
    earlier = lax.broadcasted_iota(I32, (tm, tm), 1) < lax.broadcasted_iota(I32, (tm, tm), 0)
    before = _dot(_f01(earlier).astype(BF16), _f01(routed).astype(BF16))
    slot = jnp.where(routed, before, -1.0)
    slot_ref[...] = slot
    slott_ref[...] = slot.T


def moe_router(grp, h2, w_router_t):
    n, d = h2.shape
    tm = grp["tm"]
    by_tok = pl.BlockSpec((tm, LANES), lambda i: (i, 0))
    by_exp = pl.BlockSpec((LANES, tm), lambda i: (0, i))
    tok_shape, exp_shape = jax.ShapeDtypeStruct((n, LANES), F32), jax.ShapeDtypeStruct((LANES, n), F32)
    return pl.pallas_call(
        _router_kernel,
        grid=(n // tm,),
        in_specs=[pl.BlockSpec((tm, d), lambda i: (i, 0)), pl.BlockSpec((LANES, d), lambda i: (0, 0))],
        out_specs=[by_tok, by_exp, by_tok, by_exp],
        out_shape=[tok_shape, exp_shape, tok_shape, exp_shape],
        compiler_params=_cparams(("arbitrary",)),
        name="moe_router",
    )(h2, w_router_t)


MOE_UNIT = 64
MOE_MAX_UNITS = 4


def _moe_kernel(cnt_ref, h_ref, cmbt_ref, slot_ref, slott_ref, wg_ref, wu_ref, wd_ref, y_ref):
    s_idx, e, f, j = (pl.program_id(k) for k in range(4))
    tm = h_ref.shape[0]
    i = s_idx * pl.num_programs(3) + j
    w_row = cmbt_ref[pl.ds(e, 1), :]
    slot_row = slott_ref[pl.ds(e, 1), :]
    slots = slot_ref[...]
    lane = lax.broadcasted_iota(I32, slots.shape, 1)
    slot_col = jnp.sum(jnp.where(lane == e, slots, 0.0), axis=-1, keepdims=True)
    h = h_ref[...]
    rows = pl.ds(pl.multiple_of(j * tm, tm), tm)

    @pl.when((e == 0) & (f == 0))
    def _():
        y_ref[rows, :] = jnp.zeros((tm, D_MODEL), F32)

    def run_slots(base, cap):
        base = base.astype(F32)
        slot_r = lax.broadcasted_iota(I32, (cap, tm), 0).astype(F32)
        slot_c = lax.broadcasted_iota(I32, (tm, cap), 1).astype(F32)
        p = _f01(slot_row - base == slot_r)
        pt = _f01(slot_col - base == slot_c).astype(BF16)
        xc = _dot(p.astype(BF16), h).astype(BF16)
        a = _silu(_dot(xc, wg_ref[0])) * _dot(xc, wu_ref[0])
        out = _dot(a.astype(BF16), wd_ref[0])
        out = out * jnp.sum(p * w_row, axis=-1, keepdims=True)
        hi = out.astype(BF16)
        lo = (out - hi.astype(F32)).astype(BF16)
        y_ref[rows, :] += _dot(pt, hi) + _dot(pt, lo)

    n_units = (cnt_ref[i, e] + MOE_UNIT - 1) // MOE_UNIT
    for k in range(1, MOE_MAX_UNITS + 1):
        @pl.when(jnp.minimum(n_units, MOE_MAX_UNITS) == k)
        def _(k=k):
            run_slots(jnp.int32(0), k * MOE_UNIT)

    def overflow(k, carry):
        run_slots((MOE_MAX_UNITS + k) * MOE_UNIT, MOE_UNIT)
        return carry

    lax.fori_loop(0, jnp.maximum(n_units - MOE_MAX_UNITS, 0), overflow, 0)


def moe_experts(grp, h2, routing, wg, wu, wd):
    combine, combine_t, slot, slot_t = routing
    n, d = h2.shape
    n_exp, _, ff = wg.shape
    tm = grp["tm"]
    sup = _tile(n, (2 * tm,))
    n_j = sup // tm
    tf = ff
    counts = jnp.sum((combine[:, :n_exp] > 0.0).reshape(n // tm, tm, n_exp), axis=1).astype(I32)
    tok = lambda s, e, f, j, cnt: (s * n_j + j, 0)
    by_exp = pl.BlockSpec((LANES, tm), lambda s, e, f, j, cnt: (0, s * n_j + j))
    return pl.pallas_call(
        _moe_kernel,
        grid_spec=pltpu.PrefetchScalarGridSpec(
            num_scalar_prefetch=1, grid=(n // sup, n_exp, ff // tf, n_j),
            in_specs=[pl.BlockSpec((tm, d), tok), by_exp, pl.BlockSpec((tm, LANES), tok), by_exp,
                      pl.BlockSpec((1, d, tf), lambda s, e, f, j, cnt: (e, 0, f)),
                      pl.BlockSpec((1, d, tf), lambda s, e, f, j, cnt: (e, 0, f)),
                      pl.BlockSpec((1, tf, d), lambda s, e, f, j, cnt: (e, f, 0))],
            out_specs=pl.BlockSpec((sup, d), lambda s, e, f, j, cnt: (s, 0))),
        out_shape=jax.ShapeDtypeStruct((n, d), F32),
        compiler_params=_cparams(("arbitrary", "arbitrary", "arbitrary", "arbitrary")),
        name="moe_experts",
    )(counts, h2, combine_t, slot, slot_t, wg, wu, wd)


def _finish_kernel(x_ref, y_ref, gate_ref, g_ref, sc_ref, sh_ref, x2_ref, hn_ref):
    rows = x_ref.shape[0]
    x2 = x_ref[...] + _mod_rows(gate_ref, rows) * y_ref[...]
    x2_ref[...] = x2
    hn = _rms(x2) * g_ref[...]
    hn_ref[...] = (hn * (1.0 + _mod_rows(sc_ref, rows)) + _mod_rows(sh_ref, rows)).astype(hn_ref.dtype)


def _next_norm_specs(grp, tm, mod_next, last):
    if last:
        zero_mod = jnp.zeros((SUBLANES, D_MODEL), F32)
        spec = pl.BlockSpec((SUBLANES, D_MODEL), lambda i, *_: (0, 0))
        return [spec, spec], (zero_mod, zero_mod)
    return [_mod_spec(grp, 1, tm), _mod_spec(grp, 0, tm)], (mod_next, mod_next)


def ffn_finish(grp, x1, y, mod, g_next, mod_next, last):
    n, d = x1.shape
    tm = grp["tm"]
    row = lambda i: (i, 0)
    next_specs, next_args = _next_norm_specs(grp, tm, mod_next, last)
    return pl.pallas_call(
        _finish_kernel,
        grid=(n // tm,),
        in_specs=[pl.BlockSpec((tm, d), row), pl.BlockSpec((tm, d), row), _mod_spec(grp, 5, tm),
                  pl.BlockSpec((1, d), lambda i: (0, 0))] + next_specs,
        out_specs=[pl.BlockSpec((tm, d), row), pl.BlockSpec((tm, d), row)],
        out_shape=[jax.ShapeDtypeStruct((n, d), F32), jax.ShapeDtypeStruct((n, d), F32 if last else BF16)],
        compiler_params=_cparams(("arbitrary",)),
        name="ffn_finish",
    )(x1, y, mod, g_next.reshape(1, d), *next_args)


def _ffn_kernel(x_ref, h_ref, cmb_ref, wg_ref, wu_ref, wd_ref, gate_ref, g_ref, sc_ref, sh_ref,
                x2_ref, hn_ref, acc_ref):
    e, f = pl.program_id(1), pl.program_id(2)
    rows = x_ref.shape[0]

    @pl.when((e == 0) & (f == 0))
    def _():
        acc_ref[...] = jnp.zeros_like(acc_ref)

    h = h_ref[...]
    a = _silu(_dot(h, wg_ref[0])) * _dot(h, wu_ref[0])
    part = _dot(a.astype(BF16), wd_ref[0])
    cmb = cmb_ref[...]
    lane = lax.broadcasted_iota(I32, cmb.shape, 1)
    w = jnp.sum(jnp.where(lane == e, cmb, 0.0), axis=-1, keepdims=True)
    acc_ref[...] += w * part

    @pl.when((e == pl.num_programs(1) - 1) & (f == pl.num_programs(2) - 1))
    def _():
        x2 = x_ref[...] + _mod_rows(gate_ref, rows) * acc_ref[...]
        x2_ref[...] = x2
        hn = _rms(x2) * g_ref[...]
        hn_ref[...] = (hn * (1.0 + _mod_rows(sc_ref, rows)) + _mod_rows(sh_ref, rows)).astype(hn_ref.dtype)


def ffn_block(grp, x1, h2, combine, wg, wu, wd, mod, g_next, mod_next, last):
    n, d = x1.shape
    n_exp, _, ff = wg.shape
    tm = grp["tm"]
    tf = 1408 if ff % 1408 == 0 else ff
    row = lambda i, e, f: (i, 0)
    if last:
        zero_mod = jnp.zeros((SUBLANES, d), F32)
        nspec = pl.BlockSpec((SUBLANES, d), lambda i, e, f: (0, 0))
        next_specs, next_args = [nspec, nspec], (zero_mod, zero_mod)
    else:
        next_specs, next_args = [_mod_spec(grp, 1, tm), _mod_spec(grp, 0, tm)], (mod_next, mod_next)
    return pl.pallas_call(
        _ffn_kernel,
        grid=(n // tm, n_exp, ff // tf),
        in_specs=[pl.BlockSpec((tm, d), row), pl.BlockSpec((tm, d), row), pl.BlockSpec((tm, LANES), row),
                  pl.BlockSpec((1, d, tf), lambda i, e, f: (e, 0, f)),
                  pl.BlockSpec((1, d, tf), lambda i, e, f: (e, 0, f)),
                  pl.BlockSpec((1, tf, d), lambda i, e, f: (e, f, 0)),
                  _mod_spec(grp, 5, tm),
                  pl.BlockSpec((1, d), lambda i, e, f: (0, 0))] + next_specs,
        out_specs=[pl.BlockSpec((tm, d), row), pl.BlockSpec((tm, d), row)],
        out_shape=[jax.ShapeDtypeStruct((n, d), F32), jax.ShapeDtypeStruct((n, d), F32 if last else BF16)],
        scratch_shapes=[pltpu.VMEM((tm, d), F32)],
        compiler_params=_cparams(("arbitrary", "arbitrary", "arbitrary")),
        name="ffn_block",
    )(x1, h2, combine, wg, wu, wd, mod, g_next.reshape(1, d), *next_args)


def _moba_kernel(q_ref, k_ref, v_ref, o_ref, kb_ref, vb_ref, km_ref, sel_ref, *, nb):
    qi = pl.program_id(2)
    blk = MOBA_BLOCK
    nbp = sel_ref.shape[1]

    @pl.when(qi == 0)
    def _():
        km = jnp.zeros((HEAD_DIM, LANES), F32)
        lane = lax.broadcasted_iota(I32, (HEAD_DIM, LANES), 1)
        for j in range(nb):
            kt = k_ref[:, j * blk:(j + 1) * blk]
            kb_ref[j] = kt.T.astype(BF16)
            vb_ref[j] = v_ref[:, j * blk:(j + 1) * blk].astype(BF16)
            mean = jnp.sum(kt, axis=1, keepdims=True) * (1.0 / blk)
            km = jnp.where(lane == j, mean, km)
        km_ref[...] = km.astype(BF16)

    g_sz = MOBA_HEADS // MOBA_KV_HEADS
    sub = lax.broadcasted_iota(I32, (nbp, blk), 0)
    qts = []
    for g in range(g_sz):
        q = q_ref[g * HEAD_DIM:(g + 1) * HEAD_DIM, :]
        gate = _dot_tn(km_ref[...], q.astype(BF16))[:nbp]
        cnt = jnp.zeros((nbp, blk), F32)
        for j2 in range(nb):
            row = gate[j2:j2 + 1, :]
            beats = _f01((row > gate) | ((row == gate) & (j2 < sub)))
            cnt = cnt + beats * _f01(j2 < qi)
        sel_ref[g] = _f01((sub < qi) & (cnt < MOBA_TOPK))
        qts.append((q * ATT_SCALE).astype(BF16))

    init = _softmax_init(blk)
    streams = [(g, par) for g in range(g_sz) for par in range(2)]

    def past_pair(i, carry):
        scores, values = [], []
        for g, par in streams:
            j = 2 * i + par
            jc = jnp.minimum(j, qi - 1)
            chosen = sel_ref[g, pl.ds(jc, 1), :] * _f01(j < qi)
            scores.append(_dot(kb_ref[jc], qts[g]) + (chosen - 1.0) * (-NEG_BIG))
            values.append(vb_ref[jc])
        return _softmax_steps(carry, scores, values)

    states = lax.fori_loop(0, (qi + 1) // 2, past_pair, (init,) * len(streams))
    krow = lax.broadcasted_iota(I32, (blk, blk), 0)
    qcol = lax.broadcasted_iota(I32, (blk, blk), 1)
    causal_bias = jnp.where(krow <= qcol, 0.0, NEG_BIG)
    own = _softmax_steps((init,) * g_sz, [_dot(kb_ref[qi], qts[g]) + causal_bias for g in range(g_sz)],
                         [vb_ref[qi]] * g_sz)
    outs = [_softmax_merge([own[g], states[2 * g], states[2 * g + 1]]).T for g in range(g_sz)]
    o_ref[...] = jnp.concatenate(outs, axis=1).astype(o_ref.dtype)


def moba_prompt(grp, proj_t):
    b_sz, seq = grp["B"], grp["L"]
    assert seq % MOBA_BLOCK == 0
    nb = seq // MOBA_BLOCK
    nbp = -(-nb // SUBLANES) * SUBLANES
    g = MOBA_HEADS // MOBA_KV_HEADS
    return pl.pallas_call(
        functools.partial(_moba_kernel, nb=nb),
        grid=(b_sz, MOBA_KV_HEADS, nb),
        in_specs=[pl.BlockSpec((g * HEAD_DIM, MOBA_BLOCK), lambda b, h, i: (T_MQ // (g * HEAD_DIM) + h, b * nb + i)),
                  pl.BlockSpec((HEAD_DIM, seq), lambda b, h, i: (T_MK // HEAD_DIM + h, b)),
                  pl.BlockSpec((HEAD_DIM, seq), lambda b, h, i: (T_MV // HEAD_DIM + h, b))],
        out_specs=pl.BlockSpec((MOBA_BLOCK, g * HEAD_DIM), lambda b, h, i: (b * nb + i, h)),
        out_shape=jax.ShapeDtypeStruct((b_sz * seq, MOBA_HEADS * HEAD_DIM), BF16),
        scratch_shapes=[pltpu.VMEM((nb, MOBA_BLOCK, HEAD_DIM), BF16), pltpu.VMEM((nb, HEAD_DIM, MOBA_BLOCK), BF16),
                        pltpu.VMEM((HEAD_DIM, LANES), BF16), pltpu.VMEM((g, nbp, MOBA_BLOCK), F32)],
        compiler_params=_cparams(("arbitrary", "arbitrary", "arbitrary")),
        name="moba_prompt",
    )(proj_t, proj_t, proj_t)


def _compress_body(get_tile, n_tiles, pe_ref, w1_ref, b1_ref, w2t_ref, b2r_ref, b2c_ref,
                   orow_ref, ot_ref, xt_ref, xb_ref):
    ncp = xt_ref.shape[0]
    half = NSA_CMP_STRIDE
    gpt = LANES // half
    half_w = half * HEAD_DIM
    o_idx = lax.broadcasted_iota(I32, (LANES, LANES), 0)
    i_idx = lax.broadcasted_iota(I32, (LANES, LANES), 1)
    perm = _f01(i_idx == (o_idx % gpt) * half + o_idx // gpt).astype(BF16)
    for sh in range(2 * NSA_KV_HEADS):
        st = sh // NSA_KV_HEADS
        for t in range(n_tiles):
            kt = get_tile(sh, t)
            for x_ref, code in ((xt_ref, pe_ref[st, 0]), (xb_ref, pe_ref[st, 1])):
                rp = _dot_nt(perm, (kt + code).astype(BF16))
                for lp in range(half // 2):
                    pair = [rp[l * gpt:(l + 1) * gpt, :] for l in (2 * lp, 2 * lp + 1)]
                    x_ref[t * gpt:(t + 1) * gpt, lp * LANES:(lp + 1) * LANES] = jnp.concatenate(pair, axis=1)
        first = _dot(xt_ref[...].astype(BF16), w1_ref[st, :half_w, :])
        second = _dot(xb_ref[...].astype(BF16), w1_ref[st, half_w:, :])
        pre = first + pltpu.roll(second, ncp - 1, 0) + b1_ref[st]
        hid = _silu(pre).astype(BF16)
        orow_ref[0, sh] = (_dot_nt(hid, w2t_ref[st]) + b2r_ref[st]).astype(orow_ref.dtype)
        ot_ref[0, sh] = (_dot_nt(w2t_ref[st], hid) + b2c_ref[st]).astype(ot_ref.dtype)


def _compress_prompt_kernel(src_ref, *rest, n_tiles):
    get = lambda sh, t: src_ref[sh * HEAD_DIM:(sh + 1) * HEAD_DIM, t * LANES:(t + 1) * LANES]
    _compress_body(get, n_tiles, *rest)


def _compress_paged_kernel(pt_ref, *rest, n_tiles):
    pages, rest = rest[:n_tiles], rest[n_tiles:]
    get = lambda sh, t: pages[t][0, 0, sh // NSA_KV_HEADS, sh % NSA_KV_HEADS]
    _compress_body(get, n_tiles, *rest)


def _compress_call(kernel, b_sz, n_tiles, src_specs, src_args, cmp_w, prefetch=()):
    pe, w1, b1, w2t, b2 = cmp_w
    ncp = n_tiles * LANES // NSA_CMP_STRIDE
    full = lambda shape: pl.BlockSpec(shape, lambda b, *_: (0,) * len(shape))
    in_specs = src_specs + [full(pe.shape), full(w1.shape), full((2, 1, NSA_CMP_HIDDEN)), full(w2t.shape),
                            full((2, 1, HEAD_DIM)), full((2, HEAD_DIM, 1))]
    nsh = 2 * NSA_KV_HEADS
    out_specs = [pl.BlockSpec((1, nsh, ncp, HEAD_DIM), lambda b, *_: (b, 0, 0, 0)),
                 pl.BlockSpec((1, nsh, HEAD_DIM, ncp), lambda b, *_: (b, 0, 0, 0))]
    out_shape = [jax.ShapeDtypeStruct((b_sz, nsh, ncp, HEAD_DIM), BF16),
                 jax.ShapeDtypeStruct((b_sz, nsh, HEAD_DIM, ncp), BF16)]
    scratch = [pltpu.VMEM((ncp, NSA_CMP_STRIDE * HEAD_DIM), F32)] * 2
    args = src_args + [pe, w1, b1.reshape(2, 1, -1), w2t, b2.reshape(2, 1, -1), b2.reshape(2, -1, 1)]
    return pl.pallas_call(
        functools.partial(kernel, n_tiles=n_tiles),
        grid_spec=pltpu.PrefetchScalarGridSpec(num_scalar_prefetch=len(prefetch), grid=(b_sz,), in_specs=in_specs,
                                               out_specs=out_specs, scratch_shapes=scratch),
        out_shape=out_shape,
        compiler_params=_cparams(("arbitrary",)),
        name="nsa_compress",
    )(*prefetch, *args)


def compress_prompt(grp, proj_t, cmp_w):
    seq = grp["L"]
    rows = 2 * NSA_KV_HEADS * HEAD_DIM
    spec = pl.BlockSpec((rows, seq), lambda b: (T_NKV // rows, b))
    return _compress_call(_compress_prompt_kernel, grp["B"], seq // LANES, [spec], [proj_t], cmp_w)


def compress_paged(grp, cache, layer, page_table, cmp_w):
    n_pages = page_table.shape[1]
    specs = _page_specs((1, 1, 2, NSA_KV_HEADS, HEAD_DIM, LANES), n_pages, layer, (0, 0, 0, 0))
    return _compress_call(_compress_paged_kernel, grp["B"], n_pages, specs, [cache] * n_pages, cmp_w,
                          prefetch=(page_table,))


def _nsa_kernel(q_ref, ks_ref, vs_ref, kw_ref, vw_ref, gt_ref, kc_ref, vct_ref, ov_ref, o_ref,
                ksr_ref, vsb_ref, kwr_ref, vwb_ref, sel_ref, *, seq, ns, nc):
    kvh, qc = pl.program_id(1), pl.program_id(2)
    n_tiles = seq // LANES
    g_sz = NSA_HEADS // NSA_KV_HEADS
    width = g_sz * LANES
    ncp = kc_ref.shape[2]
    nsp = sel_ref.shape[0]

    @pl.when(qc == 0)
    def _():
        for t in range(n_tiles):
            sl = slice(t * LANES, (t + 1) * LANES)
            ksr_ref[t] = ks_ref[:, sl].T.astype(BF16)
            vsb_ref[t] = vs_ref[:, sl].astype(BF16)
            kwr_ref[t] = kw_ref[:, sl].T.astype(BF16)
            vwb_ref[t] = vw_ref[:, sl].astype(BF16)

    q4 = q_ref[...]
    qts = jnp.concatenate([q4[g * HEAD_DIM:(g + 1) * HEAD_DIM, :] for g in range(g_sz)], axis=1)
    qts = (qts * ATT_SCALE).astype(BF16)
    lane_w = lax.broadcasted_iota(I32, (1, width), 1)
    qpos_w = qc * LANES + (lane_w & (LANES - 1))
    qpos = qpos_w[:, :LANES]

    sub = lax.broadcasted_iota(I32, (LANES, LANES), 0)
    init = _softmax_init(width)

    def tile_scores(k_ref, t, allowed):
        bias = jnp.concatenate([(allowed - 1.0) * (-NEG_BIG)] * g_sz, axis=1)
        return _dot(k_ref[t], qts) + bias

    s = _dot(kc_ref[0, 0], qts)
    win_scores, win_values = [], []
    for k in range(NSA_WINDOW // LANES + 1):
        t = qc - k
        tc = jnp.maximum(t, 0)
        dist = qpos - (tc * LANES + sub)
        allowed = _f01((dist >= 0) & (dist <= NSA_WINDOW)) * _f01(t >= 0)
        win_scores.append(tile_scores(kwr_ref, tc, allowed))
        win_values.append(vwb_ref[tc])

    cidx = lax.broadcasted_iota(I32, (ncp, width), 0)
    valid = _f01((cidx * NSA_CMP_STRIDE + (NSA_CMP_LEN - 1) <= qpos_w) & (cidx < nc))
    s = jnp.where(valid > 0.5, s, NEG_BIG)
    p = jnp.exp(s - jnp.max(s, axis=0, keepdims=True)) * valid
    l = jnp.sum(p, axis=0, keepdims=True)
    pb = (p / jnp.where(l > 0.0, l, 1.0)).astype(BF16)
    o_cmp = _dot(vct_ref[0, 0], pb)
    ps = _dot(ov_ref[...], pb)
    o_win = _softmax_merge(list(_softmax_steps((init,) * len(win_scores), win_scores, win_values)))
    p_slc = ps[:, 0:LANES]
    for g in range(1, g_sz):
        p_slc = p_slc + ps[:, g * LANES:(g + 1) * LANES]

    j = lax.broadcasted_iota(I32, (nsp, LANES), 0)
    cur = qpos // NSA_SLC_BLOCK
    forced = (j == 0) | (j == cur) | (j == cur - 1)
    elig = j <= cur
    score = jnp.where(forced, jnp.inf, p_slc)
    score = jnp.where(elig, score, -jnp.inf)
    cnt = jnp.zeros((nsp, LANES), F32)
    for j2 in range(ns):
        row = score[j2:j2 + 1, :]
        cnt = cnt + _f01((row > score) | ((row == score) & (j2 < j)))
    sel_ref[...] = _f01(elig & (cnt < NSA_TOPN))

    def slc_allowed(t, in_range):
        blocks_per_tile = LANES // NSA_SLC_BLOCK
        r0 = sel_ref[pl.ds(blocks_per_tile * t, 1), :]
        r1 = sel_ref[pl.ds(blocks_per_tile * t + 1, 1), :]
        chosen = jnp.where(sub < NSA_SLC_BLOCK, r0, r1)
        return chosen * _f01(t * LANES + sub <= qpos) * in_range

    n_streams = 4

    def slc_group(i, carry):
        scores, values = [], []
        for k in range(n_streams):
            t = i * n_streams + k
            tc = jnp.minimum(t, qc)
            scores.append(tile_scores(ksr_ref, tc, slc_allowed(tc, _f01(t <= qc))))
            values.append(vsb_ref[tc])
        return _softmax_steps(carry, scores, values)

    slc_states = lax.fori_loop(0, (qc + n_streams) // n_streams, slc_group, (init,) * n_streams)
    o_slc = _softmax_merge(list(slc_states))

    outs = []
    for g in range(g_sz):
        head = kvh * g_sz + g
        gates = [_sigmoid(gt_ref[pl.ds(br * NSA_HEADS + head, 1), :]) for br in range(3)]
        sl = slice(g * LANES, (g + 1) * LANES)
        o = gates[0] * o_cmp[:, sl] + gates[1] * o_slc[:, sl] + gates[2] * o_win[:, sl]
        outs.append(o.T)
    o_ref[...] = jnp.concatenate(outs, axis=1).astype(o_ref.dtype)


def _overlap_matrix(ncp, nsp):
    c0 = np.arange(ncp)[:, None] * NSA_CMP_STRIDE
    s0 = np.arange(nsp)[None, :] * NSA_SLC_BLOCK
    return ((c0 < s0 + NSA_SLC_BLOCK) & (c0 + NSA_CMP_LEN > s0)).astype(np.float32)


def nsa_prompt(grp, proj_t, kc_rows, vc_t):
    b_sz, seq = grp["B"], grp["L"]
    assert seq % LANES == 0 and seq >= NSA_CMP_LEN
    nqc = seq // LANES
    g_sz = NSA_HEADS // NSA_KV_HEADS
    ns = seq // NSA_SLC_BLOCK
    nsp = -(-ns // SUBLANES) * SUBLANES
    nc = (seq - NSA_CMP_LEN) // NSA_CMP_STRIDE + 1
    ncp = kc_rows.shape[2]
    overlap_t = jnp.asarray(_overlap_matrix(ncp, nsp).T, BF16)
    kv_spec = lambda st: pl.BlockSpec((HEAD_DIM, seq), lambda b, h, i: (T_NKV // HEAD_DIM + st * NSA_KV_HEADS + h, b))
    tile_rows = pltpu.VMEM((nqc, LANES, HEAD_DIM), BF16)
    tile_cols = pltpu.VMEM((nqc, HEAD_DIM, LANES), BF16)
    return pl.pallas_call(
        functools.partial(_nsa_kernel, seq=seq, ns=ns, nc=nc),
        grid=(b_sz, NSA_KV_HEADS, nqc),
        in_specs=[pl.BlockSpec((g_sz * HEAD_DIM, LANES), lambda b, h, i: (T_NQ // (g_sz * HEAD_DIM) + h, b * nqc + i)),
                  kv_spec(2), kv_spec(3), kv_spec(4), kv_spec(5),
                  pl.BlockSpec((HEAD_DIM, LANES), lambda b, h, i: (T_NGATE // HEAD_DIM, b * nqc + i)),
                  pl.BlockSpec((1, 1, ncp, HEAD_DIM), lambda b, h, i: (b, h, 0, 0)),
                  pl.BlockSpec((1, 1, HEAD_DIM, ncp), lambda b, h, i: (b, NSA_KV_HEADS + h, 0, 0)),
                  pl.BlockSpec((nsp, ncp), lambda b, h, i: (0, 0))],
        out_specs=pl.BlockSpec((LANES, g_sz * HEAD_DIM), lambda b, h, i: (b * nqc + i, h)),
        out_shape=jax.ShapeDtypeStruct((b_sz * seq, NSA_HEADS * HEAD_DIM), BF16),
        scratch_shapes=[tile_rows, tile_cols, tile_rows, tile_cols, pltpu.VMEM((nsp, LANES), F32)],
        compiler_params=_cparams(("arbitrary", "arbitrary", "arbitrary")),
        name="nsa_prompt",
    )(proj_t, proj_t, proj_t, proj_t, proj_t, proj_t, kc_rows, vc_t, overlap_t)


def _ssm_kernel(xbc_ref, z_ref, dtc_ref, dtr_ref, cs_ref, h0_ref, cw_ref, cb_ref, dbr_ref, alr_ref, dbc_ref, alc_ref,
                gs_ref, dsk_ref, y_ref, hf_ref, tail_ref, h_ref, ybuf_ref, *, n_valid):
    c = pl.program_id(1)
    q_len = xbc_ref.shape[0]

    @pl.when(c == 0)
    def _():
        tail_ref[...] = cs_ref[0]
        h_ref[...] = h0_ref[0]

    x = xbc_ref[...]
    tail = tail_ref[...]
    row8 = lax.broadcasted_iota(I32, tail.shape, 0)
    conv = cb_ref[...] + x * cw_ref[SSM_CONV - 1:SSM_CONV, :]
    for k in range(1, SSM_CONV):
        xs = pltpu.roll(x, k, 0)
        first = jnp.where(row8 < k, pltpu.roll(tail, k, 0), xs[:SUBLANES])
        xs = first if q_len == SUBLANES else jnp.concatenate([first, xs[SUBLANES:]], axis=0)
        conv = conv + xs * cw_ref[SSM_CONV - 1 - k:SSM_CONV - k, :]
    tail_ref[...] = x[q_len - SUBLANES:, :]
    act = _silu(conv)
    gn = SSM_GROUPS * SSM_STATE
    b_all = act[:, SSM_INNER:SSM_INNER + gn].astype(BF16)
    c_all = act[:, SSM_INNER + gn:].astype(BF16)

    rows_q = lax.broadcasted_iota(I32, (q_len, LANES), 0)
    dt_c = _softplus(dtc_ref[...] + dbr_ref[...]) * _f01(rows_q < n_valid)
    a_c = dt_c * (-jnp.exp(alr_ref[...]))
    cols_q = lax.broadcasted_iota(I32, (SSM_HEADS, q_len), 1)
    dt_r = _softplus(dtr_ref[0] + dbc_ref[...]) * _f01(cols_q < n_valid)
    a_r = dt_r * (-jnp.exp(alc_ref[...]))
    ti = lax.broadcasted_iota(I32, (q_len, q_len), 0)
    si = lax.broadcasted_iota(I32, (q_len, q_len), 1)
    causal = ti >= si
    acs_c = _dot_exact(_f01(causal), a_c)
    acs_r = _dot_exact(a_r, _f01(si >= ti))

    cb = []
    for gi in range(SSM_GROUPS):
        sl = slice(gi * SSM_STATE, (gi + 1) * SSM_STATE)
        cb.append(_dot_nt(c_all[:, sl], b_all[:, sl]))
    hpg = SSM_HEADS // SSM_GROUPS
    for h in range(SSM_HEADS):
        gi = h // hpg
        sl = slice(gi * SSM_STATE, (gi + 1) * SSM_STATE)
        col = acs_c[:, h:h + 1]
        decay = jnp.where(causal, jnp.exp(col - acs_r[h:h + 1, :]), 0.0)
        xh = act[:, h * SSM_HEAD_DIM:(h + 1) * SSM_HEAD_DIM]
        xdt = xh * dt_c[:, h:h + 1]
        y = _dot((cb[gi] * decay).astype(BF16), xdt.astype(BF16))
        h_prev = h_ref[h]
        y = y + _dot_nt(c_all[:, sl], h_prev.astype(BF16)) * jnp.exp(col)
        last = acs_c[q_len - 1:q_len, h:h + 1]
        upd = _dot_tn((xdt * jnp.exp(last - col)).astype(BF16), b_all[:, sl])
        h_ref[h] = jnp.exp(last) * h_prev + upd
        ybuf_ref[:, h * SSM_HEAD_DIM:(h + 1) * SSM_HEAD_DIM] = y + xh * dsk_ref[h]

    yz = ybuf_ref[...] * _silu(z_ref[...])
    y_ref[...] = (_rms(yz) * gs_ref[...]).astype(y_ref.dtype)

    @pl.when(c == pl.num_programs(1) - 1)
    def _():
        hf_ref[0] = h_ref[...]


def ssm_mixer(b_sz, q_len, n_chunks, n_valid, xbc, xbc_col, z, z_col, dt_cols, dt_rows, conv_state8, h0, ssm_w,
              out_dtype):
    conv_w, conv_b, dt_bias, a_log, d_skip, g_ssm = ssm_w
    tok = lambda col: (lambda b, c: (b * n_chunks + c, col))
    fixed = lambda shape: pl.BlockSpec(shape, lambda b, c: (0,) * len(shape))
    pad_lanes = lambda v: jnp.pad(v.reshape(1, -1), ((0, 0), (0, LANES - v.shape[0])))
    state_shape = (SSM_HEADS, SSM_HEAD_DIM, SSM_STATE)
    return pl.pallas_call(
        functools.partial(_ssm_kernel, n_valid=n_valid),
        grid=(b_sz, n_chunks),
        in_specs=[pl.BlockSpec((q_len, SSM_CONV_DIM), tok(xbc_col)),
                  pl.BlockSpec((q_len, SSM_INNER), tok(z_col)),
                  pl.BlockSpec((q_len, LANES), tok(0)),
                  pl.BlockSpec((1, SSM_HEADS, q_len), lambda b, c: (b * n_chunks + c, 0, 0)),
                  pl.BlockSpec((1, SUBLANES, SSM_CONV_DIM), lambda b, c: (b, 0, 0)),
                  pl.BlockSpec((1,) + state_shape, lambda b, c: (b, 0, 0, 0)),
                  fixed((SSM_CONV, SSM_CONV_DIM)), fixed((1, SSM_CONV_DIM)),
                  fixed((1, LANES)), fixed((1, LANES)), fixed((SSM_HEADS, 1)), fixed((SSM_HEADS, 1)),
                  fixed((1, SSM_INNER)),
                  pl.BlockSpec(memory_space=pltpu.SMEM)],
        out_specs=[pl.BlockSpec((q_len, SSM_INNER), tok(0)),
                   pl.BlockSpec((1,) + state_shape, lambda b, c: (b, 0, 0, 0))],
        out_shape=[jax.ShapeDtypeStruct((b_sz * n_chunks * q_len, SSM_INNER), out_dtype),
                   jax.ShapeDtypeStruct((b_sz,) + state_shape, F32)],
        scratch_shapes=[pltpu.VMEM((SUBLANES, SSM_CONV_DIM), F32), pltpu.VMEM(state_shape, F32),
                        pltpu.VMEM((q_len, SSM_INNER), F32)],
        compiler_params=_cparams(("arbitrary", "arbitrary")),
        name="ssm_mixer",
    )(xbc, z, dt_cols, dt_rows, conv_state8, h0, conv_w, conv_b.reshape(1, -1), pad_lanes(dt_bias), pad_lanes(a_log),
      dt_bias.reshape(-1, 1), a_log.reshape(-1, 1), g_ssm.reshape(1, -1), d_skip)


def _moba_dec_kernel(pt_ref, q_ref, kn_ref, vn_ref, *rest, n_pages, n_new):
    pages, o_ref = rest[:n_pages], rest[n_pages]
    rows = q_ref.shape[2]
    ppb = MOBA_BLOCK // LANES
    nb_past = n_pages // ppb
    lane = lax.broadcasted_iota(I32, (rows, LANES), 1)
    lane_k = lax.broadcasted_iota(I32, (HEAD_DIM, LANES), 1)
    t_row = lax.broadcasted_iota(I32, (rows, SUBLANES), 0) % n_new
    t_col = lax.broadcasted_iota(I32, (rows, SUBLANES), 1)
    new_ok = _f01((t_col <= t_row) & (t_col < n_new))
    for kvh in range(MOBA_KV_HEADS):
        q = q_ref[0, kvh].astype(BF16)
        scores = []
        km = jnp.zeros((HEAD_DIM, LANES), F32)
        for j in range(nb_past):
            ksum = jnp.zeros((HEAD_DIM, LANES), F32)
            for pp in range(ppb):
                kt = pages[j * ppb + pp][0, 0, 0, kvh]
                scores.append(_dot(q, kt.astype(BF16)) * ATT_SCALE)
                ksum = ksum + kt
            mean = jnp.sum(ksum, axis=1, keepdims=True) * (1.0 / MOBA_BLOCK)
            km = jnp.where(lane_k == j, mean, km)
        gate = jnp.where(lane < nb_past, _dot(q, km.astype(BF16)), -jnp.inf)
        cnt = jnp.zeros((rows, LANES), F32)
        for j2 in range(nb_past):
            col = gate[:, j2:j2 + 1]
            cnt = cnt + _f01((col > gate) | ((col == gate) & (j2 < lane)))
        sel = _f01((lane < nb_past) & (cnt < MOBA_TOPK))
        s_new = jnp.where(new_ok > 0.5, _dot_nt(q, kn_ref[0, kvh].astype(BF16)) * ATT_SCALE, NEG_BIG)
        m = jnp.max(s_new, axis=1, keepdims=True)
        chosen = []
        for p in range(n_pages):
            a = sel[:, p // ppb:p // ppb + 1]
            scores[p] = jnp.where(a > 0.5, scores[p], NEG_BIG)
            chosen.append(a)
            m = jnp.maximum(m, jnp.max(scores[p], axis=1, keepdims=True))
        p_new = jnp.exp(s_new - m) * new_ok
        l = jnp.sum(p_new, axis=1, keepdims=True)
        o = _dot(p_new.astype(BF16), vn_ref[0, kvh].astype(BF16))
        for p in range(n_pages):
            pr = jnp.exp(scores[p] - m) * chosen[p]
            l = l + jnp.sum(pr, axis=1, keepdims=True)
            o = o + _dot_nt(pr.astype(BF16), pages[p][0, 0, 1, kvh].astype(BF16))
        o_ref[0, kvh] = o / l


def _page_specs(block, n_pages, layer, block_idx):
    return [pl.BlockSpec(block, functools.partial(lambda b, pt, j: (layer, pt[b, j]) + block_idx, j=j))
            for j in range(n_pages)]


def moba_decode(grp, q_dec, k_new, v_new, cache, layer, page_table):
    b_sz = grp["B"]
    n_pages = page_table.shape[1]
    per_seq = lambda shape: pl.BlockSpec((1,) + shape, lambda b, pt: (b,) + (0,) * len(shape))
    q_shape, n_shape = q_dec.shape[1:], k_new.shape[1:]
    specs = [per_seq(q_shape), per_seq(n_shape), per_seq(n_shape)]
    specs += _page_specs((1, 1, 2, MOBA_KV_HEADS, HEAD_DIM, LANES), n_pages, layer, (0, 0, 0, 0))
    return pl.pallas_call(
        functools.partial(_moba_dec_kernel, n_pages=n_pages, n_new=grp["L"]),
        grid_spec=pltpu.PrefetchScalarGridSpec(num_scalar_prefetch=1, grid=(b_sz,), in_specs=specs,
                                               out_specs=per_seq(q_shape)),
        out_shape=jax.ShapeDtypeStruct(q_dec.shape, F32),
        compiler_params=_cparams(("arbitrary",)),
        name="moba_decode",
    )(page_table, q_dec, k_new, v_new, *([cache] * n_pages))


def _nsa_dec_kernel(pt_ref, q_ref, kn_ref, vn_ref, kwn_ref, vwn_ref, gt_ref, kc_ref, win_ref, ov_ref, ex_ref, *rest,
                    n_pages, n_new, past, ns, nc):
    pages, o_ref = rest[:n_pages], rest[n_pages]
    rows = q_ref.shape[2]
    ncp = kc_ref.shape[2]
    wlen = win_ref.shape[-1]
    t_rowl = lax.broadcasted_iota(I32, (rows, LANES), 0) % n_new
    lane = lax.broadcasted_iota(I32, (rows, LANES), 1)
    t_row8 = lax.broadcasted_iota(I32, (rows, SUBLANES), 0) % n_new
    t_col8 = lax.broadcasted_iota(I32, (rows, SUBLANES), 1)
    new_ok = _f01((t_col8 <= t_row8) & (t_col8 < n_new))

    def softmax_parts(parts):
        m = None
        masked = []
        for s, a in parts:
            s = jnp.where(a > 0.5, s, NEG_BIG)
            masked.append(s)
            mx = jnp.max(s, axis=1, keepdims=True)
            m = mx if m is None else jnp.maximum(m, mx)
        ps = [jnp.exp(s - m) * a for s, (_, a) in zip(masked, parts)]
        l = ps[0].sum(axis=1, keepdims=True)
        for p in ps[1:]:
            l = l + p.sum(axis=1, keepdims=True)
        inv = 1.0 / jnp.where(l > 0.0, l, 1.0)
        return [p * inv for p in ps]

    heads = range(NSA_KV_HEADS)
    qpos = past + t_rowl
    cidx = lax.broadcasted_iota(I32, (rows, ncp), 1)
    qpos_c = past + lax.broadcasted_iota(I32, (rows, ncp), 0) % n_new
    cmp_ok = _f01((cidx * NSA_CMP_STRIDE + (NSA_CMP_LEN - 1) <= qpos_c) & (cidx < nc))
    jw = lax.broadcasted_iota(I32, (rows, wlen), 1)
    tw = lax.broadcasted_iota(I32, (rows, wlen), 0) % n_new
    dist = wlen + tw - jw
    win_okay = _f01((dist >= 0) & (dist <= NSA_WINDOW) & (past - wlen + jw >= 0))

    qs = [q_ref[0, h].astype(BF16) for h in heads]
    s_cmp = [_dot_nt(qs[h], kc_ref[0, h]) * ATT_SCALE for h in heads]
    s_pages = [[_dot(qs[h], pages[p][0, 0, 0, h].astype(BF16)) * ATT_SCALE for p in range(n_pages)] for h in heads]
    s_new = [_dot_nt(qs[h], kn_ref[0, h].astype(BF16)) * ATT_SCALE for h in heads]
    s_win = [_dot(qs[h], win_ref[0, 0, 0, h].astype(BF16)) * ATT_SCALE for h in heads]
    s_wnew = [_dot_nt(qs[h], kwn_ref[0, h].astype(BF16)) * ATT_SCALE for h in heads]

    pbs = [softmax_parts([(s_cmp[h], cmp_ok)])[0].astype(BF16) for h in heads]
    o_cmp = [_dot(pbs[h], kc_ref[0, NSA_KV_HEADS + h]) for h in heads]
    ps_all = [_dot(pbs[h], ov_ref[...]) for h in heads]
    p_win = [softmax_parts([(s_win[h], win_okay), (s_wnew[h], new_ok)]) for h in heads]
    o_win = [_dot_nt(p_win[h][0].astype(BF16), win_ref[0, 0, 1, h].astype(BF16))
             + _dot(p_win[h][1].astype(BF16), vwn_ref[0, h].astype(BF16)) for h in heads]

    sels = []
    for h in heads:
        p_slc = ps_all[h]
        for g in range(1, rows // n_new):
            p_slc = p_slc + pltpu.roll(ps_all[h], g * n_new, 0)
        cur = qpos // NSA_SLC_BLOCK
        forced = (lane == 0) | (lane == cur) | (lane == cur - 1)
        elig = lane <= cur
        score = jnp.where(forced, jnp.inf, p_slc)
        score = jnp.where(elig, score, -jnp.inf)
        cnt = jnp.zeros((rows, LANES), F32)
        for j2 in range(ns):
            col = score[:, j2:j2 + 1]
            cnt = cnt + _f01((col > score) | ((col == score) & (j2 < lane)))
        sels.append(_f01(elig & (cnt < NSA_TOPN)).astype(BF16))
    key_ok = [_dot(sels[h], ex_ref[...]) for h in heads]

    p_slc_all = []
    for h in heads:
        parts = [(s_pages[h][p], key_ok[h][:, p * LANES:(p + 1) * LANES]) for p in range(n_pages)]
        parts.append((s_new[h], new_ok))
        p_slc_all.append([p.astype(BF16) for p in softmax_parts(parts)])
    for h in heads:
        probs = p_slc_all[h]
        o_slc = _dot(probs[-1], vn_ref[0, h].astype(BF16))
        for p in range(n_pages):
            o_slc = o_slc + _dot_nt(probs[p], pages[p][0, 0, 1, h].astype(BF16))
        gates = _sigmoid(gt_ref[0, h])
        o_ref[0, h] = gates[:, 0:1] * o_cmp[h] + gates[:, 1:2] * o_slc + gates[:, 2:3] * o_win[h]


def nsa_decode(grp, q_dec, new_rows, gates_dec, kc_rows, win_state, cache, layer, page_table, past):
    b_sz, n_new = grp["B"], grp["L"]
    n_pages = page_table.shape[1]
    total = past + n_new
    ns = -(-total // NSA_SLC_BLOCK)
    nc = (total - NSA_CMP_LEN) // NSA_CMP_STRIDE + 1
    ncp = kc_rows.shape[2]
    assert ns <= LANES and past % NSA_SLC_BLOCK == 0 and n_new <= NSA_SLC_BLOCK
    assert (nc - 1) * NSA_CMP_STRIDE + NSA_CMP_LEN <= past, "compressed blocks must not reach the new rows"
    overlap = jnp.asarray(_overlap_matrix(ncp, LANES), BF16)
    expand = jnp.asarray((np.arange(past)[None, :] // NSA_SLC_BLOCK == np.arange(LANES)[:, None]), BF16)
    per_seq = lambda shape: pl.BlockSpec((1,) + shape, lambda b, pt: (b,) + (0,) * len(shape))
    fixed = lambda shape: pl.BlockSpec(shape, lambda b, pt: (0,) * len(shape))
    k_new, v_new, kw_new, vw_new = new_rows
    specs = [per_seq(q_dec.shape[1:])] + [per_seq(k_new.shape[1:])] * 4
    specs += [per_seq(gates_dec.shape[1:]), per_seq(kc_rows.shape[1:]),
              pl.BlockSpec((1, 1) + win_state.shape[2:], lambda b, pt: (layer, b, 0, 0, 0, 0)),
              fixed(overlap.shape), fixed(expand.shape)]
    specs += _page_specs((1, 1, 2, NSA_KV_HEADS, HEAD_DIM, LANES), n_pages, layer, (1, 0, 0, 0))
    return pl.pallas_call(
        functools.partial(_nsa_dec_kernel, n_pages=n_pages, n_new=n_new, past=past, ns=ns, nc=nc),
        grid_spec=pltpu.PrefetchScalarGridSpec(num_scalar_prefetch=1, grid=(b_sz,), in_specs=specs,
                                               out_specs=per_seq(q_dec.shape[1:])),
        out_shape=jax.ShapeDtypeStruct(q_dec.shape, F32),
        compiler_params=_cparams(("arbitrary",)),
        name="nsa_decode",
    )(page_table, q_dec, k_new, v_new, kw_new, vw_new, gates_dec, kc_rows, win_state, overlap, expand,
      *([cache] * n_pages))


def _tile(n, prefs):
    for t in prefs:
        if n % t == 0:
            return t
    return n


def _prep_layer_weights(w_in, w_branch, w_out, cmp_w1, cmp_w2):
    wt = jnp.swapaxes(w_in, 1, 2)
    depth = wt.shape[0]
    zeros = lambda r: jnp.zeros((depth, r, wt.shape[2]), wt.dtype)
    w_att = jnp.concatenate([wt[:, 0:1024], wt[:, 2568:3872], wt[:, 2560:2568], zeros(T_ROWS - 2336)], axis=1)
    w_row = jnp.concatenate([wt[:, 1536:2560], wt[:, 1024:1536]], axis=1)
    w_dt = jnp.concatenate([wt[:, 2560:2568], zeros(LANES - SSM_HEADS)], axis=1)
    return (w_att.astype(BF16), w_row.astype(BF16), w_dt.astype(BF16), wt[:, 3872:6944].astype(BF16),
            w_branch.astype(BF16), w_out.astype(BF16), cmp_w1.astype(BF16), jnp.swapaxes(cmp_w2, 2, 3).astype(BF16))


def _heads_first(a, b_sz, n_new, dims, pad_to=None):
    a = a.reshape((b_sz, n_new) + dims + (HEAD_DIM,))
    nd = len(dims)
    a = jnp.transpose(a, (0,) + tuple(range(2, 2 + nd)) + (1, 2 + nd))
    if pad_to is not None and pad_to > n_new:
        a = jnp.pad(a, [(0, 0)] * (1 + nd) + [(0, pad_to - n_new), (0, 0)])
    return a


def _layer_prompt(grp, x, h, mod, mod_next, lw, ffn_w, g_next, last):
    (w_att, w_row, w_dt, w_mgate, w_branch, w_out, cmp_w, ssm_w, g_ffn) = lw
    b_sz, seq, n = grp["B"], grp["L"], grp["N"]
    proj_t = nt_matmul(w_att, h, _tile(T_ROWS, (640, 512)), _tile(n, (1024, 512)))
    row_proj = nt_matmul(h, w_row, grp["tm"], _tile(R_COLS, (1536, 512)))
    dt_cols = nt_matmul(h, w_dt, grp["tm"], LANES)

    moba_out = moba_prompt(grp, proj_t)

    q_len = SSM_CHUNK
    n_chunks = seq // q_len
    dt_rows = proj_t[T_DT:T_DT + SSM_HEADS].reshape(SSM_HEADS, b_sz * n_chunks, q_len).transpose(1, 0, 2)
    ssm_out, new_ssm = ssm_mixer(
        b_sz, q_len, n_chunks, q_len, row_proj, R_XBC // SSM_CONV_DIM, row_proj, R_Z // SSM_INNER, dt_cols, dt_rows,
        jnp.zeros((b_sz, SUBLANES, SSM_CONV_DIM), F32), jnp.zeros((b_sz, SSM_HEADS, SSM_HEAD_DIM, SSM_STATE), F32),
        ssm_w, BF16)

    kc_rows, vc_t = compress_prompt(grp, proj_t, cmp_w)
    nsa_out = nsa_prompt(grp, proj_t, kc_rows, vc_t)

    x1, h2 = merge_branches(grp, x, (moba_out, ssm_out, nsa_out), h, w_mgate, w_branch, w_out, g_ffn, mod)
    x2, hn = _ffn(grp, x1, h2, ffn_w, mod, g_next, mod_next, last)

    def rows_of(lo, hi, dims):
        return proj_t[lo:hi].reshape(dims + (HEAD_DIM, b_sz, seq)).transpose(3, 4, 0, 1, 2)

    new_moba = rows_of(T_MK, T_NQ, (2, MOBA_KV_HEADS))
    new_nsa = rows_of(T_NKV, T_NKV + 4 * NSA_KV_HEADS * HEAD_DIM, (4, NSA_KV_HEADS))
    win_len = grp["win_len"]
    assert seq >= win_len
    new_win = rows_of(T_NKV + 4 * NSA_KV_HEADS * HEAD_DIM, T_NGATE, (2, NSA_KV_HEADS))[:, seq - win_len:]
    new_conv = row_proj[:, R_XBC:R_XBC + SSM_CONV_DIM].reshape(b_sz, seq, SSM_CONV_DIM)[:, seq - (SSM_CONV - 1):]
    return x2, hn, (new_moba, new_nsa, new_win, new_ssm, new_conv)


def _ffn(grp, x1, h2, ffn_w, mod, g_next, mod_next, last):
    if len(ffn_w) == 4:
        w_router_t, wg, wu, wd = ffn_w
        y = moe_experts(grp, h2, moe_router(grp, h2, w_router_t), wg, wu, wd)
        return ffn_finish(grp, x1, y, mod, g_next, mod_next, last)
    wg, wu, wd = ffn_w
    combine = jnp.ones((x1.shape[0], LANES), F32)
    return ffn_block(grp, x1, h2, combine, wg, wu, wd, mod, g_next, mod_next, last)


def _layer_sample(grp, x, h, mod, mod_next, lw, ffn_w, g_next, last, caches):
    (w_att, w_row, w_dt, w_mgate, w_branch, w_out, cmp_w, ssm_w, g_ffn) = lw
    moba_cache, nsa_cache, win_state, win_prev, conv_state, ssm_state, layer, page_table, past = caches
    b_sz, n_new, n = grp["B"], grp["L"], grp["N"]
    att = nt_matmul(h, w_att, grp["tm"], _tile(T_ROWS, (640, 512)))
    row_proj = nt_matmul(h, w_row, grp["tm"], _tile(R_COLS, (1536, 512)))
    dt_cols = nt_matmul(h, w_dt, grp["tm"], LANES)

    g_m = MOBA_HEADS // MOBA_KV_HEADS
    q_dec = _heads_first(att[:, T_MQ:T_MK], b_sz, n_new, (MOBA_KV_HEADS, g_m)).reshape(b_sz, MOBA_KV_HEADS, g_m * n_new, HEAD_DIM)
    k_new = _heads_first(att[:, T_MK:T_MV], b_sz, n_new, (MOBA_KV_HEADS,), SUBLANES)
    v_new = _heads_first(att[:, T_MV:T_NQ], b_sz, n_new, (MOBA_KV_HEADS,), SUBLANES)
    o = moba_decode(grp, q_dec, k_new, v_new, moba_cache, layer, page_table)
    moba_out = o.reshape(b_sz, MOBA_KV_HEADS, g_m, n_new, HEAD_DIM).transpose(0, 3, 1, 2, 4).reshape(n, -1)

    q_len = SUBLANES
    pad_t = lambda a: jnp.pad(a.reshape(b_sz, n_new, -1), ((0, 0), (0, q_len - n_new), (0, 0))).reshape(b_sz * q_len, -1)
    xbc = row_proj[:, R_XBC:R_XBC + SSM_CONV_DIM]
    dt_rows = jnp.pad(att[:, T_DT:T_DT + SSM_HEADS].reshape(b_sz, n_new, SSM_HEADS).transpose(0, 2, 1),
                      ((0, 0), (0, 0), (0, q_len - n_new)))
    conv8 = jnp.pad(conv_state, ((0, 0), (SUBLANES - (SSM_CONV - 1), 0), (0, 0)))
    ssm_pad, new_ssm = ssm_mixer(b_sz, q_len, 1, n_new, pad_t(xbc), 0, pad_t(row_proj[:, R_Z:R_Z + SSM_INNER]), 0,
                                 pad_t(dt_cols), dt_rows, conv8, ssm_state, ssm_w, F32)
    ssm_out = ssm_pad.reshape(b_sz, q_len, SSM_INNER)[:, :n_new].reshape(n, SSM_INNER)

    g_n = NSA_HEADS // NSA_KV_HEADS
    nq_dec = _heads_first(att[:, T_NQ:T_NKV], b_sz, n_new, (NSA_KV_HEADS, g_n)).reshape(b_sz, NSA_KV_HEADS, g_n * n_new, HEAD_DIM)
    sets = _heads_first(att[:, T_NKV:T_NGATE], b_sz, n_new, (6, NSA_KV_HEADS), SUBLANES)
    gates = att[:, T_NGATE:T_NGATE + 3 * NSA_HEADS].reshape(b_sz, n_new, 3, NSA_KV_HEADS, g_n)
    gates = gates.transpose(0, 3, 4, 1, 2).reshape(b_sz, NSA_KV_HEADS, g_n * n_new, 3)
    gates = jnp.pad(gates, ((0, 0), (0, 0), (0, 0), (0, LANES - 3)))
    kc_rows, _ = compress_paged(grp, nsa_cache, layer, page_table, cmp_w)
    o = nsa_decode(grp, nq_dec, (sets[:, 2], sets[:, 3], sets[:, 4], sets[:, 5]), gates, kc_rows, win_state,
                   nsa_cache, layer, page_table, past)
    nsa_out = o.reshape(b_sz, NSA_KV_HEADS, g_n, n_new, HEAD_DIM).transpose(0, 3, 1, 2, 4).reshape(n, -1)

    x1, h2 = merge_branches(grp, x, (moba_out, ssm_out, nsa_out), h, w_mgate, w_branch, w_out, g_ffn, mod)
    x2, hn = _ffn(grp, x1, h2, ffn_w, mod, g_next, mod_next, last)

    new_moba = att[:, T_MK:T_NQ].reshape(b_sz, n_new, 2, MOBA_KV_HEADS, HEAD_DIM)
    new_nsa = att[:, T_NKV:T_NKV + 4 * NSA_KV_HEADS * HEAD_DIM].reshape(b_sz, n_new, 4, NSA_KV_HEADS, HEAD_DIM)
    win_rows = att[:, T_NKV + 4 * NSA_KV_HEADS * HEAD_DIM:T_NGATE].reshape(b_sz, n_new, 2, NSA_KV_HEADS, HEAD_DIM)
    new_conv = jnp.concatenate([conv_state, xbc.reshape(b_sz, n_new, -1)], axis=1)[:, -(SSM_CONV - 1):]
    return x2, hn, (new_moba, new_nsa, win_rows, new_ssm, new_conv)


def kernel(x_prompt, x_sample, c_prompt, c_sample, cache_moba_kv, cache_nsa_kv, state_nsa_win_kv, state_ssm, state_conv, page_table, w_ada, b_ada, g_mix, w_in, conv_w, conv_b, dt_bias, a_log, d_skip, g_ssm, cmp_pe, cmp_w1, cmp_b1, cmp_w2, cmp_b2, w_branch, w_out, g_ffn, w_ffn_gate, w_ffn_up, w_ffn_down, w_router, w_exp_gate, w_exp_up, w_exp_down, g_final):
    bp, seq, d = x_prompt.shape
    bs, n_new, _ = x_sample.shape
    depth = w_in.shape[0]
    n_pages, page = page_table.shape[1], cache_moba_kv.shape[2]
    past = n_pages * page
    win_len = state_nsa_win_kv.shape[2]
    assert page == LANES and past % MOBA_BLOCK == 0 and n_new <= SUBLANES and win_len == min(NSA_WINDOW, past)

    n_p, n_s = bp * seq, bs * n_new
    grp_p = dict(B=bp, L=seq, N=n_p, tm=_tile(seq, (512, 256, 128)), per_token_mod=False, win_len=win_len)
    grp_s = dict(B=bs, L=n_new, N=n_s, tm=_tile(n_s, (512, 256, 128)), per_token_mod=True, win_len=win_len)

    rows = bp + bs
    rows_pad = -(-rows // SUBLANES) * SUBLANES
    c_all = jnp.pad(jnp.concatenate([c_prompt, c_sample], axis=0), ((0, rows_pad - rows), (0, 0)))
    mod_all = ada_modulation(c_all, w_ada, b_ada)
    mod_p = [jnp.repeat(mod_all[l, :bp], SUBLANES, axis=0) for l in range(depth)]
    mod_s = [jnp.repeat(mod_all[l, bp:rows], n_new, axis=0) for l in range(depth)]

    w_att, w_row, w_dt, w_mg, wb, wo, w1, w2t = _prep_layer_weights(w_in, w_branch, w_out, cmp_w1, cmp_w2)
    groups = LANES // NSA_CMP_STRIDE
    pe_codes = cmp_pe.reshape(depth, 2, NSA_CMP_LEN // NSA_CMP_STRIDE, NSA_CMP_STRIDE, HEAD_DIM)
    pe_codes = jnp.swapaxes(jnp.tile(pe_codes, (1, 1, 1, groups, 1)), -1, -2)
    w_router_t = jnp.pad(jnp.swapaxes(w_router, 1, 2), ((0, 0), (0, LANES - N_EXPERTS), (0, 0))).astype(BF16)
    dense_w = (w_ffn_gate.astype(BF16), w_ffn_up.astype(BF16), w_ffn_down.astype(BF16))
    moe_w = (w_exp_gate.astype(BF16), w_exp_up.astype(BF16), w_exp_down.astype(BF16))

    moba_cache = jnp.transpose(cache_moba_kv, (0, 1, 3, 4, 5, 2))
    nsa_cache = jnp.transpose(cache_nsa_kv, (0, 1, 3, 4, 5, 2))
    win_state = jnp.transpose(state_nsa_win_kv, (0, 1, 3, 4, 5, 2))

    xp = x_prompt.reshape(n_p, d)
    xs = x_sample.reshape(n_s, d)
    hp = norm_modulate(grp_p, xp, g_mix[0], mod_p[0])
    hs = norm_modulate(grp_s, xs, g_mix[0], mod_s[0])
    st_p, st_s = [], []
    for l in range(depth):
        last = l == depth - 1
        if l % 2:
            ffn_w = (w_router_t[l // 2],) + tuple(w[l // 2] for w in moe_w)
        else:
            ffn_w = tuple(w[l // 2][None] for w in dense_w)
        cmp_w = (pe_codes[l], w1[l], cmp_b1[l], w2t[l], cmp_b2[l])
        ssm_w = (conv_w[l], conv_b[l], dt_bias[l], a_log[l], d_skip[l], g_ssm[l])
        lw = (w_att[l], w_row[l], w_dt[l], w_mg[l], wb[l], wo[l], cmp_w, ssm_w, g_ffn[l])
        g_next = g_final if last else g_mix[l + 1]
        xp, hp, new_p = _layer_prompt(grp_p, xp, hp, mod_p[l], None if last else mod_p[l + 1], lw, ffn_w, g_next, last)
        caches = (moba_cache, nsa_cache, win_state, state_nsa_win_kv[l], state_conv[l], state_ssm[l], l, page_table, past)
        xs, hs, new_s = _layer_sample(grp_s, xs, hs, mod_s[l], None if last else mod_s[l + 1], lw, ffn_w, g_next, last,
                                      caches)
        st_p.append(new_p)
        st_s.append(new_s)

    stack = lambda sts, k: jnp.stack([s[k] for s in sts])
    y_prompt = hp.reshape(bp, seq, d)
    y_sample = hs.reshape(bs, n_new, d)
    new_win_s = jnp.concatenate([state_nsa_win_kv, stack(st_s, 2)], axis=2)[:, :, -win_len:]
    return (y_prompt, y_sample, stack(st_p, 0), stack(st_s, 0), stack(st_p, 1), stack(st_s, 1), stack(st_p, 2),
            new_win_s, stack(st_p, 3), stack(st_s, 3), stack(st_p, 4), stack(st_s, 4))
```
